```python
import math
import jax, jax.numpy as jnp
from jax import lax
import numpy as np

D_MODEL = 4096
BATCH = 4
SEQ = 4096
DEPTH = 1

HEAD_DIM = 128
D_MIX = D_MODEL
N_HEADS = D_MIX // HEAD_DIM
GDN_HEADS = N_HEADS // 2
NSA_HEADS = N_HEADS - GDN_HEADS
NSA_KV_HEADS = max(1, NSA_HEADS // 4)
GDN_DK = GDN_HEADS * HEAD_DIM
NSA_DQ = NSA_HEADS * HEAD_DIM
NSA_DKV = NSA_KV_HEADS * HEAD_DIM
GDN_CONV = 4
GDN_CHUNK = 64
CMP_BLOCK = 32
CMP_STRIDE = 16
SEL_BLOCK = 64
SEL_TOPN = 16
SEL_LOCAL = 2
SEL_QBLOCK = 32
WINDOW = 512
WIN_QBLOCK = 128
WIN_PREV_BLOCKS = -(-WINDOW // WIN_QBLOCK)
D_FF = 4 * D_MODEL
EPS = 1e-6
IN_SIZES = (GDN_DK, GDN_DK, GDN_DK, GDN_DK, GDN_HEADS, GDN_HEADS,
            NSA_DQ, NSA_DKV, NSA_DKV, NSA_DKV, NSA_DKV, NSA_DKV, NSA_DKV, 3 * NSA_HEADS)
D_IN = 4 * GDN_DK + 2 * GDN_HEADS + NSA_DQ + 6 * NSA_DKV + 3 * NSA_HEADS

kernel_name = "hybrid_gdn_nsa_block"


def rms_norm(x, gain):
    xf = x.astype(jnp.float32)
    y = xf * lax.rsqrt(jnp.mean(xf * xf, axis=-1, keepdims=True) + EPS)
    return (y * gain.astype(jnp.float32)).astype(x.dtype)


def l2_normalize(x):
    xf = x.astype(jnp.float32)
    return xf * lax.rsqrt(jnp.sum(xf * xf, axis=-1, keepdims=True) + EPS)


def masked_softmax(s, mask):
    s = jnp.where(mask, s.astype(jnp.float32), -jnp.inf)
    m = jnp.max(s, axis=-1, keepdims=True)
    m = jnp.where(jnp.isfinite(m), m, 0.0)
    e = jnp.exp(s - m)
    d = jnp.sum(e, axis=-1, keepdims=True)
    return e / jnp.where(d > 0, d, 1.0)


def alibi_slopes(n):
    return 2.0 ** (-8.0 * jnp.arange(1, n + 1, dtype=jnp.float32) / n)


def split_columns(proj):
    out, start = [], 0
    for size in IN_SIZES:
        out.append(proj[..., start:start + size])
        start += size
    return out


def causal_depthwise_conv(x, w):
    k = w.shape[0]
    return lax.conv_general_dilated(
        x, w[:, None, :].astype(x.dtype), window_strides=(1,), padding=[(k - 1, 0)],
        dimension_numbers=("NWC", "WIO", "NWC"), feature_group_count=x.shape[-1])


def gated_delta_rule_chunked(q, k, v, g, beta):
    f32 = jnp.float32
    B, H, T, dk = q.shape
    dv = v.shape[-1]
    C = GDN_CHUNK
    N = T // C
    q = q.astype(f32).reshape(B, H, N, C, dk)
    k = k.astype(f32).reshape(B, H, N, C, dk)
    v = v.astype(f32).reshape(B, H, N, C, dv)
    g = g.astype(f32).reshape(B, H, N, C)
    beta = beta.astype(f32).reshape(B, H, N, C)
    gam = jnp.cumsum(g, axis=-1)
    causal = jnp.tril(jnp.ones((C, C), bool))
    strict = jnp.tril(jnp.ones((C, C), bool), -1)
    diff = gam[..., :, None] - gam[..., None, :]
    decay = jnp.where(causal, jnp.exp(jnp.where(causal, diff, 0.0)), 0.0)
    kk = jnp.einsum('bhncd,bhnsd->bhncs', k, k)
    lower = jnp.where(strict, beta[..., :, None] * kk * decay, 0.0)
    a = lower + jnp.eye(C, dtype=f32)
    rhs = jnp.concatenate([v * beta[..., None], k * (beta * jnp.exp(gam))[..., None]], axis=-1)
    sol = lax.linalg.triangular_solve(a, rhs, left_side=True, lower=True, unit_diagonal=True)
    u, w = sol[..., :dv], sol[..., dv:]
    qk = jnp.einsum('bhncd,bhnsd->bhncs', q, k) * decay
    q_dec = q * jnp.exp(gam)[..., None]
    g_last = gam[..., -1]
    k_dec = k * jnp.exp(g_last[..., None] - gam)[..., None]

    def step(S, inp):
        u_n, w_n, qk_n, qd_n, kd_n, gl_n = inp
        v_new = u_n - jnp.einsum('bhcd,bhde->bhce', w_n, S)
        o = jnp.einsum('bhcd,bhde->bhce', qd_n, S) + jnp.einsum('bhcs,bhse->bhce', qk_n, v_new)
        S = S * jnp.exp(gl_n)[..., None, None] + jnp.einsum('bhcd,bhce->bhde', kd_n, v_new)
        return S, o

    xs = tuple(jnp.moveaxis(t, 2, 0) for t in (u, w, qk, q_dec, k_dec, g_last))
    _, o = lax.scan(step, jnp.zeros((B, H, dk, dv), f32), xs)
    return jnp.moveaxis(o, 0, 2).reshape(B, H, T, dv)


def gdn_mixer(q, k, v, z, b_raw, a_raw, conv_w, a_log, dt_bias, norm_w):
    f32 = jnp.float32
    B, T, _ = q.shape
    qkv = jax.nn.silu(causal_depthwise_conv(jnp.concatenate([q, k, v], axis=-1), conv_w))
    q, k, v = jnp.split(qkv, 3, axis=-1)
    heads = lambda t: t.reshape(B, T, GDN_HEADS, HEAD_DIM).transpose(0, 2, 1, 3)
    q = l2_normalize(heads(q)) * (HEAD_DIM ** -0.5)
    k = l2_normalize(heads(k))
    v = heads(v)
    beta = jax.nn.sigmoid(b_raw.astype(f32)).transpose(0, 2, 1)
    g = (-jnp.exp(a_log.astype(f32)) *
         jax.nn.softplus(a_raw.astype(f32) + dt_bias.astype(f32))).transpose(0, 2, 1)
    o = gated_delta_rule_chunked(q, k, v, g, beta).transpose(0, 2, 1, 3)
    o = rms_norm(o, norm_w) * jax.nn.silu(z.reshape(B, T, GDN_HEADS, HEAD_DIM).astype(f32))
    return o.reshape(B, T, GDN_DK).astype(z.dtype)


def compress_blocks(x, pos, w1, w2):
    B, G, T, D = x.shape
    r = CMP_BLOCK // CMP_STRIDE
    n = T // CMP_STRIDE
    pieces = x.reshape(B, G, n, CMP_STRIDE, D)
    blocks = jnp.concatenate([pieces[:, :, i:n - r + 1 + i] for i in range(r)], axis=3)
    blocks = (blocks + pos.astype(x.dtype)).reshape(B, G, n - r + 1, CMP_BLOCK * D)
    return jax.nn.silu(blocks @ w1) @ w2


def cmp_to_sel_overlap(n_cmp, n_sel):
    start = jnp.arange(n_cmp)[:, None] * CMP_STRIDE
    lo = jnp.arange(n_sel)[None, :] * SEL_BLOCK
    return ((start <= lo + SEL_BLOCK - 1) & (start + CMP_BLOCK - 1 >= lo)).astype(jnp.float32)


def selected_attention(qh, ks, vs, sel_idx, slopes):
    B, G, R, T, D = qh.shape
    n = sel_idx.shape[-1]
    nq = T // SEL_QBLOCK
    kb = ks.reshape(B, G, T // SEL_BLOCK, SEL_BLOCK, D)
    vb = vs.reshape(B, G, T // SEL_BLOCK, SEL_BLOCK, D)
    gather = jax.vmap(jax.vmap(lambda blocks, ix: blocks[ix]))
    q_chunks = jnp.moveaxis(qh.reshape(B, G, R, nq, SEL_QBLOCK, D), 3, 0)
    i_chunks = jnp.moveaxis(sel_idx.reshape(B, G, nq, SEL_QBLOCK, n), 2, 0)
    offs = jnp.arange(SEL_BLOCK)
    n_keys = n * SEL_BLOCK

    def one_chunk(args):
        qc, ic, c0 = args
        kg = gather(kb, ic).reshape(B, G, SEL_QBLOCK, n_keys, D)
        vg = gather(vb, ic).reshape(B, G, SEL_QBLOCK, n_keys, D)
        t = c0 * SEL_QBLOCK + jnp.arange(SEL_QBLOCK)
        kpos = (ic[..., None] * SEL_BLOCK + offs).reshape(B, G, SEL_QBLOCK, n_keys)
        dist = (t[:, None] - kpos).astype(jnp.float32)
        s = (jnp.einsum('bgrqd,bgqkd->bgrqk', qc, kg).astype(jnp.float32)
             - slopes[:, :, None, None] * dist[:, :, None])
        p = masked_softmax(s, (dist >= 0)[:, :, None])
        return jnp.einsum('bgrqk,bgqkd->bgrqd', p.astype(vg.dtype), vg)

    o = lax.map(one_chunk, (q_chunks, i_chunks, jnp.arange(nq)))
    return jnp.moveaxis(o, 0, 3).reshape(B, G, R, T, D)


def window_attention(qh, kw, vw, slopes):
    B, G, R, T, D = qh.shape
    qb_len = WIN_QBLOCK
    nb = T // qb_len
    npv = WIN_PREV_BLOCKS
    pad = npv * qb_len
    kp = jnp.pad(kw, ((0, 0), (0, 0), (pad, 0), (0, 0))).reshape(B, G, nb + npv, qb_len, D)
    vp = jnp.pad(vw, ((0, 0), (0, 0), (pad, 0), (0, 0))).reshape(B, G, nb + npv, qb_len, D)
    kband = jnp.concatenate([kp[:, :, i:i + nb] for i in range(npv + 1)], axis=3)
    vband = jnp.concatenate([vp[:, :, i:i + nb] for i in range(npv + 1)], axis=3)
    qb = qh.reshape(B, G, R, nb, qb_len, D)
    t = jnp.arange(T).reshape(nb, qb_len)
    kpos = jnp.arange(nb)[:, None] * qb_len - pad + jnp.arange((npv + 1) * qb_len)[None, :]
    dist = t[:, :, None] - kpos[:, None, :]
    mask = (dist >= 0) & (dist < WINDOW) & (kpos[:, None, :] >= 0)
    s = (jnp.einsum('bgrnqd,bgnkd->bgrnqk', qb, kband).astype(jnp.float32)
         - slopes[:, :, None, None, None] * dist.astype(jnp.float32))
    p = masked_softmax(s, mask)
    o = jnp.einsum('bgrnqk,bgnkd->bgrnqd', p.astype(vband.dtype), vband)
    return o.reshape(B, G, R, T, D)


def nsa_mixer(q, k_c, v_c, k_s, v_s, k_w, v_w, gate_raw,
              pos_k, w1_k, w2_k, pos_v, w1_v, w2_v):
    B, T, _ = q.shape
    G = NSA_KV_HEADS
    R = NSA_HEADS // G
    D = HEAD_DIM
    slopes = alibi_slopes(NSA_HEADS).reshape(G, R)
    qh = q.reshape(B, T, G, R, D).transpose(0, 2, 3, 1, 4) * (D ** -0.5)
    kv = lambda t: t.reshape(B, T, G, D).transpose(0, 2, 1, 3)
    t_pos = jnp.arange(T)
    kc = compress_blocks(kv(k_c), pos_k, w1_k, w2_k)
    vc = compress_blocks(kv(v_c), pos_v, w1_v, w2_v)
    n_cmp = kc.shape[2]
    cmp_end = jnp.arange(n_cmp) * CMP_STRIDE + CMP_BLOCK - 1
    dist_c = t_pos[:, None] - cmp_end[None, :]
    s_c = (jnp.einsum('bgrtd,bgnd->bgrtn', qh, kc).astype(jnp.float32)
           - slopes[:, :, None, None] * dist_c.astype(jnp.float32))
    p_c = masked_softmax(s_c, dist_c >= 0)
    o_c = jnp.einsum('bgrtn,bgnd->bgrtd', p_c.astype(vc.dtype), vc)
    n_sel = T // SEL_BLOCK
    imp = jnp.einsum('bgrtn,ns->bgts', p_c, cmp_to_sel_overlap(n_cmp, n_sel))
    blk = jnp.arange(n_sel)[None, :]
    cur = (t_pos // SEL_BLOCK)[:, None]
    forced = (blk == 0) | ((cur - blk) < SEL_LOCAL)
    imp = jnp.where(forced, jnp.inf, imp)
    imp = jnp.where(blk <= cur, imp, -jnp.inf)
    _, sel_idx = lax.top_k(imp, min(SEL_TOPN, n_sel))
    o_s = selected_attention(qh, kv(k_s), kv(v_s), sel_idx, slopes)
    o_w = window_attention(qh, kv(k_w), kv(v_w), slopes)
    gates = jax.nn.sigmoid(gate_raw.astype(jnp.float32)).reshape(B, T, G, R, 3).transpose(0, 2, 3, 1, 4)
    o = gates[..., 0:1] * o_c + gates[..., 1:2] * o_s + gates[..., 2:3] * o_w
    return o.transpose(0, 3, 1, 2, 4).reshape(B, T, NSA_DQ).astype(q.dtype)


def setup_inputs(seed: int = 0) -> dict:
    key = jax.random.key(seed)
    ks = jax.random.split(key, 24)
    f32 = jnp.float32
    nrm = lambda k, shape, scale: jax.random.normal(k, shape, f32) * scale
    L = DEPTH
    x = nrm(ks[0], (BATCH, SEQ, D_MODEL), 1.0)
    c = nrm(ks[1], (BATCH, D_MODEL), 1.0)
    ada_w = nrm(ks[2], (L, D_MODEL, 6 * D_MODEL), D_MODEL ** -0.5)
    ada_b = nrm(ks[3], (L, 6 * D_MODEL), 0.02)
    norm1_w = 1.0 + nrm(ks[4], (L, D_MODEL), 0.02)
    w_in = nrm(ks[5], (L, D_MODEL, D_IN), D_MODEL ** -0.5)
    gdn_conv_w = nrm(ks[6], (L, GDN_CONV, 3 * GDN_DK), GDN_CONV ** -0.5)
    gdn_a_log = jnp.log(jax.random.uniform(ks[7], (L, GDN_HEADS), f32, 1.0, 16.0))
    dt = jnp.exp(jax.random.uniform(ks[8], (L, GDN_HEADS), f32, math.log(1e-3), math.log(1e-1)))
    gdn_dt_bias = dt + jnp.log(-jnp.expm1(-dt))
    gdn_norm_w = 1.0 + nrm(ks[9], (L, HEAD_DIM), 0.02)
    cmp_pos_k = nrm(ks[10], (L, CMP_BLOCK, HEAD_DIM), 0.02)
    cmp_w1_k = nrm(ks[11], (L, CMP_BLOCK * HEAD_DIM, HEAD_DIM), (CMP_BLOCK * HEAD_DIM) ** -0.5)
    cmp_w2_k = nrm(ks[12], (L, HEAD_DIM, HEAD_DIM), HEAD_DIM ** -0.5)
    cmp_pos_v = nrm(ks[13], (L, CMP_BLOCK, HEAD_DIM), 0.02)
    cmp_w1_v = nrm(ks[14], (L, CMP_BLOCK * HEAD_DIM, HEAD_DIM), (CMP_BLOCK * HEAD_DIM) ** -0.5)
    cmp_w2_v = nrm(ks[15], (L, HEAD_DIM, HEAD_DIM), HEAD_DIM ** -0.5)
    w_out = nrm(ks[16], (L, D_MIX, D_MODEL), D_MIX ** -0.5)
    norm2_w = 1.0 + nrm(ks[17], (L, D_MODEL), 0.02)
    w_up = nrm(ks[18], (L, D_MODEL, D_FF), D_MODEL ** -0.5)
    w_down = nrm(ks[19], (L, D_FF, D_MODEL), D_FF ** -0.5)
    final_norm_w = 1.0 + nrm(ks[20], (D_MODEL,), 0.02)
    return {"x": x, "c": c, "ada_w": ada_w, "ada_b": ada_b, "norm1_w": norm1_w, "w_in": w_in,
            "gdn_conv_w": gdn_conv_w, "gdn_a_log": gdn_a_log, "gdn_dt_bias": gdn_dt_bias,
            "gdn_norm_w": gdn_norm_w, "cmp_pos_k": cmp_pos_k, "cmp_w1_k": cmp_w1_k,
            "cmp_w2_k": cmp_w2_k, "cmp_pos_v": cmp_pos_v, "cmp_w1_v": cmp_w1_v,
            "cmp_w2_v": cmp_w2_v, "w_out": w_out, "norm2_w": norm2_w, "w_up": w_up,
            "w_down": w_down, "final_norm_w": final_norm_w}


def reference(x, c, ada_w, ada_b, norm1_w, w_in, gdn_conv_w, gdn_a_log, gdn_dt_bias, gdn_norm_w,
              cmp_pos_k, cmp_w1_k, cmp_w2_k, cmp_pos_v, cmp_w1_v, cmp_w2_v, w_out, norm2_w,
              w_up, w_down, final_norm_w):
    for i in range(DEPTH):
        mod = jax.nn.silu(c) @ ada_w[i] + ada_b[i]
        sh1, sc1, g1, sh2, sc2, g2 = jnp.split(mod[:, None, :], 6, axis=-1)
        h = rms_norm(x, norm1_w[i]) * (1.0 + sc1) + sh1
        (gq, gk, gv, gz, gb, ga, nq, nkc, nvc, nks, nvs, nkw, nvw, ngate) = split_columns(h @ w_in[i])
        o_a = gdn_mixer(gq, gk, gv, gz, gb, ga, gdn_conv_w[i], gdn_a_log[i], gdn_dt_bias[i], gdn_norm_w[i])
        o_b = nsa_mixer(nq, nkc, nvc, nks, nvs, nkw, nvw, ngate,
                        cmp_pos_k[i], cmp_w1_k[i], cmp_w2_k[i], cmp_pos_v[i], cmp_w1_v[i], cmp_w2_v[i])
        mix = jnp.concatenate([o_a, o_b], axis=-1)
        x = x + g1 * (mix @ w_out[i])
        h = rms_norm(x, norm2_w[i]) * (1.0 + sc2) + sh2
        x = x + g2 * (jnp.square(jax.nn.relu(h @ w_up[i])) @ w_down[i])
    return rms_norm(x, final_norm_w)
```

```python
import functools
import math

import jax
import jax.numpy as jnp
from jax import lax
from jax.experimental import pallas as pl
from jax.experimental.pallas import tpu as pltpu

F32 = jnp.float32
BF16 = jnp.bfloat16

HEAD_DIM = 128
GDN_HEADS = 16
NSA_HEADS = 16
NSA_KV_HEADS = 4
GDN_DK = GDN_HEADS * HEAD_DIM
NSA_DQ = NSA_HEADS * HEAD_DIM
NSA_DKV = NSA_KV_HEADS * HEAD_DIM
GDN_CONV = 4
GDN_CHUNK = 64
CMP_BLOCK = 32
CMP_STRIDE = 16
SEL_BLOCK = 64
SEL_TOPN = 16
SEL_LOCAL = 2
SEL_QBLOCK = 32
WINDOW = 512
WIN_QBLOCK = 128
WIN_PREV_BLOCKS = -(-WINDOW // WIN_QBLOCK)
EPS = 1e-6
IN_SIZES = (GDN_DK, GDN_DK, GDN_DK, GDN_DK, GDN_HEADS, GDN_HEADS,
            NSA_DQ, NSA_DKV, NSA_DKV, NSA_DKV, NSA_DKV, NSA_DKV, NSA_DKV, 3 * NSA_HEADS)

VMEM_LIMIT_V7X = 56 * 1024 * 1024
LANES = 128


def _params(sem):
    return pltpu.CompilerParams(dimension_semantics=sem, vmem_limit_bytes=VMEM_LIMIT_V7X)


def _mod_kernel(c_ref, w_ref, b_ref, o_ref):
    c = c_ref[...]
    s = (c * jax.nn.sigmoid(c)).astype(BF16)
    o_ref[...] = jnp.dot(s, w_ref[...].astype(BF16), preferred_element_type=F32) + b_ref[...]


def _modulation(c, w, b):
    bsz, d = c.shape
    n = w.shape[1]
    rows = 8
    cp = jnp.zeros((rows, d), F32).at[:bsz].set(c)
    tn = 512
    out = pl.pallas_call(
        _mod_kernel,
        out_shape=jax.ShapeDtypeStruct((rows, n), F32),
        grid=(n // tn,),
        in_specs=[pl.BlockSpec((rows, d), lambda j: (0, 0)),
                  pl.BlockSpec((d, tn), lambda j: (0, j)),
                  pl.BlockSpec((1, tn), lambda j: (0, j))],
        out_specs=pl.BlockSpec((rows, tn), lambda j: (0, j)),
        compiler_params=_params(("parallel",)),
        name="adaln_mod",
    )(cp, w, b.reshape(1, n))
    return out[:bsz]


def _norm_kernel(x_ref, g_ref, sc_ref, sh_ref, o_ref):
    x = x_ref[...]
    y = x * lax.rsqrt(jnp.mean(x * x, axis=-1, keepdims=True) + EPS) * g_ref[...]
    o_ref[...] = (y * (1.0 + sc_ref[0]) + sh_ref[0]).astype(o_ref.dtype)


def _final_norm_kernel(x_ref, g_ref, o_ref):
    x = x_ref[...]
    o_ref[...] = x * lax.rsqrt(jnp.mean(x * x, axis=-1, keepdims=True) + EPS) * g_ref[...]


def _norm_mod(x2, gain, scale, shift, seq, out_dtype):
    m, d = x2.shape
    tm = 256
    per_b = seq // tm
    bsz = m // seq
    return pl.pallas_call(
        _norm_kernel,
        out_shape=jax.ShapeDtypeStruct((m, d), out_dtype),
        grid=(m // tm,),
        in_specs=[pl.BlockSpec((tm, d), lambda i: (i, 0)),
                  pl.BlockSpec((1, d), lambda i: (0, 0)),
                  pl.BlockSpec((1, 1, d), lambda i: (i // per_b, 0, 0)),
                  pl.BlockSpec((1, 1, d), lambda i: (i // per_b, 0, 0))],
        out_specs=pl.BlockSpec((tm, d), lambda i: (i, 0)),
        compiler_params=_params(("parallel",)),
        name="rmsnorm_mod",
    )(x2, gain.reshape(1, d), scale.reshape(bsz, 1, d), shift.reshape(bsz, 1, d))


def _final_norm(x2, gain):
    m, d = x2.shape
    tm = 256
    return pl.pallas_call(
        _final_norm_kernel,
        out_shape=jax.ShapeDtypeStruct((m, d), F32),
        grid=(m // tm,),
        in_specs=[pl.BlockSpec((tm, d), lambda i: (i, 0)),
                  pl.BlockSpec((1, d), lambda i: (0, 0))],
        out_specs=pl.BlockSpec((tm, d), lambda i: (i, 0)),
        compiler_params=_params(("parallel",)),
        name="final_rmsnorm",
    )(x2, gain.reshape(1, d))


def _mm_kernel(a_ref, w_ref, o_ref, *, relu2):
    acc = jnp.dot(a_ref[...], w_ref[...], preferred_element_type=F32)
    if relu2:
        acc = jnp.square(jnp.maximum(acc, 0.0))
    o_ref[...] = acc.astype(o_ref.dtype)


def _matmul(a, w, *, tm, tn, out_dtype, relu2=False, name):
    m, k = a.shape
    n = w.shape[1]
    return pl.pallas_call(
        functools.partial(_mm_kernel, relu2=relu2),
        out_shape=jax.ShapeDtypeStruct((m, n), out_dtype),
        grid=(m // tm, n // tn),
        in_specs=[pl.BlockSpec((tm, k), lambda i, j: (i, 0)),
                  pl.BlockSpec((k, tn), lambda i, j: (0, j))],
        out_specs=pl.BlockSpec((tm, tn), lambda i, j: (i, j)),
        compiler_params=_params(("parallel", "parallel")),
        name=name,
    )(a, w)


def _mm_res_kernel(a_ref, w_ref, x_ref, g_ref, o_ref, acc_ref):
    k = pl.program_id(2)

    @pl.when(k == 0)
    def _():
        acc_ref[...] = jnp.zeros_like(acc_ref)

    acc_ref[...] += jnp.dot(a_ref[...], w_ref[...], preferred_element_type=F32)

    @pl.when(k == pl.num_programs(2) - 1)
    def _():
        o_ref[...] = x_ref[...] + g_ref[0] * acc_ref[...]


def _matmul_residual(a, w, x2, gate, seq, *, tm, tn, tk, name):
    m, kdim = a.shape
    n = w.shape[1]
    bsz = m // seq
    per_b = seq // tm
    return pl.pallas_call(
        _mm_res_kernel,
        out_shape=jax.ShapeDtypeStruct((m, n), F32),
        grid=(m // tm, n // tn, kdim // tk),
        in_specs=[pl.BlockSpec((tm, tk), lambda i, j, k: (i, k)),
                  pl.BlockSpec((tk, tn), lambda i, j, k: (k, j)),
                  pl.BlockSpec((tm, tn), lambda i, j, k: (i, j)),
                  pl.BlockSpec((1, 1, tn), lambda i, j, k: (i // per_b, 0, j))],
        out_specs=pl.BlockSpec((tm, tn), lambda i, j, k: (i, j)),
        scratch_shapes=[pltpu.VMEM((tm, tn), F32)],
        compiler_params=_params(("parallel", "parallel", "arbitrary")),
        name=name,
    )(a, w, x2, gate.reshape(bsz, 1, n))


def _rms_norm(x, gain):
    xf = x.astype(F32)
    y = xf * lax.rsqrt(jnp.mean(xf * xf, axis=-1, keepdims=True) + EPS)
    return (y * gain.astype(F32)).astype(x.dtype)


def _l2_normalize(x):
    return x * lax.rsqrt(jnp.sum(x * x, axis=-1, keepdims=True) + EPS)


def _masked_softmax(s, mask):
    s = jnp.where(mask, s.astype(F32), -jnp.inf)
    m = jnp.max(s, axis=-1, keepdims=True)
    m = jnp.where(jnp.isfinite(m), m, 0.0)
    e = jnp.exp(s - m)
    d = jnp.sum(e, axis=-1, keepdims=True)
    return e / jnp.where(d > 0, d, 1.0)


def _alibi_slopes(n):
    return 2.0 ** (-8.0 * jnp.arange(1, n + 1, dtype=F32) / n)


def _causal_depthwise_conv(x, w):
    k = w.shape[0]
    return lax.conv_general_dilated(
        x, w[:, None, :].astype(x.dtype), window_strides=(1,), padding=[(k - 1, 0)],
        dimension_numbers=("NWC", "WIO", "NWC"), feature_group_count=x.shape[-1])


def _gated_delta_rule_chunked(q, k, v, g, beta):
    B, H, T, dk = q.shape
    dv = v.shape[-1]
    C = GDN_CHUNK
    N = T // C
    q = q.reshape(B, H, N, C, dk)
    k = k.reshape(B, H, N, C, dk)
    v = v.reshape(B, H, N, C, dv)
    g = g.reshape(B, H, N, C)
    beta = beta.reshape(B, H, N, C)
    gam = jnp.cumsum(g, axis=-1)
    causal = jnp.tril(jnp.ones((C, C), bool))
    strict = jnp.tril(jnp.ones((C, C), bool), -1)
    diff = gam[..., :, None] - gam[..., None, :]
    decay = jnp.where(causal, jnp.exp(jnp.where(causal, diff, 0.0)), 0.0)
    kk = jnp.einsum('bhncd,bhnsd->bhncs', k, k)
    lower = jnp.where(strict, beta[..., :, None] * kk * decay, 0.0)
    a = lower + jnp.eye(C, dtype=F32)
    rhs = jnp.concatenate([v * beta[..., None], k * (beta * jnp.exp(gam))[..., None]], axis=-1)
    sol = lax.linalg.triangular_solve(a, rhs, left_side=True, lower=True, unit_diagonal=True)
    u, w = sol[..., :dv], sol[..., dv:]
    qk = jnp.einsum('bhncd,bhnsd->bhncs', q, k) * decay
    q_dec = q * jnp.exp(gam)[..., None]
    g_last = gam[..., -1]
    k_dec = k * jnp.exp(g_last[..., None] - gam)[..., None]

    def step(S, inp):
        u_n, w_n, qk_n, qd_n, kd_n, gl_n = inp
        v_new = u_n - jnp.einsum('bhcd,bhde->bhce', w_n, S)
        o = jnp.einsum('bhcd,bhde->bhce', qd_n, S) + jnp.einsum('bhcs,bhse->bhce', qk_n, v_new)
        S = S * jnp.exp(gl_n)[..., None, None] + jnp.einsum('bhcd,bhce->bhde', kd_n, v_new)
        return S, o

    xs = tuple(jnp.moveaxis(t, 2, 0) for t in (u, w, qk, q_dec, k_dec, g_last))
    _, o = lax.scan(step, jnp.zeros((B, H, dk, dv), F32), xs)
    return jnp.moveaxis(o, 0, 2).reshape(B, H, T, dv)


def _gdn_mixer(q, k, v, z, b_raw, a_raw, conv_w, a_log, dt_bias, norm_w):
    B, T, _ = q.shape
    qkv = jax.nn.silu(_causal_depthwise_conv(jnp.concatenate([q, k, v], axis=-1), conv_w))
    q, k, v = jnp.split(qkv, 3, axis=-1)
    heads = lambda t: t.reshape(B, T, GDN_HEADS, HEAD_DIM).transpose(0, 2, 1, 3)
    q = _l2_normalize(heads(q)) * (HEAD_DIM ** -0.5)
    k = _l2_normalize(heads(k))
    v = heads(v)
    beta = jax.nn.sigmoid(b_raw).transpose(0, 2, 1)
    g = (-jnp.exp(a_log) * jax.nn.softplus(a_raw + dt_bias)).transpose(0, 2, 1)
    o = _gated_delta_rule_chunked(q, k, v, g, beta).transpose(0, 2, 1, 3)
    o = _rms_norm(o, norm_w) * jax.nn.silu(z.reshape(B, T, GDN_HEADS, HEAD_DIM))
    return o.reshape(B, T, GDN_DK)


def _compress_blocks(x, pos, w1, w2):
    B, G, T, D = x.shape
    r = CMP_BLOCK // CMP_STRIDE
    n = T // CMP_STRIDE
    pieces = x.reshape(B, G, n, CMP_STRIDE, D)
    blocks = jnp.concatenate([pieces[:, :, i:n - r + 1 + i] for i in range(r)], axis=3)
    blocks = (blocks + pos).reshape(B, G, n - r + 1, CMP_BLOCK * D)
    return jax.nn.silu(blocks @ w1) @ w2


def _cmp_to_sel_overlap(n_cmp, n_sel):
    start = jnp.arange(n_cmp)[:, None] * CMP_STRIDE
    lo = jnp.arange(n_sel)[None, :] * SEL_BLOCK
    return ((start <= lo + SEL_BLOCK - 1) & (start + CMP_BLOCK - 1 >= lo)).astype(F32)


def _selected_attention(qh, ks, vs, sel_idx, slopes):
    B, G, R, T, D = qh.shape
    n = sel_idx.shape[-1]
    nq = T // SEL_QBLOCK
    kb = ks.reshape(B, G, T // SEL_BLOCK, SEL_BLOCK, D)
    vb = vs.reshape(B, G, T // SEL_BLOCK, SEL_BLOCK, D)
    gather = jax.vmap(jax.vmap(lambda blocks, ix: blocks[ix]))
    q_chunks = jnp.moveaxis(qh.reshape(B, G, R, nq, SEL_QBLOCK, D), 3, 0)
    i_chunks = jnp.moveaxis(sel_idx.reshape(B, G, nq, SEL_QBLOCK, n), 2, 0)
    offs = jnp.arange(SEL_BLOCK)
    n_keys = n * SEL_BLOCK

    def one_chunk(args):
        qc, ic, c0 = args
        kg = gather(kb, ic).reshape(B, G, SEL_QBLOCK, n_keys, D)
        vg = gather(vb, ic).reshape(B, G, SEL_QBLOCK, n_keys, D)
        t = c0 * SEL_QBLOCK + jnp.arange(SEL_QBLOCK)
        kpos = (ic[..., None] * SEL_BLOCK + offs).reshape(B, G, SEL_QBLOCK, n_keys)
        dist = (t[:, None] - kpos).astype(F32)
        s = (jnp.einsum('bgrqd,bgqkd->bgrqk', qc, kg).astype(F32)
             - slopes[:, :, None, None] * dist[:, :, None])
        p = _masked_softmax(s, (dist >= 0)[:, :, None])
        return jnp.einsum('bgrqk,bgqkd->bgrqd', p.astype(vg.dtype), vg)

    o = lax.map(one_chunk, (q_chunks, i_chunks, jnp.arange(nq)))
    return jnp.moveaxis(o, 0, 3).reshape(B, G, R, T, D)


def _window_attention(qh, kw, vw, slopes):
    B, G, R, T, D = qh.shape
    qb_len = WIN_QBLOCK
    nb = T // qb_len
    npv = WIN_PREV_BLOCKS
    pad = npv * qb_len
    kp = jnp.pad(kw, ((0, 0), (0, 0), (pad, 0), (0, 0))).reshape(B, G, nb + npv, qb_len, D)
    vp = jnp.pad(vw, ((0, 0), (0, 0), (pad, 0), (0, 0))).reshape(B, G, nb + npv, qb_len, D)
    kband = jnp.concatenate([kp[:, :, i:i + nb] for i in range(npv + 1)], axis=3)
    vband = jnp.concatenate([vp[:, :, i:i + nb] for i in range(npv + 1)], axis=3)
    qb = qh.reshape(B, G, R, nb, qb_len, D)
    t = jnp.arange(T).reshape(nb, qb_len)
    kpos = jnp.arange(nb)[:, None] * qb_len - pad + jnp.arange((npv + 1) * qb_len)[None, :]
    dist = t[:, :, None] - kpos[:, None, :]
    mask = (dist >= 0) & (dist < WINDOW) & (kpos[:, None, :] >= 0)
    s = (jnp.einsum('bgrnqd,bgnkd->bgrnqk', qb, kband).astype(F32)
         - slopes[:, :, None, None, None] * dist.astype(F32))
    p = _masked_softmax(s, mask)
    o = jnp.einsum('bgrnqk,bgnkd->bgrnqd', p.astype(vband.dtype), vband)
    return o.reshape(B, G, R, T, D)


def _nsa_mixer(q, k_c, v_c, k_s, v_s, k_w, v_w, gate_raw,
               pos_k, w1_k, w2_k, pos_v, w1_v, w2_v):
    B, T, _ = q.shape
    G = NSA_KV_HEADS
    R = NSA_HEADS // G
    D = HEAD_DIM
    slopes = _alibi_slopes(NSA_HEADS).reshape(G, R)
    qh = q.reshape(B, T, G, R, D).transpose(0, 2, 3, 1, 4) * (D ** -0.5)
    kv = lambda t: t.reshape(B, T, G, D).transpose(0, 2, 1, 3)
    t_pos = jnp.arange(T)
    kc = _compress_blocks(kv(k_c), pos_k, w1_k, w2_k)
    vc = _compress_blocks(kv(v_c), pos_v, w1_v, w2_v)
    n_cmp = kc.shape[2]
    cmp_end = jnp.arange(n_cmp) * CMP_STRIDE + CMP_BLOCK - 1
    dist_c = t_pos[:, None] - cmp_end[None, :]
    s_c = (jnp.einsum('bgrtd,bgnd->bgrtn', qh, kc).astype(F32)
           - slopes[:, :, None, None] * dist_c.astype(F32))
    p_c = _masked_softmax(s_c, dist_c >= 0)
    o_c = jnp.einsum('bgrtn,bgnd->bgrtd', p_c, vc)
    n_sel = T // SEL_BLOCK
    imp = jnp.einsum('bgrtn,ns->bgts', p_c, _cmp_to_sel_overlap(n_cmp, n_sel))
    blk = jnp.arange(n_sel)[None, :]
    cur = (t_pos // SEL_BLOCK)[:, None]
    forced = (blk == 0) | ((cur - blk) < SEL_LOCAL)
    imp = jnp.where(forced, jnp.inf, imp)
    imp = jnp.where(blk <= cur, imp, -jnp.inf)
    _, sel_idx = lax.top_k(imp, min(SEL_TOPN, n_sel))
    o_s = _selected_attention(qh, kv(k_s), kv(v_s), sel_idx, slopes)
    o_w = _window_attention(qh, kv(k_w), kv(v_w), slopes)
    gates = jax.nn.sigmoid(gate_raw).reshape(B, T, G, R, 3).transpose(0, 2, 3, 1, 4)
    o = gates[..., 0:1] * o_c + gates[..., 1:2] * o_s + gates[..., 2:3] * o_w
    return o.transpose(0, 3, 1, 2, 4).reshape(B, T, NSA_DQ)


def _split_columns(proj):
    out, start = [], 0
    for size in IN_SIZES:
        out.append(proj[..., start:start + size])
        start += size
    return out


def kernel(x, c, ada_w, ada_b, norm1_w, w_in, gdn_conv_w, gdn_a_log, gdn_dt_bias, gdn_norm_w,
           cmp_pos_k, cmp_w1_k, cmp_w2_k, cmp_pos_v, cmp_w1_v, cmp_w2_v, w_out, norm2_w,
           w_up, w_down, final_norm_w):
    bsz, seq, d = x.shape
    depth = ada_w.shape[0]
    m = bsz * seq
    x2 = x.reshape(m, d)
    for i in range(depth):
        mod = _modulation(c, ada_w[i], ada_b[i])
        sh1, sc1, g1, sh2, sc2, g2 = jnp.split(mod, 6, axis=-1)
        h = _norm_mod(x2, norm1_w[i], sc1, sh1, seq, BF16)
        cols = _split_columns(w_in[i])
        small = jnp.concatenate([cols[4], cols[5], cols[13]], axis=-1)
        small = jnp.pad(small, ((0, 0), (0, LANES - small.shape[1])))
        w_big = jnp.concatenate(cols[0:4] + cols[6:13] + [small], axis=-1).astype(BF16)
        proj = _matmul(h, w_big, tm=1024, tn=896, out_dtype=F32, name="in_proj")
        proj = proj.reshape(bsz, seq, -1)
        o0 = 0
        parts = []
        for size in (GDN_DK,) * 4 + (NSA_DQ,) + (NSA_DKV,) * 6 + (GDN_HEADS, GDN_HEADS, 3 * NSA_HEADS):
            parts.append(proj[..., o0:o0 + size])
            o0 += size
        gq, gk, gv, gz, nq, nkc, nvc, nks, nvs, nkw, nvw, gb, ga, ngate = parts
        o_a = _gdn_mixer(gq, gk, gv, gz, gb, ga, gdn_conv_w[i], gdn_a_log[i], gdn_dt_bias[i], gdn_norm_w[i])
        o_b = _nsa_mixer(nq, nkc, nvc, nks, nvs, nkw, nvw, ngate,
                         cmp_pos_k[i], cmp_w1_k[i], cmp_w2_k[i], cmp_pos_v[i], cmp_w1_v[i], cmp_w2_v[i])
        mix = jnp.concatenate([o_a, o_b], axis=-1).reshape(m, -1).astype(BF16)
        x2 = _matmul_residual(mix, w_out[i].astype(BF16), x2, g1, seq,
                              tm=1024, tn=1024, tk=2048, name="out_proj")
        h2 = _norm_mod(x2, norm2_w[i], sc2, sh2, seq, BF16)
        hid = _matmul(h2, w_up[i].astype(BF16), tm=1024, tn=1024, out_dtype=BF16, relu2=True, name="mlp_up")
        x2 = _matmul_residual(hid, w_down[i].astype(BF16), x2, g2, seq,
                              tm=1024, tn=1024, tk=2048, name="mlp_down")
    return _final_norm(x2, final_norm_w).reshape(bsz, seq, d)
```

```python
import functools

import jax
import jax.numpy as jnp
from jax import lax
from jax.experimental import pallas as pl
from jax.experimental.pallas import tpu as pltpu

F32 = jnp.float32
BF16 = jnp.bfloat16
I32 = jnp.int32

HEAD_DIM = 128
GDN_HEADS = 16
NSA_HEADS = 16
NSA_KV_HEADS = 4
NSA_REP = NSA_HEADS // NSA_KV_HEADS
GDN_DK = GDN_HEADS * HEAD_DIM
NSA_DQ = NSA_HEADS * HEAD_DIM
NSA_DKV = NSA_KV_HEADS * HEAD_DIM
GDN_CONV = 4
GDN_CHUNK = 64
CMP_BLOCK = 32
CMP_STRIDE = 16
SEL_BLOCK = 64
SEL_TOPN = 16
SEL_LOCAL = 2
WINDOW = 512
EPS = 1e-6
IN_SIZES = (GDN_DK, GDN_DK, GDN_DK, GDN_DK, GDN_HEADS, GDN_HEADS,
            NSA_DQ, NSA_DKV, NSA_DKV, NSA_DKV, NSA_DKV, NSA_DKV, NSA_DKV, 3 * NSA_HEADS)

VMEM_LIMIT_V7X = 56 * 1024 * 1024
LANES = 128
NSA_TQ = 128
NSA_TK = 512
NEG_INF = float("-inf")
NT_DIMS = (((1,), (1,)), ((), ()))
TN_DIMS = (((0,), (0,)), ((), ()))


def _params(sem):
    return pltpu.CompilerParams(dimension_semantics=sem, vmem_limit_bytes=VMEM_LIMIT_V7X)


def _mod_kernel(c_ref, w_ref, b_ref, o_ref):
    c = c_ref[...]
    s = (c * jax.nn.sigmoid(c)).astype(BF16)
    o_ref[...] = jnp.dot(s, w_ref[...].astype(BF16), preferred_element_type=F32) + b_ref[...]


def _modulation(c, w, b):
    bsz, d = c.shape
    n = w.shape[1]
    rows = 8
    cp = jnp.zeros((rows, d), F32).at[:bsz].set(c)
    tn = 512
    out = pl.pallas_call(
        _mod_kernel,
        out_shape=jax.ShapeDtypeStruct((rows, n), F32),
        grid=(n // tn,),
        in_specs=[pl.BlockSpec((rows, d), lambda j: (0, 0)),
                  pl.BlockSpec((d, tn), lambda j: (0, j)),
                  pl.BlockSpec((1, tn), lambda j: (0, j))],
        out_specs=pl.BlockSpec((rows, tn), lambda j: (0, j)),
        compiler_params=_params(("parallel",)),
        name="adaln_mod",
    )(cp, w, b.reshape(1, n))
    return out[:bsz]


def _norm_kernel(x_ref, g_ref, sc_ref, sh_ref, o_ref):
    x = x_ref[...]
    y = x * lax.rsqrt(jnp.mean(x * x, axis=-1, keepdims=True) + EPS) * g_ref[...]
    o_ref[...] = (y * (1.0 + sc_ref[0]) + sh_ref[0]).astype(o_ref.dtype)


def _final_norm_kernel(x_ref, g_ref, o_ref):
    x = x_ref[...]
    o_ref[...] = x * lax.rsqrt(jnp.mean(x * x, axis=-1, keepdims=True) + EPS) * g_ref[...]


def _norm_mod(x2, gain, scale, shift, seq, out_dtype):
    m, d = x2.shape
    tm = 256
    per_b = seq // tm
    bsz = m // seq
    return pl.pallas_call(
        _norm_kernel,
        out_shape=jax.ShapeDtypeStruct((m, d), out_dtype),
        grid=(m // tm,),
        in_specs=[pl.BlockSpec((tm, d), lambda i: (i, 0)),
                  pl.BlockSpec((1, d), lambda i: (0, 0)),
                  pl.BlockSpec((1, 1, d), lambda i: (i // per_b, 0, 0)),
                  pl.BlockSpec((1, 1, d), lambda i: (i // per_b, 0, 0))],
        out_specs=pl.BlockSpec((tm, d), lambda i: (i, 0)),
        compiler_params=_params(("parallel",)),
        name="rmsnorm_mod",
    )(x2, gain.reshape(1, d), scale.reshape(bsz, 1, d), shift.reshape(bsz, 1, d))


def _final_norm(x2, gain):
    m, d = x2.shape
    tm = 256
    return pl.pallas_call(
        _final_norm_kernel,
        out_shape=jax.ShapeDtypeStruct((m, d), F32),
        grid=(m // tm,),
        in_specs=[pl.BlockSpec((tm, d), lambda i: (i, 0)),
                  pl.BlockSpec((1, d), lambda i: (0, 0))],
        out_specs=pl.BlockSpec((tm, d), lambda i: (i, 0)),
        compiler_params=_params(("parallel",)),
        name="final_rmsnorm",
    )(x2, gain.reshape(1, d))


def _mm_kernel(a_ref, w_ref, o_ref, *, relu2):
    acc = jnp.dot(a_ref[...], w_ref[...], preferred_element_type=F32)
    if relu2:
        acc = jnp.square(jnp.maximum(acc, 0.0))
    o_ref[...] = acc.astype(o_ref.dtype)


def _matmul(a, w, *, tm, tn, out_dtype, relu2=False, name):
    m, k = a.shape
    n = w.shape[1]
    return pl.pallas_call(
        functools.partial(_mm_kernel, relu2=relu2),
        out_shape=jax.ShapeDtypeStruct((m, n), out_dtype),
        grid=(m // tm, n // tn),
        in_specs=[pl.BlockSpec((tm, k), lambda i, j: (i, 0)),
                  pl.BlockSpec((k, tn), lambda i, j: (0, j))],
        out_specs=pl.BlockSpec((tm, tn), lambda i, j: (i, j)),
        compiler_params=_params(("parallel", "parallel")),
        name=name,
    )(a, w)


def _mm_res_kernel(a_ref, w_ref, x_ref, g_ref, o_ref, acc_ref):
    k = pl.program_id(2)

    @pl.when(k == 0)
    def _():
        acc_ref[...] = jnp.zeros_like(acc_ref)

    acc_ref[...] += jnp.dot(a_ref[...], w_ref[...], preferred_element_type=F32)

    @pl.when(k == pl.num_programs(2) - 1)
    def _():
        o_ref[...] = x_ref[...] + g_ref[0] * acc_ref[...]


def _matmul_residual(a, w, x2, gate, seq, *, tm, tn, tk, name):
    m, kdim = a.shape
    n = w.shape[1]
    bsz = m // seq
    per_b = seq // tm
    return pl.pallas_call(
        _mm_res_kernel,
        out_shape=jax.ShapeDtypeStruct((m, n), F32),
        grid=(m // tm, n // tn, kdim // tk),
        in_specs=[pl.BlockSpec((tm, tk), lambda i, j, k: (i, k)),
                  pl.BlockSpec((tk, tn), lambda i, j, k: (k, j)),
                  pl.BlockSpec((tm, tn), lambda i, j, k: (i, j)),
                  pl.BlockSpec((1, 1, tn), lambda i, j, k: (i // per_b, 0, j))],
        out_specs=pl.BlockSpec((tm, tn), lambda i, j, k: (i, j)),
        scratch_shapes=[pltpu.VMEM((tm, tn), F32)],
        compiler_params=_params(("parallel", "parallel", "arbitrary")),
        name=name,
    )(a, w, x2, gate.reshape(bsz, 1, n))


def _rms_norm(x, gain):
    y = x * lax.rsqrt(jnp.mean(x * x, axis=-1, keepdims=True) + EPS)
    return y * gain


def _l2_normalize(x):
    return x * lax.rsqrt(jnp.sum(x * x, axis=-1, keepdims=True) + EPS)


def _causal_depthwise_conv(x, w):
    k = w.shape[0]
    return lax.conv_general_dilated(
        x, w[:, None, :].astype(x.dtype), window_strides=(1,), padding=[(k - 1, 0)],
        dimension_numbers=("NWC", "WIO", "NWC"), feature_group_count=x.shape[-1])


def _gated_delta_rule_chunked(q, k, v, g, beta):
    B, H, T, dk = q.shape
    dv = v.shape[-1]
    C = GDN_CHUNK
    N = T // C
    q = q.reshape(B, H, N, C, dk)
    k = k.reshape(B, H, N, C, dk)
    v = v.reshape(B, H, N, C, dv)
    g = g.reshape(B, H, N, C)
    beta = beta.reshape(B, H, N, C)
    gam = jnp.cumsum(g, axis=-1)
    causal = jnp.tril(jnp.ones((C, C), bool))
    strict = jnp.tril(jnp.ones((C, C), bool), -1)
    diff = gam[..., :, None] - gam[..., None, :]
    decay = jnp.where(causal, jnp.exp(jnp.where(causal, diff, 0.0)), 0.0)
    kk = jnp.einsum('bhncd,bhnsd->bhncs', k, k)
    lower = jnp.where(strict, beta[..., :, None] * kk * decay, 0.0)
    a = lower + jnp.eye(C, dtype=F32)
    rhs = jnp.concatenate([v * beta[..., None], k * (beta * jnp.exp(gam))[..., None]], axis=-1)
    sol = lax.linalg.triangular_solve(a, rhs, left_side=True, lower=True, unit_diagonal=True)
    u, w = sol[..., :dv], sol[..., dv:]
    qk = jnp.einsum('bhncd,bhnsd->bhncs', q, k) * decay
    q_dec = q * jnp.exp(gam)[..., None]
    g_last = gam[..., -1]
    k_dec = k * jnp.exp(g_last[..., None] - gam)[..., None]

    def step(S, inp):
        u_n, w_n, qk_n, qd_n, kd_n, gl_n = inp
        v_new = u_n - jnp.einsum('bhcd,bhde->bhce', w_n, S)
        o = jnp.einsum('bhcd,bhde->bhce', qd_n, S) + jnp.einsum('bhcs,bhse->bhce', qk_n, v_new)
        S = S * jnp.exp(gl_n)[..., None, None] + jnp.einsum('bhcd,bhce->bhde', kd_n, v_new)
        return S, o

    xs = tuple(jnp.moveaxis(t, 2, 0) for t in (u, w, qk, q_dec, k_dec, g_last))
    _, o = lax.scan(step, jnp.zeros((B, H, dk, dv), F32), xs)
    return jnp.moveaxis(o, 0, 2).reshape(B, H, T, dv)


def _gdn_mixer(q, k, v, z, b_raw, a_raw, conv_w, a_log, dt_bias, norm_w):
    B, T, _ = q.shape
    qkv = jax.nn.silu(_causal_depthwise_conv(jnp.concatenate([q, k, v], axis=-1), conv_w))
    q, k, v = jnp.split(qkv, 3, axis=-1)
    heads = lambda t: t.reshape(B, T, GDN_HEADS, HEAD_DIM).transpose(0, 2, 1, 3)
    q = _l2_normalize(heads(q)) * (HEAD_DIM ** -0.5)
    k = _l2_normalize(heads(k))
    v = heads(v)
    beta = jax.nn.sigmoid(b_raw).transpose(0, 2, 1)
    g = (-jnp.exp(a_log) * jax.nn.softplus(a_raw + dt_bias)).transpose(0, 2, 1)
    o = _gated_delta_rule_chunked(q, k, v, g, beta).transpose(0, 2, 1, 3)
    o = _rms_norm(o, norm_w) * jax.nn.silu(z.reshape(B, T, GDN_HEADS, HEAD_DIM))
    return o.reshape(B, T, GDN_DK)


def _compress_kernel(x_ref, pos_ref, w1_ref, w2_ref, o_ref):
    n = o_ref.shape[0]
    d = x_ref.shape[1]
    acc_a = jnp.zeros((n, d), F32)
    acc_b = jnp.zeros((n, d), F32)
    for j in range(CMP_STRIDE):
        xj = x_ref[pl.ds(j, n, stride=CMP_STRIDE), :]
        lo = (xj + pos_ref[j:j + 1, :]).astype(BF16)
        hi = (xj + pos_ref[CMP_STRIDE + j:CMP_STRIDE + j + 1, :]).astype(BF16)
        acc_a += jnp.dot(lo, w1_ref[j * d:(j + 1) * d, :].astype(BF16), preferred_element_type=F32)
        acc_b += jnp.dot(hi, w1_ref[(CMP_STRIDE + j) * d:(CMP_STRIDE + j + 1) * d, :].astype(BF16),
                         preferred_element_type=F32)
    pre = acc_a + pltpu.roll(acc_b, n - 1, 0)
    hid = (pre * jax.nn.sigmoid(pre)).astype(BF16)
    out = jnp.dot(hid, w2_ref[...].astype(BF16), preferred_element_type=F32)
    row = lax.broadcasted_iota(I32, (n, d), 0)
    o_ref[...] = jnp.where(row < n - 1, out, 0.0).astype(o_ref.dtype)


def _compress(kvc, pos, w1, w2, seq):
    bsz = kvc.shape[0]
    g = NSA_KV_HEADS
    n = seq // CMP_STRIDE
    d = HEAD_DIM
    return pl.pallas_call(
        _compress_kernel,
        out_shape=jax.ShapeDtypeStruct((2, bsz, g, n, d), BF16),
        grid=(2, bsz, g),
        in_specs=[pl.BlockSpec((None, seq, d), lambda s, b, h: (b, 0, s * NSA_KV_HEADS + h)),
                  pl.BlockSpec((None, CMP_BLOCK, d), lambda s, b, h: (s, 0, 0)),
                  pl.BlockSpec((None, CMP_BLOCK * d, d), lambda s, b, h: (s, 0, 0)),
                  pl.BlockSpec((None, d, d), lambda s, b, h: (s, 0, 0))],
        out_specs=pl.BlockSpec((None, None, None, n, d), lambda s, b, h: (s, b, h, 0, 0)),
        compiler_params=_params(("parallel", "parallel", "parallel")),
        name="nsa_compress",
    )(kvc, pos, w1, w2)


def _softmax_rows(s, mask):
    s = jnp.where(mask, s, NEG_INF)
    m = jnp.max(s, axis=-1, keepdims=True)
    m = jnp.where(jnp.abs(m) < jnp.inf, m, 0.0)
    e = jnp.exp(s - m)
    d = jnp.sum(e, axis=-1, keepdims=True)
    return e * (1.0 / jnp.where(d > 0, d, 1.0))


def _nsa_kernel(slopes_ref, q_ref, kc_ref, vc_ref, ks_ref, vs_ref, kw_ref, vw_ref, gate_ref, o_ref):
    seq = ks_ref.shape[0]
    tq, tk, rep, d = NSA_TQ, NSA_TK, NSA_REP, HEAD_DIM
    rows = rep * tq
    g = pl.program_id(1)
    t0 = pl.program_id(2) * tq

    q = q_ref[...] * (d ** -0.5)
    qs = jnp.concatenate([q[:, r * d:(r + 1) * d] for r in range(rep)], axis=0).astype(BF16)
    slope = jnp.concatenate([jnp.full((tq, 1), slopes_ref[g * rep + r], F32) for r in range(rep)], axis=0)
    row = lax.broadcasted_iota(I32, (rows, 1), 0)
    t_col = t0 + (row & (tq - 1))

    ncp = kc_ref.shape[0]
    n_idx = lax.broadcasted_iota(I32, (rows, ncp), 1)
    dist_c = t_col - (n_idx * CMP_STRIDE + (CMP_BLOCK - 1))
    s_c = lax.dot_general(qs, kc_ref[...], NT_DIMS, preferred_element_type=F32)
    s_c = s_c - slope * dist_c.astype(F32)
    p_c = _softmax_rows(s_c, dist_c >= 0).astype(BF16)
    o_c = jnp.dot(p_c, vc_ref[...], preferred_element_type=F32)

    n_sel = seq // SEL_BLOCK
    ov_s = lax.broadcasted_iota(I32, (n_sel, ncp), 0) * SEL_BLOCK
    ov_n = lax.broadcasted_iota(I32, (n_sel, ncp), 1) * CMP_STRIDE
    ov_t = ((ov_n <= ov_s + (SEL_BLOCK - 1)) & (ov_n + (CMP_BLOCK - 1) >= ov_s)).astype(BF16)
    imp = jnp.zeros((n_sel, tq), F32)
    for r in range(rep):
        imp += lax.dot_general(ov_t, p_c[r * tq:(r + 1) * tq, :], NT_DIMS, preferred_element_type=F32)
    blk = lax.broadcasted_iota(I32, (n_sel, tq), 0)
    cur = (t0 + lax.broadcasted_iota(I32, (n_sel, tq), 1)) // SEL_BLOCK
    forced = (blk == 0) | ((cur - blk) < SEL_LOCAL)
    imp = jnp.where(forced, jnp.inf, imp)
    imp = jnp.where(blk <= cur, imp, NEG_INF)
    rank = jnp.zeros((n_sel, tq), I32)
    for i in range(n_sel):
        ri = imp[i:i + 1, :]
        rank += ((ri > imp) | ((ri == imp) & (blk > i))).astype(I32)
    sel_t = (rank < min(SEL_TOPN, n_sel)).astype(BF16)

    def sel_body(j, carry):
        m, l, acc = carry
        k0 = pl.multiple_of(j * tk, tk)
        kt = ks_ref[pl.ds(k0, tk), :]
        vt = vs_ref[pl.ds(k0, tk), :]
        s = lax.dot_general(qs, kt, NT_DIMS, preferred_element_type=F32)
        dist = t_col - (k0 + lax.broadcasted_iota(I32, (rows, tk), 1))
        s = s - slope * dist.astype(F32)
        e_blk = lax.broadcasted_iota(I32, (n_sel, tk), 0)
        e_key = (k0 + lax.broadcasted_iota(I32, (n_sel, tk), 1)) // SEL_BLOCK
        expand = (e_blk == e_key).astype(BF16)
        picked = lax.dot_general(sel_t, expand, TN_DIMS, preferred_element_type=F32)
        picked = jnp.concatenate([picked] * rep, axis=0)
        s = jnp.where((dist >= 0) & (picked > 0.5), s, NEG_INF)
        m_new = jnp.maximum(m, jnp.max(s, axis=-1, keepdims=True))
        m_safe = jnp.where(m_new > NEG_INF, m_new, 0.0)
        alpha = jnp.exp(m - m_safe)
        p = jnp.exp(s - m_safe)
        l = alpha * l + jnp.sum(p, axis=-1, keepdims=True)
        acc = alpha * acc + jnp.dot(p.astype(BF16), vt, preferred_element_type=F32)
        return m_new, l, acc

    n_tiles = (t0 + tq + tk - 1) // tk
    init = (jnp.full((rows, 1), NEG_INF, F32), jnp.zeros((rows, 1), F32), jnp.zeros((rows, d), F32))
    _, l_s, acc_s = lax.fori_loop(0, n_tiles, sel_body, init)
    o_s = acc_s * (1.0 / jnp.where(l_s > 0, l_s, 1.0))

    wk = tq + WINDOW
    start = pl.multiple_of(jnp.maximum(t0 - WINDOW, 0), tq)
    kt = kw_ref[pl.ds(start, wk), :]
    vt = vw_ref[pl.ds(start, wk), :]
    s_w = lax.dot_general(qs, kt, NT_DIMS, preferred_element_type=F32)
    dist_w = t_col - (start + lax.broadcasted_iota(I32, (rows, wk), 1))
    s_w = s_w - slope * dist_w.astype(F32)
    p_w = _softmax_rows(s_w, (dist_w >= 0) & (dist_w < WINDOW)).astype(BF16)
    o_w = jnp.dot(p_w, vt, preferred_element_type=F32)

    gt = jax.nn.sigmoid(gate_ref[...])
    for r in range(rep):
        sl = slice(r * tq, (r + 1) * tq)
        o = (gt[:, 3 * r:3 * r + 1] * o_c[sl] + gt[:, 3 * r + 1:3 * r + 2] * o_s[sl]
             + gt[:, 3 * r + 2:3 * r + 3] * o_w[sl])
        o_ref[:, r * d:(r + 1) * d] = o.astype(o_ref.dtype)


def _nsa_attention(qa, q_col0, cmp, kvb, gates, slopes, seq):
    bsz = qa.shape[0]
    g, rep, d, tq = NSA_KV_HEADS, NSA_REP, HEAD_DIM, NSA_TQ
    ncp = seq // CMP_STRIDE
    qblk0 = q_col0 // (rep * d)
    kv_spec = lambda s: pl.BlockSpec((None, seq, d), lambda b, h, i: (b, 0, s * NSA_KV_HEADS + h))
    cmp_spec = lambda s: pl.BlockSpec((None, None, None, ncp, d), lambda b, h, i: (s, b, h, 0, 0))
    return pl.pallas_call(
        _nsa_kernel,
        out_shape=jax.ShapeDtypeStruct((bsz, seq, NSA_DQ), BF16),
        grid=(bsz, g, seq // tq),
        in_specs=[pl.BlockSpec(memory_space=pltpu.SMEM),
                  pl.BlockSpec((None, tq, rep * d), lambda b, h, i: (b, i, qblk0 + h)),
                  cmp_spec(0), cmp_spec(1), kv_spec(0), kv_spec(1), kv_spec(2), kv_spec(3),
                  pl.BlockSpec((None, None, tq, 3 * rep), lambda b, h, i: (b, h, i, 0))],
        out_specs=pl.BlockSpec((None, tq, rep * d), lambda b, h, i: (b, i, h)),
        compiler_params=_params(("parallel", "parallel", "arbitrary")),
        name="nsa_attention",
    )(slopes, qa, cmp, cmp, kvb, kvb, kvb, kvb, gates)


def _split_columns(proj):
    out, start = [], 0
    for size in IN_SIZES:
        out.append(proj[..., start:start + size])
        start += size
    return out


def kernel(x, c, ada_w, ada_b, norm1_w, w_in, gdn_conv_w, gdn_a_log, gdn_dt_bias, gdn_norm_w,
           cmp_pos_k, cmp_w1_k, cmp_w2_k, cmp_pos_v, cmp_w1_v, cmp_w2_v, w_out, norm2_w,
           w_up, w_down, final_norm_w):
    bsz, seq, d = x.shape
    depth = ada_w.shape[0]
    m = bsz * seq
    x2 = x.reshape(m, d)
    slopes = 2.0 ** (-8.0 * jnp.arange(1, NSA_HEADS + 1, dtype=F32) / NSA_HEADS)
    for i in range(depth):
        mod = _modulation(c, ada_w[i], ada_b[i])
        sh1, sc1, g1, sh2, sc2, g2 = jnp.split(mod, 6, axis=-1)
        h = _norm_mod(x2, norm1_w[i], sc1, sh1, seq, BF16)
        cols = _split_columns(w_in[i])
        small = jnp.concatenate([cols[4], cols[5], cols[13]], axis=-1)
        small = jnp.pad(small, ((0, 0), (0, LANES - small.shape[1])))
        w_a = jnp.concatenate(cols[0:4] + [cols[6]], axis=-1).astype(BF16)
        w_b = jnp.concatenate(cols[7:9] + [small], axis=-1).astype(BF16)
        w_c = jnp.concatenate(cols[9:13], axis=-1).astype(BF16)
        pa = _matmul(h, w_a, tm=1024, tn=1024, out_dtype=F32, name="in_proj_a").reshape(bsz, seq, -1)
        pb = _matmul(h, w_b, tm=1024, tn=w_b.shape[1], out_dtype=F32, name="in_proj_b").reshape(bsz, seq, -1)
        pc = _matmul(h, w_c, tm=1024, tn=1024, out_dtype=BF16, name="in_proj_c").reshape(bsz, seq, -1)

        gq, gk, gv, gz = (pa[..., j * GDN_DK:(j + 1) * GDN_DK] for j in range(4))
        sm = pb[..., 2 * NSA_DKV:]
        gb, ga = sm[..., :GDN_HEADS], sm[..., GDN_HEADS:2 * GDN_HEADS]
        ngate = sm[..., 2 * GDN_HEADS:2 * GDN_HEADS + 3 * NSA_HEADS]
        o_a = _gdn_mixer(gq, gk, gv, gz, gb, ga, gdn_conv_w[i], gdn_a_log[i], gdn_dt_bias[i], gdn_norm_w[i])

        cmp = _compress(pb, jnp.stack([cmp_pos_k[i], cmp_pos_v[i]]), jnp.stack([cmp_w1_k[i], cmp_w1_v[i]]),
                        jnp.stack([cmp_w2_k[i], cmp_w2_v[i]]), seq)
        gates = ngate.reshape(bsz, seq, NSA_KV_HEADS, 3 * NSA_REP).transpose(0, 2, 1, 3)
        o_b = _nsa_attention(pa, 4 * GDN_DK, cmp, pc, gates, slopes, seq)

        mix = jnp.concatenate([o_a.astype(BF16), o_b], axis=-1).reshape(m, -1)
        x2 = _matmul_residual(mix, w_out[i].astype(BF16), x2, g1, seq,
                              tm=1024, tn=1024, tk=2048, name="out_proj")
        h2 = _norm_mod(x2, norm2_w[i], sc2, sh2, seq, BF16)
        hid = _matmul(h2, w_up[i].astype(BF16), tm=1024, tn=1024, out_dtype=BF16, relu2=True, name="mlp_up")
        x2 = _matmul_residual(hid, w_down[i].astype(BF16), x2, g2, seq,
                              tm=1024, tn=1024, tk=2048, name="mlp_down")
    return _final_norm(x2, final_norm_w).reshape(bsz, seq, d)
```

```python
import functools

import jax
import jax.numpy as jnp
from jax import lax
from jax.experimental import pallas as pl
from jax.experimental.pallas import tpu as pltpu

F32 = jnp.float32
BF16 = jnp.bfloat16
I32 = jnp.int32

HEAD_DIM = 128
GDN_HEADS = 16
NSA_HEADS = 16
NSA_KV_HEADS = 4
NSA_REP = NSA_HEADS // NSA_KV_HEADS
GDN_DK = GDN_HEADS * HEAD_DIM
NSA_DQ = NSA_HEADS * HEAD_DIM
NSA_DKV = NSA_KV_HEADS * HEAD_DIM
GDN_CONV = 4
GDN_CHUNK = 64
CMP_BLOCK = 32
CMP_STRIDE = 16
SEL_BLOCK = 64
SEL_TOPN = 16
SEL_LOCAL = 2
WINDOW = 512
EPS = 1e-6
IN_SIZES = (GDN_DK, GDN_DK, GDN_DK, GDN_DK, GDN_HEADS, GDN_HEADS,
            NSA_DQ, NSA_DKV, NSA_DKV, NSA_DKV, NSA_DKV, NSA_DKV, NSA_DKV, 3 * NSA_HEADS)

VMEM_LIMIT_V7X = 56 * 1024 * 1024
LANES = 128
NSA_TQ = 128
NSA_TK = 512
NEG_INF = float("-inf")
NT_DIMS = (((1,), (1,)), ((), ()))
TN_DIMS = (((0,), (0,)), ((), ()))


def _params(sem):
    return pltpu.CompilerParams(dimension_semantics=sem, vmem_limit_bytes=VMEM_LIMIT_V7X)


def _mod_kernel(c_ref, w_ref, b_ref, o_ref):
    c = c_ref[...]
    s = (c * jax.nn.sigmoid(c)).astype(BF16)
    o_ref[...] = jnp.dot(s, w_ref[...].astype(BF16), preferred_element_type=F32) + b_ref[...]


def _modulation(c, w, b):
    bsz, d = c.shape
    n = w.shape[1]
    rows = 8
    cp = jnp.zeros((rows, d), F32).at[:bsz].set(c)
    tn = 512
    out = pl.pallas_call(
        _mod_kernel,
        out_shape=jax.ShapeDtypeStruct((rows, n), F32),
        grid=(n // tn,),
        in_specs=[pl.BlockSpec((rows, d), lambda j: (0, 0)),
                  pl.BlockSpec((d, tn), lambda j: (0, j)),
                  pl.BlockSpec((1, tn), lambda j: (0, j))],
        out_specs=pl.BlockSpec((rows, tn), lambda j: (0, j)),
        compiler_params=_params(("parallel",)),
        name="adaln_mod",
    )(cp, w, b.reshape(1, n))
    return out[:bsz]


def _norm_kernel(x_ref, g_ref, sc_ref, sh_ref, o_ref):
    x = x_ref[...]
    y = x * lax.rsqrt(jnp.mean(x * x, axis=-1, keepdims=True) + EPS) * g_ref[...]
    o_ref[...] = (y * (1.0 + sc_ref[0]) + sh_ref[0]).astype(o_ref.dtype)


def _final_norm_kernel(x_ref, g_ref, o_ref):
    x = x_ref[...]
    o_ref[...] = x * lax.rsqrt(jnp.mean(x * x, axis=-1, keepdims=True) + EPS) * g_ref[...]


def _norm_mod(x2, gain, scale, shift, seq, out_dtype):
    m, d = x2.shape
    tm = 256
    per_b = seq // tm
    bsz = m // seq
    return pl.pallas_call(
        _norm_kernel,
        out_shape=jax.ShapeDtypeStruct((m, d), out_dtype),
        grid=(m // tm,),
        in_specs=[pl.BlockSpec((tm, d), lambda i: (i, 0)),
                  pl.BlockSpec((1, d), lambda i: (0, 0)),
                  pl.BlockSpec((1, 1, d), lambda i: (i // per_b, 0, 0)),
                  pl.BlockSpec((1, 1, d), lambda i: (i // per_b, 0, 0))],
        out_specs=pl.BlockSpec((tm, d), lambda i: (i, 0)),
        compiler_params=_params(("parallel",)),
        name="rmsnorm_mod",
    )(x2, gain.reshape(1, d), scale.reshape(bsz, 1, d), shift.reshape(bsz, 1, d))


def _final_norm(x2, gain):
    m, d = x2.shape
    tm = 256
    return pl.pallas_call(
        _final_norm_kernel,
        out_shape=jax.ShapeDtypeStruct((m, d), F32),
        grid=(m // tm,),
        in_specs=[pl.BlockSpec((tm, d), lambda i: (i, 0)),
                  pl.BlockSpec((1, d), lambda i: (0, 0))],
        out_specs=pl.BlockSpec((tm, d), lambda i: (i, 0)),
        compiler_params=_params(("parallel",)),
        name="final_rmsnorm",
    )(x2, gain.reshape(1, d))


def _mm_kernel(a_ref, w_ref, o_ref, *, relu2):
    acc = jnp.dot(a_ref[...], w_ref[...], preferred_element_type=F32)
    if relu2:
        acc = jnp.square(jnp.maximum(acc, 0.0))
    o_ref[...] = acc.astype(o_ref.dtype)


def _matmul(a, w, *, tm, tn, out_dtype, relu2=False, name):
    m, k = a.shape
    n = w.shape[1]
    return pl.pallas_call(
        functools.partial(_mm_kernel, relu2=relu2),
        out_shape=jax.ShapeDtypeStruct((m, n), out_dtype),
        grid=(m // tm, n // tn),
        in_specs=[pl.BlockSpec((tm, k), lambda i, j: (i, 0)),
                  pl.BlockSpec((k, tn), lambda i, j: (0, j))],
        out_specs=pl.BlockSpec((tm, tn), lambda i, j: (i, j)),
        compiler_params=_params(("parallel", "parallel")),
        name=name,
    )(a, w)


def _mm_res_kernel(a_ref, w_ref, x_ref, g_ref, o_ref, acc_ref):
    k = pl.program_id(2)

    @pl.when(k == 0)
    def _():
        acc_ref[...] = jnp.zeros_like(acc_ref)

    acc_ref[...] += jnp.dot(a_ref[...], w_ref[...], preferred_element_type=F32)

    @pl.when(k == pl.num_programs(2) - 1)
    def _():
        o_ref[...] = x_ref[...] + g_ref[0] * acc_ref[...]


def _matmul_residual(a, w, x2, gate, seq, *, tm, tn, tk, name):
    m, kdim = a.shape
    n = w.shape[1]
    bsz = m // seq
    per_b = seq // tm
    return pl.pallas_call(
        _mm_res_kernel,
        out_shape=jax.ShapeDtypeStruct((m, n), F32),
        grid=(m // tm, n // tn, kdim // tk),
        in_specs=[pl.BlockSpec((tm, tk), lambda i, j, k: (i, k)),
                  pl.BlockSpec((tk, tn), lambda i, j, k: (k, j)),
                  pl.BlockSpec((tm, tn), lambda i, j, k: (i, j)),
                  pl.BlockSpec((1, 1, tn), lambda i, j, k: (i // per_b, 0, j))],
        out_specs=pl.BlockSpec((tm, tn), lambda i, j, k: (i, j)),
        scratch_shapes=[pltpu.VMEM((tm, tn), F32)],
        compiler_params=_params(("parallel", "parallel", "arbitrary")),
        name=name,
    )(a, w, x2, gate.reshape(bsz, 1, n))


def _gdn_kernel(q_ref, k_ref, v_ref, z_ref, ba_ref, cw_ref, alog_ref, dtb_ref, nw_ref, o_ref,
                xbuf_ref, s_ref):
    c, d, nh = GDN_CHUNK, HEAD_DIM, GDN_HEADS
    width = nh * d
    hist = 8

    @pl.when(pl.program_id(1) == 0)
    def _():
        xbuf_ref[:, 0:hist, :] = jnp.zeros((3, hist, width), F32)
        s_ref[...] = jnp.zeros_like(s_ref)

    conv = []
    for i, src in enumerate((q_ref, k_ref, v_ref)):
        xbuf_ref[i, hist:hist + c, :] = src[...]
        acc = cw_ref[GDN_CONV - 1:GDN_CONV, i * width:(i + 1) * width] * src[...]
        for j in range(GDN_CONV - 1):
            shift = GDN_CONV - 1 - j
            acc += cw_ref[j:j + 1, i * width:(i + 1) * width] * xbuf_ref[i, pl.ds(hist - shift, c), :]
        conv.append(acc * jax.nn.sigmoid(acc))
        xbuf_ref[i, 0:hist, :] = xbuf_ref[i, c:c + hist, :]
    cq, ck, cv = conv

    ba = ba_ref[...]
    beta = jax.nn.sigmoid(ba[:, 0:nh])
    g = -jnp.exp(alog_ref[...]) * jax.nn.softplus(ba[:, nh:2 * nh] + dtb_ref[...])
    r_i = lax.broadcasted_iota(I32, (c, c), 0)
    c_i = lax.broadcasted_iota(I32, (c, c), 1)
    causal = r_i >= c_i
    strict = r_i > c_i
    gam = jnp.dot(causal.astype(F32), g, precision=lax.Precision.HIGHEST, preferred_element_type=F32)
    gam_t = lax.dot_general(g, (r_i <= c_i).astype(F32), TN_DIMS, precision=lax.Precision.HIGHEST,
                            preferred_element_type=F32)
    g_last = gam[c - 1:c, :]
    egam = jnp.exp(gam)
    bexp = beta * egam
    kscale = jnp.exp(g_last - gam)
    eg_last = jnp.exp(g_last)
    nw = nw_ref[...]

    heads = range(nh)
    hsl = [slice(h * d, (h + 1) * d) for h in heads]
    hcol = [slice(h, h + 1) for h in heads]
    dot32 = functools.partial(jnp.dot, precision=lax.Precision.HIGHEST, preferred_element_type=F32)
    dot16 = lambda a, b: jnp.dot(a.astype(BF16), b.astype(BF16), preferred_element_type=F32)

    qn, kn, qk, lower, decay = [], [], [], [], []
    for h in heads:
        qh = cq[:, hsl[h]]
        kh = ck[:, hsl[h]]
        qh = qh * lax.rsqrt(jnp.sum(qh * qh, axis=-1, keepdims=True) + EPS) * (d ** -0.5)
        kh = kh * lax.rsqrt(jnp.sum(kh * kh, axis=-1, keepdims=True) + EPS)
        kb = kh.astype(BF16)
        both = lax.dot_general(jnp.concatenate([qh.astype(BF16), kb], axis=0), kb, NT_DIMS,
                               preferred_element_type=F32)
        diff = gam[:, hcol[h]] - gam_t[h:h + 1, :]
        dec = jnp.where(causal, jnp.exp(jnp.where(causal, diff, 0.0)), 0.0)
        qn.append(qh)
        kn.append(kh)
        decay.append(dec)
        qk.append(both[:c] * dec)
        lower.append(jnp.where(strict, beta[:, hcol[h]] * both[c:] * dec, 0.0))

    base = 8
    same = lambda s: (r_i // s) == (c_i // s)
    diag = [jnp.where(same(base), lo, 0.0) for lo in lower]
    p2 = [dot32(dg, dg) for dg in diag]
    qinv = [p - dg - dot32(dg, p) for dg, p in zip(diag, p2)]
    p4 = [dot32(p, p) for p in p2]
    qinv = [q + p + dot32(q, p) for q, p in zip(qinv, p4)]
    eye = (r_i == c_i).astype(F32)
    inv = [eye + q for q in qinv]
    size = base
    while size < c:
        off_mask = same(2 * size) & jnp.logical_not(same(size))
        off = [jnp.where(off_mask, lo, 0.0) for lo in lower]
        t1 = [dot32(o_, x_) for o_, x_ in zip(off, inv)]
        inv = [x_ - dot32(x_, t_) for x_, t_ in zip(inv, t1)]
        size *= 2

    sol = []
    for h in heads:
        rhs = jnp.concatenate([cv[:, hsl[h]] * beta[:, hcol[h]], kn[h] * bexp[:, hcol[h]]], axis=1)
        sol.append(rhs + dot16(inv[h] - eye, rhs))
    wq = [dot16(jnp.concatenate([sol[h][:, d:], qn[h] * egam[:, hcol[h]]], axis=0), s_ref[h])
          for h in heads]
    vnew = [(sol[h][:, :d] - wq[h][:c]).astype(BF16) for h in heads]
    for h in heads:
        k_dec = kn[h] * kscale[:, hcol[h]]
        s_ref[h] = s_ref[h] * eg_last[:, hcol[h]] + lax.dot_general(
            k_dec.astype(BF16), vnew[h], TN_DIMS, preferred_element_type=F32)
    for h in heads:
        o = wq[h][c:] + jnp.dot(qk[h].astype(BF16), vnew[h], preferred_element_type=F32)
        y = o * lax.rsqrt(jnp.mean(o * o, axis=-1, keepdims=True) + EPS) * nw
        zh = z_ref[:, hsl[h]]
        o_ref[:, hsl[h]] = (y * (zh * jax.nn.sigmoid(zh))).astype(o_ref.dtype)


def _gdn(pa, pb, ba_blk, conv_w, a_log, dt_bias, norm_w, seq):
    bsz = pa.shape[0]
    c, d, nh = GDN_CHUNK, HEAD_DIM, GDN_HEADS
    width = nh * d
    col_spec = lambda j: pl.BlockSpec((None, c, width), lambda b, n: (b, n, j))
    full = lambda shape: pl.BlockSpec(shape, lambda b, n: (0,) * len(shape))
    return pl.pallas_call(
        _gdn_kernel,
        out_shape=jax.ShapeDtypeStruct((bsz, seq, width), BF16),
        grid=(bsz, seq // c),
        in_specs=[col_spec(0), col_spec(1), col_spec(2), col_spec(3),
                  pl.BlockSpec((None, c, LANES), lambda b, n: (b, n, ba_blk)),
                  full((GDN_CONV, 3 * width)), full((1, nh)), full((1, nh)), full((1, d))],
        out_specs=pl.BlockSpec((None, c, width), lambda b, n: (b, n, 0)),
        scratch_shapes=[pltpu.VMEM((3, 8 + c, width), F32), pltpu.VMEM((nh, d, d), F32)],
        compiler_params=_params(("parallel", "arbitrary")),
        name="gdn_chunked",
    )(pa, pa, pa, pa, pb, conv_w, a_log.reshape(1, nh), dt_bias.reshape(1, nh), norm_w.reshape(1, d))


def _compress_kernel(x_ref, pos_ref, w1_ref, w2_ref, o_ref):
    n = o_ref.shape[0]
    d = x_ref.shape[1]
    acc_a = jnp.zeros((n, d), F32)
    acc_b = jnp.zeros((n, d), F32)
    for j in range(CMP_STRIDE):
        xj = x_ref[pl.ds(j, n, stride=CMP_STRIDE), :]
        lo = (xj + pos_ref[j:j + 1, :]).astype(BF16)
        hi = (xj + pos_ref[CMP_STRIDE + j:CMP_STRIDE + j + 1, :]).astype(BF16)
        acc_a += jnp.dot(lo, w1_ref[j * d:(j + 1) * d, :].astype(BF16), preferred_element_type=F32)
        acc_b += jnp.dot(hi, w1_ref[(CMP_STRIDE + j) * d:(CMP_STRIDE + j + 1) * d, :].astype(BF16),
                         preferred_element_type=F32)
    pre = acc_a + pltpu.roll(acc_b, n - 1, 0)
    hid = (pre * jax.nn.sigmoid(pre)).astype(BF16)
    out = jnp.dot(hid, w2_ref[...].astype(BF16), preferred_element_type=F32)
    row = lax.broadcasted_iota(I32, (n, d), 0)
    o_ref[...] = jnp.where(row < n - 1, out, 0.0).astype(o_ref.dtype)


def _compress(kvc, pos, w1, w2, seq):
    bsz = kvc.shape[0]
    g = NSA_KV_HEADS
    n = seq // CMP_STRIDE
    d = HEAD_DIM
    return pl.pallas_call(
        _compress_kernel,
        out_shape=jax.ShapeDtypeStruct((2, bsz, g, n, d), BF16),
        grid=(2, bsz, g),
        in_specs=[pl.BlockSpec((None, seq, d), lambda s, b, h: (b, 0, s * NSA_KV_HEADS + h)),
                  pl.BlockSpec((None, CMP_BLOCK, d), lambda s, b, h: (s, 0, 0)),
                  pl.BlockSpec((None, CMP_BLOCK * d, d), lambda s, b, h: (s, 0, 0)),
                  pl.BlockSpec((None, d, d), lambda s, b, h: (s, 0, 0))],
        out_specs=pl.BlockSpec((None, None, None, n, d), lambda s, b, h: (s, b, h, 0, 0)),
        compiler_params=_params(("parallel", "parallel", "parallel")),
        name="nsa_compress",
    )(kvc, pos, w1, w2)


def _softmax_rows(s, mask):
    s = jnp.where(mask, s, NEG_INF)
    m = jnp.max(s, axis=-1, keepdims=True)
    m = jnp.where(jnp.abs(m) < jnp.inf, m, 0.0)
    e = jnp.exp(s - m)
    d = jnp.sum(e, axis=-1, keepdims=True)
    return e * (1.0 / jnp.where(d > 0, d, 1.0))


def _nsa_kernel(slopes_ref, q_ref, kc_ref, vc_ref, ks_ref, vs_ref, kw_ref, vw_ref, gate_ref, o_ref):
    seq = ks_ref.shape[0]
    tq, tk, rep, d = NSA_TQ, NSA_TK, NSA_REP, HEAD_DIM
    rows = rep * tq
    g = pl.program_id(1)
    t0 = pl.program_id(2) * tq

    q = q_ref[...] * (d ** -0.5)
    qs = jnp.concatenate([q[:, r * d:(r + 1) * d] for r in range(rep)], axis=0).astype(BF16)
    slope = jnp.concatenate([jnp.full((tq, 1), slopes_ref[g * rep + r], F32) for r in range(rep)], axis=0)
    row = lax.broadcasted_iota(I32, (rows, 1), 0)
    t_col = t0 + (row & (tq - 1))

    ncp = kc_ref.shape[0]
    n_idx = lax.broadcasted_iota(I32, (rows, ncp), 1)
    dist_c = t_col - (n_idx * CMP_STRIDE + (CMP_BLOCK - 1))
    s_c = lax.dot_general(qs, kc_ref[...], NT_DIMS, preferred_element_type=F32)
    s_c = s_c - slope * dist_c.astype(F32)
    p_c = _softmax_rows(s_c, dist_c >= 0).astype(BF16)
    o_c = jnp.dot(p_c, vc_ref[...], preferred_element_type=F32)

    n_sel = seq // SEL_BLOCK
    ov_s = lax.broadcasted_iota(I32, (n_sel, ncp), 0) * SEL_BLOCK
    ov_n = lax.broadcasted_iota(I32, (n_sel, ncp), 1) * CMP_STRIDE
    ov_t = ((ov_n <= ov_s + (SEL_BLOCK - 1)) & (ov_n + (CMP_BLOCK - 1) >= ov_s)).astype(BF16)
    imp = jnp.zeros((n_sel, tq), F32)
    for r in range(rep):
        imp += lax.dot_general(ov_t, p_c[r * tq:(r + 1) * tq, :], NT_DIMS, preferred_element_type=F32)
    blk = lax.broadcasted_iota(I32, (n_sel, tq), 0)
    cur = (t0 + lax.broadcasted_iota(I32, (n_sel, tq), 1)) // SEL_BLOCK
    forced = (blk == 0) | ((cur - blk) < SEL_LOCAL)
    imp = jnp.where(forced, jnp.inf, imp)
    imp = jnp.where(blk <= cur, imp, NEG_INF)
    rank = jnp.zeros((n_sel, tq), I32)
    for i in range(n_sel):
        ri = imp[i:i + 1, :]
        rank += ((ri > imp) | ((ri == imp) & (blk > i))).astype(I32)
    sel_t = (rank < min(SEL_TOPN, n_sel)).astype(BF16)

    def sel_body(j, carry):
        m, l, acc = carry
        k0 = pl.multiple_of(j * tk, tk)
        kt = ks_ref[pl.ds(k0, tk), :]
        vt = vs_ref[pl.ds(k0, tk), :]
        s = lax.dot_general(qs, kt, NT_DIMS, preferred_element_type=F32)
        dist = t_col - (k0 + lax.broadcasted_iota(I32, (rows, tk), 1))
        s = s - slope * dist.astype(F32)
        e_blk = lax.broadcasted_iota(I32, (n_sel, tk), 0)
        e_key = (k0 + lax.broadcasted_iota(I32, (n_sel, tk), 1)) // SEL_BLOCK
        expand = (e_blk == e_key).astype(BF16)
        picked = lax.dot_general(sel_t, expand, TN_DIMS, preferred_element_type=F32)
        picked = jnp.concatenate([picked] * rep, axis=0)
        s = jnp.where((dist >= 0) & (picked > 0.5), s, NEG_INF)
        m_new = jnp.maximum(m, jnp.max(s, axis=-1, keepdims=True))
        m_safe = jnp.where(m_new > NEG_INF, m_new, 0.0)
        alpha = jnp.exp(m - m_safe)
        p = jnp.exp(s - m_safe)
        l = alpha * l + jnp.sum(p, axis=-1, keepdims=True)
        acc = alpha * acc + jnp.dot(p.astype(BF16), vt, preferred_element_type=F32)
        return m_new, l, acc

    n_tiles = (t0 + tq + tk - 1) // tk
    init = (jnp.full((rows, 1), NEG_INF, F32), jnp.zeros((rows, 1), F32), jnp.zeros((rows, d), F32))
    _, l_s, acc_s = lax.fori_loop(0, n_tiles, sel_body, init)
    o_s = acc_s * (1.0 / jnp.where(l_s > 0, l_s, 1.0))

    wk = tq + WINDOW
    start = pl.multiple_of(jnp.maximum(t0 - WINDOW, 0), tq)
    kt = kw_ref[pl.ds(start, wk), :]
    vt = vw_ref[pl.ds(start, wk), :]
    s_w = lax.dot_general(qs, kt, NT_DIMS, preferred_element_type=F32)
    dist_w = t_col - (start + lax.broadcasted_iota(I32, (rows, wk), 1))
    s_w = s_w - slope * dist_w.astype(F32)
    p_w = _softmax_rows(s_w, (dist_w >= 0) & (dist_w < WINDOW)).astype(BF16)
    o_w = jnp.dot(p_w, vt, preferred_element_type=F32)

    gt = jax.nn.sigmoid(gate_ref[...])
    for r in range(rep):
        sl = slice(r * tq, (r + 1) * tq)
        o = (gt[:, 3 * r:3 * r + 1] * o_c[sl] + gt[:, 3 * r + 1:3 * r + 2] * o_s[sl]
             + gt[:, 3 * r + 2:3 * r + 3] * o_w[sl])
        o_ref[:, r * d:(r + 1) * d] = o.astype(o_ref.dtype)


def _nsa_attention(qa, q_col0, cmp, kvb, gates, slopes, seq):
    bsz = qa.shape[0]
    g, rep, d, tq = NSA_KV_HEADS, NSA_REP, HEAD_DIM, NSA_TQ
    ncp = seq // CMP_STRIDE
    qblk0 = q_col0 // (rep * d)
    kv_spec = lambda s: pl.BlockSpec((None, seq, d), lambda b, h, i: (b, 0, s * NSA_KV_HEADS + h))
    cmp_spec = lambda s: pl.BlockSpec((None, None, None, ncp, d), lambda b, h, i: (s, b, h, 0, 0))
    return pl.pallas_call(
        _nsa_kernel,
        out_shape=jax.ShapeDtypeStruct((bsz, seq, NSA_DQ), BF16),
        grid=(bsz, g, seq // tq),
        in_specs=[pl.BlockSpec(memory_space=pltpu.SMEM),
                  pl.BlockSpec((None, tq, rep * d), lambda b, h, i: (b, i, qblk0 + h)),
                  cmp_spec(0), cmp_spec(1), kv_spec(0), kv_spec(1), kv_spec(2), kv_spec(3),
                  pl.BlockSpec((None, None, tq, 3 * rep), lambda b, h, i: (b, h, i, 0))],
        out_specs=pl.BlockSpec((None, tq, rep * d), lambda b, h, i: (b, i, h)),
        compiler_params=_params(("parallel", "parallel", "arbitrary")),
        name="nsa_attention",
    )(slopes, qa, cmp, cmp, kvb, kvb, kvb, kvb, gates)


def _split_columns(proj):
    out, start = [], 0
    for size in IN_SIZES:
        out.append(proj[..., start:start + size])
        start += size
    return out


def kernel(x, c, ada_w, ada_b, norm1_w, w_in, gdn_conv_w, gdn_a_log, gdn_dt_bias, gdn_norm_w,
           cmp_pos_k, cmp_w1_k, cmp_w2_k, cmp_pos_v, cmp_w1_v, cmp_w2_v, w_out, norm2_w,
           w_up, w_down, final_norm_w):
    bsz, seq, d = x.shape
    depth = ada_w.shape[0]
    m = bsz * seq
    x2 = x.reshape(m, d)
    slopes = 2.0 ** (-8.0 * jnp.arange(1, NSA_HEADS + 1, dtype=F32) / NSA_HEADS)
    for i in range(depth):
        mod = _modulation(c, ada_w[i], ada_b[i])
        sh1, sc1, g1, sh2, sc2, g2 = jnp.split(mod, 6, axis=-1)
        h = _norm_mod(x2, norm1_w[i], sc1, sh1, seq, BF16)
        cols = _split_columns(w_in[i])
        small = jnp.concatenate([cols[4], cols[5], cols[13]], axis=-1)
        small = jnp.pad(small, ((0, 0), (0, LANES - small.shape[1])))
        w_a = jnp.concatenate(cols[0:4] + [cols[6]], axis=-1).astype(BF16)
        w_b = jnp.concatenate(cols[7:9] + [small], axis=-1).astype(BF16)
        w_c = jnp.concatenate(cols[9:13], axis=-1).astype(BF16)
        pa = _matmul(h, w_a, tm=1024, tn=1024, out_dtype=F32, name="in_proj_a").reshape(bsz, seq, -1)
        pb = _matmul(h, w_b, tm=1024, tn=w_b.shape[1], out_dtype=F32, name="in_proj_b").reshape(bsz, seq, -1)
        pc = _matmul(h, w_c, tm=1024, tn=1024, out_dtype=BF16, name="in_proj_c").reshape(bsz, seq, -1)

        o_a = _gdn(pa, pb, 2 * NSA_DKV // LANES, gdn_conv_w[i], gdn_a_log[i], gdn_dt_bias[i], gdn_norm_w[i], seq)
        ngate = pb[..., 2 * NSA_DKV + 2 * GDN_HEADS:2 * NSA_DKV + 2 * GDN_HEADS + 3 * NSA_HEADS]

        cmp = _compress(pb, jnp.stack([cmp_pos_k[i], cmp_pos_v[i]]), jnp.stack([cmp_w1_k[i], cmp_w1_v[i]]),
                        jnp.stack([cmp_w2_k[i], cmp_w2_v[i]]), seq)
        gates = ngate.reshape(bsz, seq, NSA_KV_HEADS, 3 * NSA_REP).transpose(0, 2, 1, 3)
        o_b = _nsa_attention(pa, 4 * GDN_DK, cmp, pc, gates, slopes, seq)

        mix = jnp.concatenate([o_a, o_b], axis=-1).reshape(m, -1)
        x2 = _matmul_residual(mix, w_out[i].astype(BF16), x2, g1, seq,
                              tm=1024, tn=1024, tk=2048, name="out_proj")
        h2 = _norm_mod(x2, norm2_w[i], sc2, sh2, seq, BF16)
        hid = _matmul(h2, w_up[i].astype(BF16), tm=1024, tn=1024, out_dtype=BF16, relu2=True, name="mlp_up")
        x2 = _matmul_residual(hid, w_down[i].astype(BF16), x2, g2, seq,
                              tm=1024, tn=1024, tk=2048, name="mlp_down")
    return _final_norm(x2, final_norm_w).reshape(bsz, seq, d)
```

```python
import functools

import jax
import jax.numpy as jnp
from jax import lax
from jax.experimental import pallas as pl
from jax.experimental.pallas import tpu as pltpu

F32 = jnp.float32
BF16 = jnp.bfloat16
I32 = jnp.int32

HEAD_DIM = 128
GDN_HEADS = 16
NSA_HEADS = 16
NSA_KV_HEADS = 4
NSA_REP = NSA_HEADS // NSA_KV_HEADS
GDN_DK = GDN_HEADS * HEAD_DIM
NSA_DQ = NSA_HEADS * HEAD_DIM
NSA_DKV = NSA_KV_HEADS * HEAD_DIM
GDN_CONV = 4
GDN_CHUNK = 64
CMP_BLOCK = 32
CMP_STRIDE = 16
SEL_BLOCK = 64
SEL_TOPN = 16
SEL_LOCAL = 2
WINDOW = 512
EPS = 1e-6
IN_SIZES = (GDN_DK, GDN_DK, GDN_DK, GDN_DK, GDN_HEADS, GDN_HEADS,
            NSA_DQ, NSA_DKV, NSA_DKV, NSA_DKV, NSA_DKV, NSA_DKV, NSA_DKV, 3 * NSA_HEADS)

VMEM_LIMIT_V7X = 56 * 1024 * 1024
LANES = 128
NSA_TQ = 128
NSA_TK = 1024
NEG_INF = float("-inf")
AUX_SEL_LANES = 64
SEL_BIAS = 2.0 ** 100
NT_DIMS = (((1,), (1,)), ((), ()))
TN_DIMS = (((0,), (0,)), ((), ()))


def _params(sem):
    return pltpu.CompilerParams(dimension_semantics=sem, vmem_limit_bytes=VMEM_LIMIT_V7X)


def _mod_kernel(c_ref, w_ref, b_ref, o_ref):
    c = c_ref[...]
    s = (c * jax.nn.sigmoid(c)).astype(BF16)
    o_ref[...] = jnp.dot(s, w_ref[...].astype(BF16), preferred_element_type=F32) + b_ref[...]


def _modulation(c, w, b):
    bsz, d = c.shape
    n = w.shape[1]
    rows = 8
    cp = jnp.zeros((rows, d), F32).at[:bsz].set(c)
    tn = 512
    out = pl.pallas_call(
        _mod_kernel,
        out_shape=jax.ShapeDtypeStruct((rows, n), F32),
        grid=(n // tn,),
        in_specs=[pl.BlockSpec((rows, d), lambda j: (0, 0)),
                  pl.BlockSpec((d, tn), lambda j: (0, j)),
                  pl.BlockSpec((1, tn), lambda j: (0, j))],
        out_specs=pl.BlockSpec((rows, tn), lambda j: (0, j)),
        compiler_params=_params(("parallel",)),
        name="adaln_mod",
    )(cp, w, b.reshape(1, n))
    return out[:bsz]


def _norm_kernel(x_ref, g_ref, sc_ref, sh_ref, o_ref):
    x = x_ref[...]
    y = x * lax.rsqrt(jnp.mean(x * x, axis=-1, keepdims=True) + EPS) * g_ref[...]
    o_ref[...] = (y * (1.0 + sc_ref[0]) + sh_ref[0]).astype(o_ref.dtype)


def _final_norm_kernel(x_ref, g_ref, o_ref):
    x = x_ref[...]
    o_ref[...] = x * lax.rsqrt(jnp.mean(x * x, axis=-1, keepdims=True) + EPS) * g_ref[...]


def _norm_mod(x2, gain, scale, shift, seq, out_dtype):
    m, d = x2.shape
    tm = 256
    per_b = seq // tm
    bsz = m // seq
    return pl.pallas_call(
        _norm_kernel,
        out_shape=jax.ShapeDtypeStruct((m, d), out_dtype),
        grid=(m // tm,),
        in_specs=[pl.BlockSpec((tm, d), lambda i: (i, 0)),
                  pl.BlockSpec((1, d), lambda i: (0, 0)),
                  pl.BlockSpec((1, 1, d), lambda i: (i // per_b, 0, 0)),
                  pl.BlockSpec((1, 1, d), lambda i: (i // per_b, 0, 0))],
        out_specs=pl.BlockSpec((tm, d), lambda i: (i, 0)),
        compiler_params=_params(("parallel",)),
        name="rmsnorm_mod",
    )(x2, gain.reshape(1, d), scale.reshape(bsz, 1, d), shift.reshape(bsz, 1, d))


def _final_norm(x2, gain):
    m, d = x2.shape
    tm = 256
    return pl.pallas_call(
        _final_norm_kernel,
        out_shape=jax.ShapeDtypeStruct((m, d), F32),
        grid=(m // tm,),
        in_specs=[pl.BlockSpec((tm, d), lambda i: (i, 0)),
                  pl.BlockSpec((1, d), lambda i: (0, 0))],
        out_specs=pl.BlockSpec((tm, d), lambda i: (i, 0)),
        compiler_params=_params(("parallel",)),
        name="final_rmsnorm",
    )(x2, gain.reshape(1, d))


def _mm_kernel(a_ref, w_ref, o_ref, *, relu2):
    acc = jnp.dot(a_ref[...], w_ref[...], preferred_element_type=F32)
    if relu2:
        acc = jnp.square(jnp.maximum(acc, 0.0))
    o_ref[...] = acc.astype(o_ref.dtype)


def _matmul(a, w, *, tm, tn, out_dtype, relu2=False, name):
    m, k = a.shape
    n = w.shape[1]
    return pl.pallas_call(
        functools.partial(_mm_kernel, relu2=relu2),
        out_shape=jax.ShapeDtypeStruct((m, n), out_dtype),
        grid=(m // tm, n // tn),
        in_specs=[pl.BlockSpec((tm, k), lambda i, j: (i, 0)),
                  pl.BlockSpec((k, tn), lambda i, j: (0, j))],
        out_specs=pl.BlockSpec((tm, tn), lambda i, j: (i, j)),
        compiler_params=_params(("parallel", "parallel")),
        name=name,
    )(a, w)


def _mm_res_kernel(*refs, nparts):
    a_refs = refs[:nparts]
    w_ref, x_ref, g_ref, o_ref, acc_ref = refs[nparts:]
    k = pl.program_id(2)

    @pl.when(k == 0)
    def _():
        acc_ref[...] = jnp.zeros_like(acc_ref)

    if nparts == 1:
        acc_ref[...] += jnp.dot(a_refs[0][...], w_ref[...], preferred_element_type=F32)
    else:
        for idx, a_ref in enumerate(a_refs):
            @pl.when(k == idx)
            def _(a_ref=a_ref):
                acc_ref[...] += jnp.dot(a_ref[...], w_ref[...], preferred_element_type=F32)

    @pl.when(k == pl.num_programs(2) - 1)
    def _():
        o_ref[...] = x_ref[...] + g_ref[0] * acc_ref[...]


def _matmul_residual(a_parts, w, x2, gate, seq, *, tm, tn, tk, name):
    nparts = len(a_parts)
    m = a_parts[0].shape[0]
    kdim = sum(a.shape[1] for a in a_parts)
    n = w.shape[1]
    bsz = m // seq
    per_b = seq // tm
    if nparts == 1:
        a_specs = [pl.BlockSpec((tm, tk), lambda i, j, k: (i, k))]
    else:
        assert all(a.shape[1] == tk for a in a_parts)
        a_specs = [pl.BlockSpec((tm, tk), lambda i, j, k: (i, 0)) for _ in a_parts]
    return pl.pallas_call(
        functools.partial(_mm_res_kernel, nparts=nparts),
        out_shape=jax.ShapeDtypeStruct((m, n), F32),
        grid=(m // tm, n // tn, kdim // tk),
        in_specs=a_specs + [pl.BlockSpec((tk, tn), lambda i, j, k: (k, j)),
                            pl.BlockSpec((tm, tn), lambda i, j, k: (i, j)),
                            pl.BlockSpec((1, 1, tn), lambda i, j, k: (i // per_b, 0, j))],
        out_specs=pl.BlockSpec((tm, tn), lambda i, j, k: (i, j)),
        scratch_shapes=[pltpu.VMEM((tm, tn), F32)],
        compiler_params=_params(("parallel", "parallel", "arbitrary")),
        name=name,
    )(*a_parts, w, x2, gate.reshape(bsz, 1, n))


def _gdn_kernel(q_ref, k_ref, v_ref, z_ref, ba_ref, cw_ref, alog_ref, dtb_ref, nw_ref, o_ref,
                xbuf_ref, s_ref):
    c, d, nh = GDN_CHUNK, HEAD_DIM, GDN_HEADS
    width = nh * d
    hist = 8

    @pl.when(pl.program_id(1) == 0)
    def _():
        xbuf_ref[:, 0:hist, :] = jnp.zeros((3, hist, width), F32)
        s_ref[...] = jnp.zeros_like(s_ref)

    conv = []
    for i, src in enumerate((q_ref, k_ref, v_ref)):
        xbuf_ref[i, hist:hist + c, :] = src[...]
        acc = cw_ref[GDN_CONV - 1:GDN_CONV, i * width:(i + 1) * width] * src[...]
        for j in range(GDN_CONV - 1):
            shift = GDN_CONV - 1 - j
            acc += cw_ref[j:j + 1, i * width:(i + 1) * width] * xbuf_ref[i, pl.ds(hist - shift, c), :]
        conv.append(acc * jax.nn.sigmoid(acc))
        xbuf_ref[i, 0:hist, :] = xbuf_ref[i, c:c + hist, :]
    cq, ck, cv = conv

    ba = ba_ref[...]
    beta = jax.nn.sigmoid(ba[:, 0:nh])
    g = -jnp.exp(alog_ref[...]) * jax.nn.softplus(ba[:, nh:2 * nh] + dtb_ref[...])
    r_i = lax.broadcasted_iota(I32, (c, c), 0)
    c_i = lax.broadcasted_iota(I32, (c, c), 1)
    causal = r_i >= c_i
    strict = r_i > c_i
    gam = jnp.dot(causal.astype(F32), g, precision=lax.Precision.HIGHEST, preferred_element_type=F32)
    gam_t = lax.dot_general(g, (r_i <= c_i).astype(F32), TN_DIMS, precision=lax.Precision.HIGHEST,
                            preferred_element_type=F32)
    g_last = gam[c - 1:c, :]
    egam = jnp.exp(gam)
    bexp = beta * egam
    kscale = jnp.exp(g_last - gam)
    eg_last = jnp.exp(g_last)
    nw = nw_ref[...]

    heads = range(nh)
    hsl = [slice(h * d, (h + 1) * d) for h in heads]
    hcol = [slice(h, h + 1) for h in heads]
    dot16 = lambda a, b: jnp.dot(a.astype(BF16), b.astype(BF16), preferred_element_type=F32)

    def dot32(a, b):
        ah = a.astype(BF16).astype(F32)
        bh = b.astype(BF16).astype(F32)
        lhs = jnp.concatenate([ah, ah, a - ah], axis=1).astype(BF16)
        rhs = jnp.concatenate([bh, b - bh, bh], axis=0).astype(BF16)
        return jnp.dot(lhs, rhs, preferred_element_type=F32)

    qn, kn, qk, lower, decay = [], [], [], [], []
    for h in heads:
        qh = cq[:, hsl[h]]
        kh = ck[:, hsl[h]]
        qh = qh * lax.rsqrt(jnp.sum(qh * qh, axis=-1, keepdims=True) + EPS) * (d ** -0.5)
        kh = kh * lax.rsqrt(jnp.sum(kh * kh, axis=-1, keepdims=True) + EPS)
        kb = kh.astype(BF16)
        both = lax.dot_general(jnp.concatenate([qh.astype(BF16), kb], axis=0), kb, NT_DIMS,
                               preferred_element_type=F32)
        diff = gam[:, hcol[h]] - gam_t[h:h + 1, :]
        dec = jnp.where(causal, jnp.exp(jnp.where(causal, diff, 0.0)), 0.0)
        qn.append(qh)
        kn.append(kh)
        decay.append(dec)
        qk.append(both[:c] * dec)
        lower.append(jnp.where(strict, beta[:, hcol[h]] * both[c:] * dec, 0.0))

    base = 8
    same = lambda s: (r_i // s) == (c_i // s)
    diag = [jnp.where(same(base), lo, 0.0) for lo in lower]
    p2 = [dot32(dg, dg) for dg in diag]
    qinv = [p - dg - dot32(dg, p) for dg, p in zip(diag, p2)]
    p4 = [dot32(p, p) for p in p2]
    qinv = [q + p + dot32(q, p) for q, p in zip(qinv, p4)]
    eye = (r_i == c_i).astype(F32)
    inv = [eye + q for q in qinv]
    size = base
    while size < c:
        off_mask = same(2 * size) & jnp.logical_not(same(size))
        off = [jnp.where(off_mask, lo, 0.0) for lo in lower]
        t1 = [dot32(o_, x_) for o_, x_ in zip(off, inv)]
        inv = [x_ - dot32(x_, t_) for x_, t_ in zip(inv, t1)]
        size *= 2

    sol = []
    for h in heads:
        rhs = jnp.concatenate([cv[:, hsl[h]] * beta[:, hcol[h]], kn[h] * bexp[:, hcol[h]]], axis=1)
        sol.append(rhs + dot16(inv[h] - eye, rhs))
    wq = [dot16(jnp.concatenate([sol[h][:, d:], qn[h] * egam[:, hcol[h]]], axis=0), s_ref[h])
          for h in heads]
    vnew = [(sol[h][:, :d] - wq[h][:c]).astype(BF16) for h in heads]
    for h in heads:
        k_dec = kn[h] * kscale[:, hcol[h]]
        s_ref[h] = s_ref[h] * eg_last[:, hcol[h]] + lax.dot_general(
            k_dec.astype(BF16), vnew[h], TN_DIMS, preferred_element_type=F32)
    for h in heads:
        o = wq[h][c:] + jnp.dot(qk[h].astype(BF16), vnew[h], preferred_element_type=F32)
        y = o * lax.rsqrt(jnp.mean(o * o, axis=-1, keepdims=True) + EPS) * nw
        zh = z_ref[:, hsl[h]]
        o_ref[:, hsl[h]] = (y * (zh * jax.nn.sigmoid(zh))).astype(o_ref.dtype)


def _gdn(pa, pb, ba_blk, conv_w, a_log, dt_bias, norm_w, seq):
    bsz = pa.shape[0]
    c, d, nh = GDN_CHUNK, HEAD_DIM, GDN_HEADS
    width = nh * d
    col_spec = lambda j: pl.BlockSpec((None, c, width), lambda b, n: (b, n, j))
    full = lambda shape: pl.BlockSpec(shape, lambda b, n: (0,) * len(shape))
    return pl.pallas_call(
        _gdn_kernel,
        out_shape=jax.ShapeDtypeStruct((bsz, seq, width), BF16),
        grid=(bsz, seq // c),
        in_specs=[col_spec(0), col_spec(1), col_spec(2), col_spec(3),
                  pl.BlockSpec((None, c, LANES), lambda b, n: (b, n, ba_blk)),
                  full((GDN_CONV, 3 * width)), full((1, nh)), full((1, nh)), full((1, d))],
        out_specs=pl.BlockSpec((None, c, width), lambda b, n: (b, n, 0)),
        scratch_shapes=[pltpu.VMEM((3, 8 + c, width), F32), pltpu.VMEM((nh, d, d), F32)],
        compiler_params=_params(("parallel", "arbitrary")),
        name="gdn_chunked",
    )(pa, pa, pa, pa, pb, conv_w, a_log.reshape(1, nh), dt_bias.reshape(1, nh), norm_w.reshape(1, d))


def _compress_kernel(x_ref, pos_ref, w1_ref, w2_ref, o_ref):
    n = o_ref.shape[0]
    d = x_ref.shape[1]
    acc_a = jnp.zeros((n, d), F32)
    acc_b = jnp.zeros((n, d), F32)
    for j in range(CMP_STRIDE):
        xj = x_ref[pl.ds(j, n, stride=CMP_STRIDE), :]
        lo = (xj + pos_ref[j:j + 1, :]).astype(BF16)
        hi = (xj + pos_ref[CMP_STRIDE + j:CMP_STRIDE + j + 1, :]).astype(BF16)
        acc_a += jnp.dot(lo, w1_ref[j * d:(j + 1) * d, :].astype(BF16), preferred_element_type=F32)
        acc_b += jnp.dot(hi, w1_ref[(CMP_STRIDE + j) * d:(CMP_STRIDE + j + 1) * d, :].astype(BF16),
                         preferred_element_type=F32)
    pre = acc_a + pltpu.roll(acc_b, n - 1, 0)
    hid = (pre * jax.nn.sigmoid(pre)).astype(BF16)
    out = jnp.dot(hid, w2_ref[...].astype(BF16), preferred_element_type=F32)
    row = lax.broadcasted_iota(I32, (n, d), 0)
    o_ref[...] = jnp.where(row < n - 1, out, 0.0).astype(o_ref.dtype)


def _compress(kvc, pos, w1, w2, seq):
    bsz = kvc.shape[0]
    g = NSA_KV_HEADS
    n = seq // CMP_STRIDE
    d = HEAD_DIM
    return pl.pallas_call(
        _compress_kernel,
        out_shape=jax.ShapeDtypeStruct((2, bsz, g, n, d), BF16),
        grid=(2, bsz, g),
        in_specs=[pl.BlockSpec((None, seq, d), lambda s, b, h: (b, 0, s * NSA_KV_HEADS + h)),
                  pl.BlockSpec((None, CMP_BLOCK, d), lambda s, b, h: (s, 0, 0)),
                  pl.BlockSpec((None, CMP_BLOCK * d, d), lambda s, b, h: (s, 0, 0)),
                  pl.BlockSpec((None, d, d), lambda s, b, h: (s, 0, 0))],
        out_specs=pl.BlockSpec((None, None, None, n, d), lambda s, b, h: (s, b, h, 0, 0)),
        compiler_params=_params(("parallel", "parallel", "parallel")),
        name="nsa_compress",
    )(kvc, pos, w1, w2)


def _softmax_rows(s, mask):
    s = jnp.where(mask, s, NEG_INF)
    m = jnp.max(s, axis=-1, keepdims=True)
    m = jnp.where(jnp.abs(m) < jnp.inf, m, 0.0)
    e = jnp.exp(s - m)
    d = jnp.sum(e, axis=-1, keepdims=True)
    return e * (1.0 / jnp.where(d > 0, d, 1.0))


def _nsa_key_aux(seq):
    key = jnp.arange(seq, dtype=I32)[:, None]
    lane = jnp.arange(LANES, dtype=I32)[None, :]
    onehot = (lane == key // SEL_BLOCK) & (lane < AUX_SEL_LANES)
    hi = (lane >= AUX_SEL_LANES) & (lane < AUX_SEL_LANES + 3)
    lo = (lane >= AUX_SEL_LANES + 3) & (lane < AUX_SEL_LANES + 6)
    aux = jnp.where(onehot, 1, 0) + jnp.where(hi, (key // 64) * 64, 0) + jnp.where(lo, key % 64, 0)
    return aux.astype(BF16)


def _nsa_query_aux(slopes):
    s1 = slopes.astype(BF16).astype(F32)
    s2 = (slopes - s1).astype(BF16).astype(F32)
    s3 = (slopes - s1 - s2).astype(BF16).astype(F32)
    pieces = jnp.stack([s1, s2, s3, s1, s2, s3], axis=-1)
    aux = jnp.zeros((slopes.shape[0], LANES), F32).at[:, AUX_SEL_LANES:AUX_SEL_LANES + 6].set(pieces)
    return aux.reshape(NSA_KV_HEADS, NSA_REP, LANES)


def _nsa_kernel(slopes_ref, qaux_ref, q_ref, kc_ref, vc_ref, ks_ref, vs_ref, kw_ref, vw_ref, gate_ref,
                o_ref):
    seq = ks_ref.shape[0]
    tq, tk, rep, d = NSA_TQ, NSA_TK, NSA_REP, HEAD_DIM
    rows = rep * tq
    g = pl.program_id(1)
    t0 = pl.program_id(2) * tq

    q = q_ref[...] * (d ** -0.5)
    qs = jnp.concatenate([q[:, r * d:(r + 1) * d] for r in range(rep)], axis=0).astype(BF16)
    slope = jnp.concatenate([jnp.full((tq, 1), slopes_ref[g * rep + r], F32) for r in range(rep)], axis=0)
    row = lax.broadcasted_iota(I32, (rows, 1), 0)
    t_col = t0 + (row & (tq - 1))

    ncp = kc_ref.shape[0]
    n_idx = lax.broadcasted_iota(I32, (rows, ncp), 1)
    dist_c = t_col - (n_idx * CMP_STRIDE + (CMP_BLOCK - 1))
    s_c = lax.dot_general(qs, kc_ref[...], NT_DIMS, preferred_element_type=F32)
    s_c = s_c - slope * dist_c.astype(F32)
    p_c = _softmax_rows(s_c, dist_c >= 0).astype(BF16)
    o_c = jnp.dot(p_c, vc_ref[...], preferred_element_type=F32)

    n_sel = seq // SEL_BLOCK
    ov_s = lax.broadcasted_iota(I32, (n_sel, ncp), 0) * SEL_BLOCK
    ov_n = lax.broadcasted_iota(I32, (n_sel, ncp), 1) * CMP_STRIDE
    ov_t = ((ov_n <= ov_s + (SEL_BLOCK - 1)) & (ov_n + (CMP_BLOCK - 1) >= ov_s)).astype(BF16)
    imp = jnp.zeros((n_sel, tq), F32)
    for r in range(rep):
        imp += lax.dot_general(ov_t, p_c[r * tq:(r + 1) * tq, :], NT_DIMS, preferred_element_type=F32)
    blk = lax.broadcasted_iota(I32, (n_sel, tq), 0)
    cur = (t0 + lax.broadcasted_iota(I32, (n_sel, tq), 1)) // SEL_BLOCK
    forced = (blk == 0) | ((cur - blk) < SEL_LOCAL)
    imp = jnp.where(forced, jnp.inf, imp)
    imp = jnp.where(blk <= cur, imp, NEG_INF)
    rank = jnp.zeros((n_sel, tq), I32)
    for i in range(n_sel):
        ri = imp[i:i + 1, :]
        rank += ((ri > imp) | ((ri == imp) & (blk > i))).astype(I32)
    sel_t = (rank < min(SEL_TOPN, n_sel)).astype(F32)

    to_lanes = (lax.broadcasted_iota(I32, (n_sel, LANES), 0)
                == lax.broadcasted_iota(I32, (n_sel, LANES), 1)).astype(BF16)
    picked = lax.dot_general(sel_t.astype(BF16), to_lanes, TN_DIMS, preferred_element_type=F32)
    lane = lax.broadcasted_iota(I32, (tq, LANES), 1)
    sel_bias = jnp.where(lane < AUX_SEL_LANES, (picked - 1.0) * SEL_BIAS, 0.0)
    qaux = qaux_ref[...]
    qx_sel = jnp.concatenate(
        [qs, jnp.concatenate([sel_bias + qaux[r:r + 1, :] for r in range(rep)], axis=0).astype(BF16)], axis=1)
    qx_win = jnp.concatenate(
        [qs, jnp.concatenate([jnp.broadcast_to(qaux[r:r + 1, :], (tq, LANES)) for r in range(rep)],
                             axis=0).astype(BF16)], axis=1)

    def sel_tile(j, carry, diagonal):
        m, l, acc = carry
        k0 = pl.multiple_of(j * tk, tk)
        s = lax.dot_general(qx_sel, ks_ref[pl.ds(k0, tk), :], NT_DIMS, preferred_element_type=F32)
        if diagonal:
            ahead = (lax.broadcasted_iota(I32, (rows, tk), 1)
                     - (lax.broadcasted_iota(I32, (rows, tk), 0) & (tq - 1)))
            s = jnp.where(ahead <= t0 - k0, s, -SEL_BIAS)
        m_new = jnp.maximum(m, jnp.max(s, axis=-1, keepdims=True))
        alpha = jnp.exp(m - m_new)
        p = jnp.exp(s - m_new)
        l = alpha * l + jnp.sum(p, axis=-1, keepdims=True)
        acc = alpha * acc + jnp.dot(p.astype(BF16), vs_ref[pl.ds(k0, tk), :], preferred_element_type=F32)
        return m_new, l, acc

    n_tiles = (t0 + tq + tk - 1) // tk
    init = (jnp.full((rows, 1), NEG_INF, F32), jnp.zeros((rows, 1), F32), jnp.zeros((rows, d), F32))
    carry = lax.fori_loop(0, n_tiles - 1, lambda j, c: sel_tile(j, c, False), init)
    _, l_s, acc_s = sel_tile(n_tiles - 1, carry, True)
    o_s = acc_s * (1.0 / l_s)

    wk = tq + WINDOW
    start = pl.multiple_of(jnp.maximum(t0 - WINDOW, 0), tq)
    s_w = lax.dot_general(qx_win, kw_ref[pl.ds(start, wk), :], NT_DIMS, preferred_element_type=F32)
    dist_w = t_col - (start + lax.broadcasted_iota(I32, (rows, wk), 1))
    p_w = _softmax_rows(s_w, (dist_w >= 0) & (dist_w < WINDOW)).astype(BF16)
    o_w = jnp.dot(p_w, vw_ref[pl.ds(start, wk), :], preferred_element_type=F32)

    gt = jax.nn.sigmoid(gate_ref[...])
    for r in range(rep):
        sl = slice(r * tq, (r + 1) * tq)
        o = (gt[:, 3 * r:3 * r + 1] * o_c[sl] + gt[:, 3 * r + 1:3 * r + 2] * o_s[sl]
             + gt[:, 3 * r + 2:3 * r + 3] * o_w[sl])
        o_ref[:, r * d:(r + 1) * d] = o.astype(o_ref.dtype)


def _nsa_attention(qa, q_col0, cmp, kvb, gates, slopes, seq):
    bsz = qa.shape[0]
    g, rep, d, tq = NSA_KV_HEADS, NSA_REP, HEAD_DIM, NSA_TQ
    ncp = seq // CMP_STRIDE
    n_sel = seq // SEL_BLOCK
    assert n_sel <= AUX_SEL_LANES and seq % NSA_TK == 0 and seq >= tq + WINDOW
    qblk0 = q_col0 // (rep * d)
    kv5 = kvb.reshape(bsz, seq, 4, g, d)
    aux = jnp.broadcast_to(_nsa_key_aux(seq)[None, :, None, :], (bsz, seq, g, LANES))
    ksx = jnp.concatenate([kv5[:, :, 0], aux], axis=-1).reshape(bsz, seq, g * 2 * d)
    kwx = jnp.concatenate([kv5[:, :, 2], aux], axis=-1).reshape(bsz, seq, g * 2 * d)
    kx_spec = pl.BlockSpec((None, seq, 2 * d), lambda b, h, i: (b, 0, h))
    v_spec = lambda s: pl.BlockSpec((None, seq, d), lambda b, h, i: (b, 0, s * NSA_KV_HEADS + h))
    cmp_spec = lambda s: pl.BlockSpec((None, None, None, ncp, d), lambda b, h, i: (s, b, h, 0, 0))
    return pl.pallas_call(
        _nsa_kernel,
        out_shape=jax.ShapeDtypeStruct((bsz, seq, NSA_DQ), BF16),
        grid=(bsz, g, seq // tq),
        in_specs=[pl.BlockSpec(memory_space=pltpu.SMEM),
                  pl.BlockSpec((None, rep, LANES), lambda b, h, i: (h, 0, 0)),
                  pl.BlockSpec((None, tq, rep * d), lambda b, h, i: (b, i, qblk0 + h)),
                  cmp_spec(0), cmp_spec(1), kx_spec, v_spec(1), kx_spec, v_spec(3),
                  pl.BlockSpec((None, None, tq, 3 * rep), lambda b, h, i: (b, h, i, 0))],
        out_specs=pl.BlockSpec((None, tq, rep * d), lambda b, h, i: (b, i, h)),
        compiler_params=_params(("parallel", "parallel", "arbitrary")),
        name="nsa_attention",
    )(slopes, _nsa_query_aux(slopes), qa, cmp, cmp, ksx, kvb, kwx, kvb, gates)


def _split_columns(proj):
    out, start = [], 0
    for size in IN_SIZES:
        out.append(proj[..., start:start + size])
        start += size
    return out


def kernel(x, c, ada_w, ada_b, norm1_w, w_in, gdn_conv_w, gdn_a_log, gdn_dt_bias, gdn_norm_w,
           cmp_pos_k, cmp_w1_k, cmp_w2_k, cmp_pos_v, cmp_w1_v, cmp_w2_v, w_out, norm2_w,
           w_up, w_down, final_norm_w):
    bsz, seq, d = x.shape
    depth = ada_w.shape[0]
    m = bsz * seq
    x2 = x.reshape(m, d)
    slopes = 2.0 ** (-8.0 * jnp.arange(1, NSA_HEADS + 1, dtype=F32) / NSA_HEADS)
    for i in range(depth):
        mod = _modulation(c, ada_w[i], ada_b[i])
        sh1, sc1, g1, sh2, sc2, g2 = jnp.split(mod, 6, axis=-1)
        h = _norm_mod(x2, norm1_w[i], sc1, sh1, seq, BF16)
        cols = _split_columns(w_in[i])
        small = jnp.concatenate([cols[4], cols[5], cols[13]], axis=-1)
        small = jnp.pad(small, ((0, 0), (0, LANES - small.shape[1])))
        w_a = jnp.concatenate(cols[0:4] + [cols[6]], axis=-1).astype(BF16)
        w_b = jnp.concatenate(cols[7:9] + [small], axis=-1).astype(BF16)
        w_c = jnp.concatenate(cols[9:13], axis=-1).astype(BF16)
        pa = _matmul(h, w_a, tm=1024, tn=1024, out_dtype=F32, name="in_proj_a").reshape(bsz, seq, -1)
        pb = _matmul(h, w_b, tm=1024, tn=w_b.shape[1], out_dtype=F32, name="in_proj_b").reshape(bsz, seq, -1)
        pc = _matmul(h, w_c, tm=1024, tn=1024, out_dtype=BF16, name="in_proj_c").reshape(bsz, seq, -1)

        o_a = _gdn(pa, pb, 2 * NSA_DKV // LANES, gdn_conv_w[i], gdn_a_log[i], gdn_dt_bias[i], gdn_norm_w[i], seq)
        ngate = pb[..., 2 * NSA_DKV + 2 * GDN_HEADS:2 * NSA_DKV + 2 * GDN_HEADS + 3 * NSA_HEADS]

        cmp = _compress(pb, jnp.stack([cmp_pos_k[i], cmp_pos_v[i]]), jnp.stack([cmp_w1_k[i], cmp_w1_v[i]]),
                        jnp.stack([cmp_w2_k[i], cmp_w2_v[i]]), seq)
        gates = ngate.reshape(bsz, seq, NSA_KV_HEADS, 3 * NSA_REP).transpose(0, 2, 1, 3)
        o_b = _nsa_attention(pa, 4 * GDN_DK, cmp, pc, gates, slopes, seq)

        x2 = _matmul_residual([o_a.reshape(m, -1), o_b.reshape(m, -1)], w_out[i].astype(BF16), x2, g1, seq,
                              tm=1024, tn=1024, tk=GDN_DK, name="out_proj")
        h2 = _norm_mod(x2, norm2_w[i], sc2, sh2, seq, BF16)
        hid = _matmul(h2, w_up[i].astype(BF16), tm=1024, tn=1024, out_dtype=BF16, relu2=True, name="mlp_up")
        x2 = _matmul_residual([hid], w_down[i].astype(BF16), x2, g2, seq,
                              tm=1024, tn=1024, tk=2048, name="mlp_down")
    return _final_norm(x2, final_norm_w).reshape(bsz, seq, d)
```

```python
import functools

import jax
import jax.numpy as jnp
from jax import lax
from jax.experimental import pallas as pl
from jax.experimental.pallas import tpu as pltpu

F32 = jnp.float32
BF16 = jnp.bfloat16
I32 = jnp.int32

HEAD_DIM = 128
GDN_HEADS = 16
NSA_HEADS = 16
NSA_KV_HEADS = 4
NSA_REP = NSA_HEADS // NSA_KV_HEADS
GDN_DK = GDN_HEADS * HEAD_DIM
NSA_DQ = NSA_HEADS * HEAD_DIM
NSA_DKV = NSA_KV_HEADS * HEAD_DIM
GDN_CONV = 4
GDN_CHUNK = 64
CMP_BLOCK = 32
CMP_STRIDE = 16
SEL_BLOCK = 64
SEL_TOPN = 16
SEL_LOCAL = 2
WINDOW = 512
EPS = 1e-6
IN_SIZES = (GDN_DK, GDN_DK, GDN_DK, GDN_DK, GDN_HEADS, GDN_HEADS,
            NSA_DQ, NSA_DKV, NSA_DKV, NSA_DKV, NSA_DKV, NSA_DKV, NSA_DKV, 3 * NSA_HEADS)

VMEM_LIMIT_V7X = 56 * 1024 * 1024
LANES = 128
NSA_TQ = 256
NSA_TK = 1024
NEG_INF = float("-inf")
AUX_SEL_LANES = 64
SEL_BIAS = 2.0 ** 100
NT_DIMS = (((1,), (1,)), ((), ()))
TN_DIMS = (((0,), (0,)), ((), ()))


def _params(sem):
    return pltpu.CompilerParams(dimension_semantics=sem, vmem_limit_bytes=VMEM_LIMIT_V7X)


def _mod_kernel(c_ref, w_ref, b_ref, o_ref):
    c = c_ref[...]
    s = (c * jax.nn.sigmoid(c)).astype(BF16)
    o_ref[...] = jnp.dot(s, w_ref[...].astype(BF16), preferred_element_type=F32) + b_ref[...]


def _modulation(c, w, b):
    bsz, d = c.shape
    n = w.shape[1]
    rows = 8
    cp = jnp.zeros((rows, d), F32).at[:bsz].set(c)
    tn = 512
    out = pl.pallas_call(
        _mod_kernel,
        out_shape=jax.ShapeDtypeStruct((rows, n), F32),
        grid=(n // tn,),
        in_specs=[pl.BlockSpec((rows, d), lambda j: (0, 0)),
                  pl.BlockSpec((d, tn), lambda j: (0, j)),
                  pl.BlockSpec((1, tn), lambda j: (0, j))],
        out_specs=pl.BlockSpec((rows, tn), lambda j: (0, j)),
        compiler_params=_params(("parallel",)),
        name="adaln_mod",
    )(cp, w, b.reshape(1, n))
    return out[:bsz]


def _norm_kernel(x_ref, g_ref, sc_ref, sh_ref, o_ref):
    x = x_ref[...]
    y = x * lax.rsqrt(jnp.mean(x * x, axis=-1, keepdims=True) + EPS) * g_ref[...]
    o_ref[...] = (y * (1.0 + sc_ref[0]) + sh_ref[0]).astype(o_ref.dtype)


def _final_norm_kernel(x_ref, g_ref, o_ref):
    x = x_ref[...]
    o_ref[...] = x * lax.rsqrt(jnp.mean(x * x, axis=-1, keepdims=True) + EPS) * g_ref[...]


def _norm_mod(x2, gain, scale, shift, seq, out_dtype):
    m, d = x2.shape
    tm = 256
    per_b = seq // tm
    bsz = m // seq
    return pl.pallas_call(
        _norm_kernel,
        out_shape=jax.ShapeDtypeStruct((m, d), out_dtype),
        grid=(m // tm,),
        in_specs=[pl.BlockSpec((tm, d), lambda i: (i, 0)),
                  pl.BlockSpec((1, d), lambda i: (0, 0)),
                  pl.BlockSpec((1, 1, d), lambda i: (i // per_b, 0, 0)),
                  pl.BlockSpec((1, 1, d), lambda i: (i // per_b, 0, 0))],
        out_specs=pl.BlockSpec((tm, d), lambda i: (i, 0)),
        compiler_params=_params(("parallel",)),
        name="rmsnorm_mod",
    )(x2, gain.reshape(1, d), scale.reshape(bsz, 1, d), shift.reshape(bsz, 1, d))


def _final_norm(x2, gain):
    m, d = x2.shape
    tm = 256
    return pl.pallas_call(
        _final_norm_kernel,
        out_shape=jax.ShapeDtypeStruct((m, d), F32),
        grid=(m // tm,),
        in_specs=[pl.BlockSpec((tm, d), lambda i: (i, 0)),
                  pl.BlockSpec((1, d), lambda i: (0, 0))],
        out_specs=pl.BlockSpec((tm, d), lambda i: (i, 0)),
        compiler_params=_params(("parallel",)),
        name="final_rmsnorm",
    )(x2, gain.reshape(1, d))


def _mm_kernel(a_ref, w_ref, o_ref, *, relu2):
    acc = jnp.dot(a_ref[...], w_ref[...], preferred_element_type=F32)
    if relu2:
        acc = jnp.square(jnp.maximum(acc, 0.0))
    o_ref[...] = acc.astype(o_ref.dtype)


def _matmul(a, w, *, tm, tn, out_dtype, relu2=False, name):
    m, k = a.shape
    n = w.shape[1]
    return pl.pallas_call(
        functools.partial(_mm_kernel, relu2=relu2),
        out_shape=jax.ShapeDtypeStruct((m, n), out_dtype),
        grid=(m // tm, n // tn),
        in_specs=[pl.BlockSpec((tm, k), lambda i, j: (i, 0)),
                  pl.BlockSpec((k, tn), lambda i, j: (0, j))],
        out_specs=pl.BlockSpec((tm, tn), lambda i, j: (i, j)),
        compiler_params=_params(("parallel", "parallel")),
        name=name,
    )(a, w)


def _mm_res_kernel(*refs, nparts):
    a_refs = refs[:nparts]
    w_ref, x_ref, g_ref, o_ref, acc_ref = refs[nparts:]
    k = pl.program_id(2)

    @pl.when(k == 0)
    def _():
        acc_ref[...] = jnp.zeros_like(acc_ref)

    if nparts == 1:
        acc_ref[...] += jnp.dot(a_refs[0][...], w_ref[...], preferred_element_type=F32)
    else:
        for idx, a_ref in enumerate(a_refs):
            @pl.when(k == idx)
            def _(a_ref=a_ref):
                acc_ref[...] += jnp.dot(a_ref[...], w_ref[...], preferred_element_type=F32)

    @pl.when(k == pl.num_programs(2) - 1)
    def _():
        o_ref[...] = x_ref[...] + g_ref[0] * acc_ref[...]


def _matmul_residual(a_parts, w, x2, gate, seq, *, tm, tn, tk, name):
    nparts = len(a_parts)
    m = a_parts[0].shape[0]
    kdim = sum(a.shape[1] for a in a_parts)
    n = w.shape[1]
    bsz = m // seq
    per_b = seq // tm
    if nparts == 1:
        a_specs = [pl.BlockSpec((tm, tk), lambda i, j, k: (i, k))]
    else:
        assert all(a.shape[1] == tk for a in a_parts)
        a_specs = [pl.BlockSpec((tm, tk), lambda i, j, k: (i, 0)) for _ in a_parts]
    return pl.pallas_call(
        functools.partial(_mm_res_kernel, nparts=nparts),
        out_shape=jax.ShapeDtypeStruct((m, n), F32),
        grid=(m // tm, n // tn, kdim // tk),
        in_specs=a_specs + [pl.BlockSpec((tk, tn), lambda i, j, k: (k, j)),
                            pl.BlockSpec((tm, tn), lambda i, j, k: (i, j)),
                            pl.BlockSpec((1, 1, tn), lambda i, j, k: (i // per_b, 0, j))],
        out_specs=pl.BlockSpec((tm, tn), lambda i, j, k: (i, j)),
        scratch_shapes=[pltpu.VMEM((tm, tn), F32)],
        compiler_params=_params(("parallel", "parallel", "arbitrary")),
        name=name,
    )(*a_parts, w, x2, gate.reshape(bsz, 1, n))


def _gdn_kernel(q_ref, k_ref, v_ref, z_ref, ba_ref, cw_ref, alog_ref, dtb_ref, nw_ref, o_ref,
                xbuf_ref, s_ref):
    c, d, nh = GDN_CHUNK, HEAD_DIM, GDN_HEADS
    width = nh * d
    hist = 8

    @pl.when(pl.program_id(1) == 0)
    def _():
        xbuf_ref[:, 0:hist, :] = jnp.zeros((3, hist, width), F32)
        s_ref[...] = jnp.zeros_like(s_ref)

    conv = []
    for i, src in enumerate((q_ref, k_ref, v_ref)):
        xbuf_ref[i, hist:hist + c, :] = src[...]
        acc = cw_ref[GDN_CONV - 1:GDN_CONV, i * width:(i + 1) * width] * src[...]
        for j in range(GDN_CONV - 1):
            shift = GDN_CONV - 1 - j
            acc += cw_ref[j:j + 1, i * width:(i + 1) * width] * xbuf_ref[i, pl.ds(hist - shift, c), :]
        conv.append(acc * jax.nn.sigmoid(acc))
        xbuf_ref[i, 0:hist, :] = xbuf_ref[i, c:c + hist, :]
    cq, ck, cv = conv

    ba = ba_ref[...]
    beta = jax.nn.sigmoid(ba[:, 0:nh])
    g = -jnp.exp(alog_ref[...]) * jax.nn.softplus(ba[:, nh:2 * nh] + dtb_ref[...])
    r_i = lax.broadcasted_iota(I32, (c, c), 0)
    c_i = lax.broadcasted_iota(I32, (c, c), 1)
    causal = r_i >= c_i
    strict = r_i > c_i
    gam = jnp.dot(causal.astype(F32), g, precision=lax.Precision.HIGHEST, preferred_element_type=F32)
    gam_t = lax.dot_general(g, (r_i <= c_i).astype(F32), TN_DIMS, precision=lax.Precision.HIGHEST,
                            preferred_element_type=F32)
    g_last = gam[c - 1:c, :]
    egam = jnp.exp(gam)
    bexp = beta * egam
    kscale = jnp.exp(g_last - gam)
    eg_last = jnp.exp(g_last)
    nw = nw_ref[...]

    heads = range(nh)
    hsl = [slice(h * d, (h + 1) * d) for h in heads]
    hcol = [slice(h, h + 1) for h in heads]
    dot16 = lambda a, b: jnp.dot(a.astype(BF16), b.astype(BF16), preferred_element_type=F32)

    def dot32(a, b):
        ah = a.astype(BF16).astype(F32)
        bh = b.astype(BF16).astype(F32)
        lhs = jnp.concatenate([ah, ah, a - ah], axis=1).astype(BF16)
        rhs = jnp.concatenate([bh, b - bh, bh], axis=0).astype(BF16)
        return jnp.dot(lhs, rhs, preferred_element_type=F32)

    qn, kn, qk, lower, decay = [], [], [], [], []
    for h in heads:
        qh = cq[:, hsl[h]]
        kh = ck[:, hsl[h]]
        qh = qh * lax.rsqrt(jnp.sum(qh * qh, axis=-1, keepdims=True) + EPS) * (d ** -0.5)
        kh = kh * lax.rsqrt(jnp.sum(kh * kh, axis=-1, keepdims=True) + EPS)
        kb = kh.astype(BF16)
        both = lax.dot_general(jnp.concatenate([qh.astype(BF16), kb], axis=0), kb, NT_DIMS,
                               preferred_element_type=F32)
        diff = gam[:, hcol[h]] - gam_t[h:h + 1, :]
        dec = jnp.where(causal, jnp.exp(jnp.where(causal, diff, 0.0)), 0.0)
        qn.append(qh)
        kn.append(kh)
        decay.append(dec)
        qk.append(both[:c] * dec)
        lower.append(jnp.where(strict, beta[:, hcol[h]] * both[c:] * dec, 0.0))

    base = 8
    same = lambda s: (r_i // s) == (c_i // s)
    diag = [jnp.where(same(base), lo, 0.0) for lo in lower]
    p2 = [dot32(dg, dg) for dg in diag]
    qinv = [p - dg - dot32(dg, p) for dg, p in zip(diag, p2)]
    p4 = [dot32(p, p) for p in p2]
    qinv = [q + p + dot32(q, p) for q, p in zip(qinv, p4)]
    eye = (r_i == c_i).astype(F32)
    inv = [eye + q for q in qinv]
    size = base
    while size < c:
        off_mask = same(2 * size) & jnp.logical_not(same(size))
        off = [jnp.where(off_mask, lo, 0.0) for lo in lower]
        t1 = [dot32(o_, x_) for o_, x_ in zip(off, inv)]
        inv = [x_ - dot32(x_, t_) for x_, t_ in zip(inv, t1)]
        size *= 2

    sol = []
    for h in heads:
        rhs = jnp.concatenate([cv[:, hsl[h]] * beta[:, hcol[h]], kn[h] * bexp[:, hcol[h]]], axis=1)
        sol.append(rhs + dot16(inv[h] - eye, rhs))
    wq = [dot16(jnp.concatenate([sol[h][:, d:], qn[h] * egam[:, hcol[h]]], axis=0), s_ref[h])
          for h in heads]
    vnew = [(sol[h][:, :d] - wq[h][:c]).astype(BF16) for h in heads]
    for h in heads:
        k_dec = kn[h] * kscale[:, hcol[h]]
        s_ref[h] = s_ref[h] * eg_last[:, hcol[h]] + lax.dot_general(
            k_dec.astype(BF16), vnew[h], TN_DIMS, preferred_element_type=F32)
    for h in heads:
        o = wq[h][c:] + jnp.dot(qk[h].astype(BF16), vnew[h], preferred_element_type=F32)
        y = o * lax.rsqrt(jnp.mean(o * o, axis=-1, keepdims=True) + EPS) * nw
        zh = z_ref[:, hsl[h]]
        o_ref[:, hsl[h]] = (y * (zh * jax.nn.sigmoid(zh))).astype(o_ref.dtype)


def _gdn(pa, pb, ba_blk, conv_w, a_log, dt_bias, norm_w, seq):
    bsz = pa.shape[0]
    c, d, nh = GDN_CHUNK, HEAD_DIM, GDN_HEADS
    width = nh * d
    col_spec = lambda j: pl.BlockSpec((None, c, width), lambda b, n: (b, n, j))
    full = lambda shape: pl.BlockSpec(shape, lambda b, n: (0,) * len(shape))
    return pl.pallas_call(
        _gdn_kernel,
        out_shape=jax.ShapeDtypeStruct((bsz, seq, width), BF16),
        grid=(bsz, seq // c),
        in_specs=[col_spec(0), col_spec(1), col_spec(2), col_spec(3),
                  pl.BlockSpec((None, c, LANES), lambda b, n: (b, n, ba_blk)),
                  full((GDN_CONV, 3 * width)), full((1, nh)), full((1, nh)), full((1, d))],
        out_specs=pl.BlockSpec((None, c, width), lambda b, n: (b, n, 0)),
        scratch_shapes=[pltpu.VMEM((3, 8 + c, width), F32), pltpu.VMEM((nh, d, d), F32)],
        compiler_params=_params(("parallel", "arbitrary")),
        name="gdn_chunked",
    )(pa, pa, pa, pa, pb, conv_w, a_log.reshape(1, nh), dt_bias.reshape(1, nh), norm_w.reshape(1, d))


def _compress_kernel(x_ref, pos_ref, w1_ref, w2_ref, o_ref):
    n = o_ref.shape[0]
    d = x_ref.shape[1]
    acc_a = jnp.zeros((n, d), F32)
    acc_b = jnp.zeros((n, d), F32)
    for j in range(CMP_STRIDE):
        xj = x_ref[pl.ds(j, n, stride=CMP_STRIDE), :]
        lo = (xj + pos_ref[j:j + 1, :]).astype(BF16)
        hi = (xj + pos_ref[CMP_STRIDE + j:CMP_STRIDE + j + 1, :]).astype(BF16)
        acc_a += jnp.dot(lo, w1_ref[j * d:(j + 1) * d, :].astype(BF16), preferred_element_type=F32)
        acc_b += jnp.dot(hi, w1_ref[(CMP_STRIDE + j) * d:(CMP_STRIDE + j + 1) * d, :].astype(BF16),
                         preferred_element_type=F32)
    pre = acc_a + pltpu.roll(acc_b, n - 1, 0)
    hid = (pre * jax.nn.sigmoid(pre)).astype(BF16)
    out = jnp.dot(hid, w2_ref[...].astype(BF16), preferred_element_type=F32)
    row = lax.broadcasted_iota(I32, (n, d), 0)
    o_ref[...] = jnp.where(row < n - 1, out, 0.0).astype(o_ref.dtype)


def _compress(kvc, pos, w1, w2, seq):
    bsz = kvc.shape[0]
    g = NSA_KV_HEADS
    n = seq // CMP_STRIDE
    d = HEAD_DIM
    return pl.pallas_call(
        _compress_kernel,
        out_shape=jax.ShapeDtypeStruct((2, bsz, g, n, d), BF16),
        grid=(2, bsz, g),
        in_specs=[pl.BlockSpec((None, seq, d), lambda s, b, h: (b, 0, s * NSA_KV_HEADS + h)),
                  pl.BlockSpec((None, CMP_BLOCK, d), lambda s, b, h: (s, 0, 0)),
                  pl.BlockSpec((None, CMP_BLOCK * d, d), lambda s, b, h: (s, 0, 0)),
                  pl.BlockSpec((None, d, d), lambda s, b, h: (s, 0, 0))],
        out_specs=pl.BlockSpec((None, None, None, n, d), lambda s, b, h: (s, b, h, 0, 0)),
        compiler_params=_params(("parallel", "parallel", "parallel")),
        name="nsa_compress",
    )(kvc, pos, w1, w2)


def _softmax_rows(s, mask):
    s = jnp.where(mask, s, NEG_INF)
    m = jnp.max(s, axis=-1, keepdims=True)
    m = jnp.where(jnp.abs(m) < jnp.inf, m, 0.0)
    e = jnp.exp(s - m)
    d = jnp.sum(e, axis=-1, keepdims=True)
    return e * (1.0 / jnp.where(d > 0, d, 1.0))


def _nsa_key_aux(seq):
    key = jnp.arange(seq, dtype=I32)[:, None]
    lane = jnp.arange(LANES, dtype=I32)[None, :]
    onehot = (lane == key // SEL_BLOCK) & (lane < AUX_SEL_LANES)
    hi = (lane >= AUX_SEL_LANES) & (lane < AUX_SEL_LANES + 3)
    lo = (lane >= AUX_SEL_LANES + 3) & (lane < AUX_SEL_LANES + 6)
    aux = jnp.where(onehot, 1, 0) + jnp.where(hi, (key // 64) * 64, 0) + jnp.where(lo, key % 64, 0)
    return aux.astype(BF16)


def _nsa_query_aux(slopes):
    s1 = slopes.astype(BF16).astype(F32)
    s2 = (slopes - s1).astype(BF16).astype(F32)
    s3 = (slopes - s1 - s2).astype(BF16).astype(F32)
    pieces = jnp.stack([s1, s2, s3, s1, s2, s3], axis=-1)
    aux = jnp.zeros((slopes.shape[0], LANES), F32).at[:, AUX_SEL_LANES:AUX_SEL_LANES + 6].set(pieces)
    return aux.reshape(NSA_KV_HEADS, NSA_REP, LANES)


def _nsa_kernel(slopes_ref, qaux_ref, kaux_ref, q_ref, kc_ref, vc_ref, ksel_ref, vs_ref, kwin_ref, vw_ref,
                gate_ref, o_ref, ks_ref, kw_ref):
    seq = ks_ref.shape[0]
    tq, tk, rep, d = NSA_TQ, NSA_TK, NSA_REP, HEAD_DIM
    rows = rep * tq
    g = pl.program_id(1)
    t0 = pl.program_id(2) * tq

    @pl.when(pl.program_id(2) == 0)
    def _():
        ks_ref[:, 0:d] = ksel_ref[...]
        ks_ref[:, d:2 * d] = kaux_ref[...]
        kw_ref[:, 0:d] = kwin_ref[...]
        kw_ref[:, d:2 * d] = kaux_ref[...]

    q = q_ref[...] * (d ** -0.5)
    qs = jnp.concatenate([q[:, r * d:(r + 1) * d] for r in range(rep)], axis=0).astype(BF16)
    slope = jnp.concatenate([jnp.full((tq, 1), slopes_ref[g * rep + r], F32) for r in range(rep)], axis=0)
    row = lax.broadcasted_iota(I32, (rows, 1), 0)
    t_col = t0 + (row & (tq - 1))

    ncp = kc_ref.shape[0]
    n_idx = lax.broadcasted_iota(I32, (rows, ncp), 1)
    dist_c = t_col - (n_idx * CMP_STRIDE + (CMP_BLOCK - 1))
    s_c = lax.dot_general(qs, kc_ref[...], NT_DIMS, preferred_element_type=F32)
    s_c = s_c - slope * dist_c.astype(F32)
    p_c = _softmax_rows(s_c, dist_c >= 0).astype(BF16)
    o_c = jnp.dot(p_c, vc_ref[...], preferred_element_type=F32)

    n_sel = seq // SEL_BLOCK
    ov_s = lax.broadcasted_iota(I32, (n_sel, ncp), 0) * SEL_BLOCK
    ov_n = lax.broadcasted_iota(I32, (n_sel, ncp), 1) * CMP_STRIDE
    ov_t = ((ov_n <= ov_s + (SEL_BLOCK - 1)) & (ov_n + (CMP_BLOCK - 1) >= ov_s)).astype(BF16)
    imp = jnp.zeros((n_sel, tq), F32)
    for r in range(rep):
        imp += lax.dot_general(ov_t, p_c[r * tq:(r + 1) * tq, :], NT_DIMS, preferred_element_type=F32)
    blk = lax.broadcasted_iota(I32, (n_sel, tq), 0)
    cur = (t0 + lax.broadcasted_iota(I32, (n_sel, tq), 1)) // SEL_BLOCK
    forced = (blk == 0) | ((cur - blk) < SEL_LOCAL)
    imp = jnp.where(forced, jnp.inf, imp)
    imp = jnp.where(blk <= cur, imp, NEG_INF)
    rank = jnp.zeros((n_sel, tq), I32)
    for i in range(n_sel):
        ri = imp[i:i + 1, :]
        rank += ((ri > imp) | ((ri == imp) & (blk > i))).astype(I32)
    sel_t = (rank < min(SEL_TOPN, n_sel)).astype(F32)

    to_lanes = (lax.broadcasted_iota(I32, (n_sel, LANES), 0)
                == lax.broadcasted_iota(I32, (n_sel, LANES), 1)).astype(BF16)
    picked = lax.dot_general(sel_t.astype(BF16), to_lanes, TN_DIMS, preferred_element_type=F32)
    lane = lax.broadcasted_iota(I32, (tq, LANES), 1)
    sel_bias = jnp.where(lane < AUX_SEL_LANES, (picked - 1.0) * SEL_BIAS, 0.0)
    qaux = qaux_ref[...]
    qx_sel = jnp.concatenate(
        [qs, jnp.concatenate([sel_bias + qaux[r:r + 1, :] for r in range(rep)], axis=0).astype(BF16)], axis=1)
    qx_win = jnp.concatenate(
        [qs, jnp.concatenate([jnp.broadcast_to(qaux[r:r + 1, :], (tq, LANES)) for r in range(rep)],
                             axis=0).astype(BF16)], axis=1)

    def sel_tile(j, carry, diagonal):
        m, l, acc = carry
        k0 = pl.multiple_of(j * tk, tk)
        s = lax.dot_general(qx_sel, ks_ref[pl.ds(k0, tk), :], NT_DIMS, preferred_element_type=F32)
        if diagonal:
            ahead = (lax.broadcasted_iota(I32, (rows, tk), 1)
                     - (lax.broadcasted_iota(I32, (rows, tk), 0) & (tq - 1)))
            s = jnp.where(ahead <= t0 - k0, s, -SEL_BIAS)
        m_new = jnp.maximum(m, jnp.max(s, axis=-1, keepdims=True))
        alpha = jnp.exp(m - m_new)
        p = jnp.exp(s - m_new)
        l = alpha * l + jnp.sum(p, axis=-1, keepdims=True)
        acc = alpha * acc + jnp.dot(p.astype(BF16), vs_ref[pl.ds(k0, tk), :], preferred_element_type=F32)
        return m_new, l, acc

    n_tiles = (t0 + tq + tk - 1) // tk
    init = (jnp.full((rows, 1), NEG_INF, F32), jnp.zeros((rows, 1), F32), jnp.zeros((rows, d), F32))
    carry = lax.fori_loop(0, n_tiles - 1, lambda j, c: sel_tile(j, c, False), init)
    _, l_s, acc_s = sel_tile(n_tiles - 1, carry, True)
    o_s = acc_s * (1.0 / l_s)

    wk = tq + WINDOW
    start = pl.multiple_of(jnp.maximum(t0 - WINDOW, 0), tq)
    s_w = lax.dot_general(qx_win, kw_ref[pl.ds(start, wk), :], NT_DIMS, preferred_element_type=F32)
    dist_w = t_col - (start + lax.broadcasted_iota(I32, (rows, wk), 1))
    p_w = _softmax_rows(s_w, (dist_w >= 0) & (dist_w < WINDOW)).astype(BF16)
    o_w = jnp.dot(p_w, vw_ref[pl.ds(start, wk), :], preferred_element_type=F32)

    gt = jax.nn.sigmoid(gate_ref[...])
    for r in range(rep):
        sl = slice(r * tq, (r + 1) * tq)
        o = (gt[:, 3 * r:3 * r + 1] * o_c[sl] + gt[:, 3 * r + 1:3 * r + 2] * o_s[sl]
             + gt[:, 3 * r + 2:3 * r + 3] * o_w[sl])
        o_ref[:, r * d:(r + 1) * d] = o.astype(o_ref.dtype)


def _nsa_attention(qa, q_col0, cmp, kvb, gates, slopes, seq):
    bsz = qa.shape[0]
    g, rep, d, tq = NSA_KV_HEADS, NSA_REP, HEAD_DIM, NSA_TQ
    ncp = seq // CMP_STRIDE
    n_sel = seq // SEL_BLOCK
    assert n_sel <= AUX_SEL_LANES and seq % NSA_TK == 0 and seq >= tq + WINDOW
    qblk0 = q_col0 // (rep * d)
    kv_spec = lambda s: pl.BlockSpec((None, seq, d), lambda b, h, i: (b, 0, s * NSA_KV_HEADS + h))
    cmp_spec = lambda s: pl.BlockSpec((None, None, None, ncp, d), lambda b, h, i: (s, b, h, 0, 0))
    return pl.pallas_call(
        _nsa_kernel,
        out_shape=jax.ShapeDtypeStruct((bsz, seq, NSA_DQ), BF16),
        grid=(bsz, g, seq // tq),
        in_specs=[pl.BlockSpec(memory_space=pltpu.SMEM),
                  pl.BlockSpec((None, rep, LANES), lambda b, h, i: (h, 0, 0)),
                  pl.BlockSpec((seq, LANES), lambda b, h, i: (0, 0)),
                  pl.BlockSpec((None, tq, rep * d), lambda b, h, i: (b, i, qblk0 + h)),
                  cmp_spec(0), cmp_spec(1), kv_spec(0), kv_spec(1), kv_spec(2), kv_spec(3),
                  pl.BlockSpec((None, None, tq, 3 * rep), lambda b, h, i: (b, h, i, 0))],
        out_specs=pl.BlockSpec((None, tq, rep * d), lambda b, h, i: (b, i, h)),
        scratch_shapes=[pltpu.VMEM((seq, 2 * d), BF16), pltpu.VMEM((seq, 2 * d), BF16)],
        compiler_params=_params(("parallel", "parallel", "arbitrary")),
        name="nsa_attention",
    )(slopes, _nsa_query_aux(slopes), _nsa_key_aux(seq), qa, cmp, cmp, kvb, kvb, kvb, kvb, gates)


def _split_columns(proj):
    out, start = [], 0
    for size in IN_SIZES:
        out.append(proj[..., start:start + size])
        start += size
    return out


def kernel(x, c, ada_w, ada_b, norm1_w, w_in, gdn_conv_w, gdn_a_log, gdn_dt_bias, gdn_norm_w,
           cmp_pos_k, cmp_w1_k, cmp_w2_k, cmp_pos_v, cmp_w1_v, cmp_w2_v, w_out, norm2_w,
           w_up, w_down, final_norm_w):
    bsz, seq, d = x.shape
    depth = ada_w.shape[0]
    m = bsz * seq
    x2 = x.reshape(m, d)
    slopes = 2.0 ** (-8.0 * jnp.arange(1, NSA_HEADS + 1, dtype=F32) / NSA_HEADS)
    for i in range(depth):
        mod = _modulation(c, ada_w[i], ada_b[i])
        sh1, sc1, g1, sh2, sc2, g2 = jnp.split(mod, 6, axis=-1)
        h = _norm_mod(x2, norm1_w[i], sc1, sh1, seq, BF16)
        cols = _split_columns(w_in[i])
        small = jnp.concatenate([cols[4], cols[5], cols[13]], axis=-1)
        small = jnp.pad(small, ((0, 0), (0, LANES - small.shape[1])))
        w_a = jnp.concatenate(cols[0:4] + [cols[6]], axis=-1).astype(BF16)
        w_b = jnp.concatenate(cols[7:9] + [small], axis=-1).astype(BF16)
        w_c = jnp.concatenate(cols[9:13], axis=-1).astype(BF16)
        pa = _matmul(h, w_a, tm=1024, tn=1024, out_dtype=F32, name="in_proj_a").reshape(bsz, seq, -1)
        pb = _matmul(h, w_b, tm=1024, tn=w_b.shape[1], out_dtype=F32, name="in_proj_b").reshape(bsz, seq, -1)
        pc = _matmul(h, w_c, tm=1024, tn=1024, out_dtype=BF16, name="in_proj_c").reshape(bsz, seq, -1)

        o_a = _gdn(pa, pb, 2 * NSA_DKV // LANES, gdn_conv_w[i], gdn_a_log[i], gdn_dt_bias[i], gdn_norm_w[i], seq)
        ngate = pb[..., 2 * NSA_DKV + 2 * GDN_HEADS:2 * NSA_DKV + 2 * GDN_HEADS + 3 * NSA_HEADS]

        cmp = _compress(pb, jnp.stack([cmp_pos_k[i], cmp_pos_v[i]]), jnp.stack([cmp_w1_k[i], cmp_w1_v[i]]),
                        jnp.stack([cmp_w2_k[i], cmp_w2_v[i]]), seq)
        gates = ngate.reshape(bsz, seq, NSA_KV_HEADS, 3 * NSA_REP).transpose(0, 2, 1, 3)
        o_b = _nsa_attention(pa, 4 * GDN_DK, cmp, pc, gates, slopes, seq)

        x2 = _matmul_residual([o_a.reshape(m, -1), o_b.reshape(m, -1)], w_out[i].astype(BF16), x2, g1, seq,
                              tm=1024, tn=1024, tk=GDN_DK, name="out_proj")
        h2 = _norm_mod(x2, norm2_w[i], sc2, sh2, seq, BF16)
        hid = _matmul(h2, w_up[i].astype(BF16), tm=1024, tn=1024, out_dtype=BF16, relu2=True, name="mlp_up")
        x2 = _matmul_residual([hid], w_down[i].astype(BF16), x2, g2, seq,
                              tm=1024, tn=1024, tk=2048, name="mlp_down")
    return _final_norm(x2, final_norm_w).reshape(bsz, seq, d)
```

```python
import functools

import jax
import jax.numpy as jnp
from jax import lax
from jax.experimental import pallas as pl
from jax.experimental.pallas import tpu as pltpu

F32 = jnp.float32
BF16 = jnp.bfloat16
I32 = jnp.int32

HEAD_DIM = 128
GDN_HEADS = 16
NSA_HEADS = 16
NSA_KV_HEADS = 4
NSA_REP = NSA_HEADS // NSA_KV_HEADS
GDN_DK = GDN_HEADS * HEAD_DIM
NSA_DQ = NSA_HEADS * HEAD_DIM
NSA_DKV = NSA_KV_HEADS * HEAD_DIM
GDN_CONV = 4
GDN_CHUNK = 64
GDN_HEAD_GROUP = 16
CMP_BLOCK = 32
CMP_STRIDE = 16
SEL_BLOCK = 64
SEL_TOPN = 16
SEL_LOCAL = 2
WINDOW = 512
EPS = 1e-6
IN_SIZES = (GDN_DK, GDN_DK, GDN_DK, GDN_DK, GDN_HEADS, GDN_HEADS,
            NSA_DQ, NSA_DKV, NSA_DKV, NSA_DKV, NSA_DKV, NSA_DKV, NSA_DKV, 3 * NSA_HEADS)

VMEM_LIMIT_V7X = 56 * 1024 * 1024
LANES = 128
NSA_TQ = 256
NSA_TK = 1024
NEG_INF = float("-inf")
AUX_SEL_LANES = 64
SEL_BIAS = 2.0 ** 100
NT_DIMS = (((1,), (1,)), ((), ()))
TN_DIMS = (((0,), (0,)), ((), ()))


def _params(sem):
    return pltpu.CompilerParams(dimension_semantics=sem, vmem_limit_bytes=VMEM_LIMIT_V7X)


def _mod_kernel(c_ref, w_ref, b_ref, o_ref):
    c = c_ref[...]
    s = (c * jax.nn.sigmoid(c)).astype(BF16)
    o_ref[...] = jnp.dot(s, w_ref[...].astype(BF16), preferred_element_type=F32) + b_ref[...]


def _modulation(c, w, b):
    bsz, d = c.shape
    n = w.shape[1]
    rows = 8
    cp = jnp.zeros((rows, d), F32).at[:bsz].set(c)
    tn = 512
    out = pl.pallas_call(
        _mod_kernel,
        out_shape=jax.ShapeDtypeStruct((rows, n), F32),
        grid=(n // tn,),
        in_specs=[pl.BlockSpec((rows, d), lambda j: (0, 0)),
                  pl.BlockSpec((d, tn), lambda j: (0, j)),
                  pl.BlockSpec((1, tn), lambda j: (0, j))],
        out_specs=pl.BlockSpec((rows, tn), lambda j: (0, j)),
        compiler_params=_params(("parallel",)),
        name="adaln_mod",
    )(cp, w, b.reshape(1, n))
    return out[:bsz]


def _norm_kernel(x_ref, g_ref, sc_ref, sh_ref, o_ref):
    x = x_ref[...]
    y = x * lax.rsqrt(jnp.mean(x * x, axis=-1, keepdims=True) + EPS) * g_ref[...]
    o_ref[...] = (y * (1.0 + sc_ref[0]) + sh_ref[0]).astype(o_ref.dtype)


def _final_norm_kernel(x_ref, g_ref, o_ref):
    x = x_ref[...]
    o_ref[...] = x * lax.rsqrt(jnp.mean(x * x, axis=-1, keepdims=True) + EPS) * g_ref[...]


def _norm_mod(x2, gain, scale, shift, seq, out_dtype):
    m, d = x2.shape
    tm = 256
    per_b = seq // tm
    bsz = m // seq
    return pl.pallas_call(
        _norm_kernel,
        out_shape=jax.ShapeDtypeStruct((m, d), out_dtype),
        grid=(m // tm,),
        in_specs=[pl.BlockSpec((tm, d), lambda i: (i, 0)),
                  pl.BlockSpec((1, d), lambda i: (0, 0)),
                  pl.BlockSpec((1, 1, d), lambda i: (i // per_b, 0, 0)),
                  pl.BlockSpec((1, 1, d), lambda i: (i // per_b, 0, 0))],
        out_specs=pl.BlockSpec((tm, d), lambda i: (i, 0)),
        compiler_params=_params(("parallel",)),
        name="rmsnorm_mod",
    )(x2, gain.reshape(1, d), scale.reshape(bsz, 1, d), shift.reshape(bsz, 1, d))


def _final_norm(x2, gain):
    m, d = x2.shape
    tm = 256
    return pl.pallas_call(
        _final_norm_kernel,
        out_shape=jax.ShapeDtypeStruct((m, d), F32),
        grid=(m // tm,),
        in_specs=[pl.BlockSpec((tm, d), lambda i: (i, 0)),
                  pl.BlockSpec((1, d), lambda i: (0, 0))],
        out_specs=pl.BlockSpec((tm, d), lambda i: (i, 0)),
        compiler_params=_params(("parallel",)),
        name="final_rmsnorm",
    )(x2, gain.reshape(1, d))


def _mm_kernel(a_ref, w_ref, o_ref, *, relu2):
    acc = jnp.dot(a_ref[...], w_ref[...], preferred_element_type=F32)
    if relu2:
        acc = jnp.square(jnp.maximum(acc, 0.0))
    o_ref[...] = acc.astype(o_ref.dtype)


def _matmul(a, w, *, tm, tn, out_dtype, relu2=False, name):
    m, k = a.shape
    n = w.shape[1]
    return pl.pallas_call(
        functools.partial(_mm_kernel, relu2=relu2),
        out_shape=jax.ShapeDtypeStruct((m, n), out_dtype),
        grid=(m // tm, n // tn),
        in_specs=[pl.BlockSpec((tm, k), lambda i, j: (i, 0)),
                  pl.BlockSpec((k, tn), lambda i, j: (0, j))],
        out_specs=pl.BlockSpec((tm, tn), lambda i, j: (i, j)),
        compiler_params=_params(("parallel", "parallel")),
        name=name,
    )(a, w)


def _mm_res_kernel(a_ref, w_ref, x_ref, g_ref, o_ref):
    k = pl.program_id(2)

    @pl.when(k == 0)
    def _():
        o_ref[...] = jnp.zeros_like(o_ref)

    o_ref[...] += jnp.dot(a_ref[...], w_ref[...], preferred_element_type=F32)

    @pl.when(k == pl.num_programs(2) - 1)
    def _():
        o_ref[...] = x_ref[...] + g_ref[0] * o_ref[...]


def _matmul_residual(a, w, x2, gate, seq, *, tm, tn, tk, name):
    m, kdim = a.shape
    n = w.shape[1]
    bsz = m // seq
    per_b = seq // tm
    return pl.pallas_call(
        _mm_res_kernel,
        out_shape=jax.ShapeDtypeStruct((m, n), F32),
        grid=(m // tm, n // tn, kdim // tk),
        in_specs=[pl.BlockSpec((tm, tk), lambda i, j, k: (i, k)),
                  pl.BlockSpec((tk, tn), lambda i, j, k: (k, j)),
                  pl.BlockSpec((tm, tn), lambda i, j, k: (i, j)),
                  pl.BlockSpec((1, 1, tn), lambda i, j, k: (i // per_b, 0, j))],
        out_specs=pl.BlockSpec((tm, tn), lambda i, j, k: (i, j)),
        compiler_params=_params(("parallel", "parallel", "arbitrary")),
        name=name,
    )(a, w, x2, gate.reshape(bsz, 1, n))


def _mm_parts_res_kernel(*refs, nparts):
    a_refs = refs[:nparts]
    w_ref, x_ref, g_ref, o_ref = refs[nparts:]
    width = a_refs[0].shape[1]
    acc = jnp.dot(a_refs[0][...], w_ref[0:width, :], preferred_element_type=F32)
    for idx in range(1, nparts):
        acc += jnp.dot(a_refs[idx][...], w_ref[idx * width:(idx + 1) * width, :], preferred_element_type=F32)
    o_ref[...] = x_ref[...] + g_ref[0] * acc


def _matmul_parts_residual(a_parts, w, x2, gate, seq, *, tm, tn, name):
    nparts = len(a_parts)
    m, width = a_parts[0].shape
    assert all(a.shape == (m, width) for a in a_parts) and w.shape[0] == nparts * width
    n = w.shape[1]
    bsz = m // seq
    per_b = seq // tm
    return pl.pallas_call(
        functools.partial(_mm_parts_res_kernel, nparts=nparts),
        out_shape=jax.ShapeDtypeStruct((m, n), F32),
        grid=(m // tm, n // tn),
        in_specs=[pl.BlockSpec((tm, width), lambda i, j: (i, 0)) for _ in a_parts]
        + [pl.BlockSpec((nparts * width, tn), lambda i, j: (0, j)),
           pl.BlockSpec((tm, tn), lambda i, j: (i, j)),
           pl.BlockSpec((1, 1, tn), lambda i, j: (i // per_b, 0, j))],
        out_specs=pl.BlockSpec((tm, tn), lambda i, j: (i, j)),
        compiler_params=_params(("parallel", "parallel")),
        name=name,
    )(*a_parts, w, x2, gate.reshape(bsz, 1, n))


def _gdn_kernel(q_ref, k_ref, v_ref, z_ref, ba_ref, cw_ref, alog_ref, dtb_ref, nw_ref, o_ref,
                xbuf_ref, s_ref):
    c, d, nh = GDN_CHUNK, HEAD_DIM, GDN_HEADS
    width = nh * d
    hist = 8

    @pl.when(pl.program_id(1) == 0)
    def _():
        xbuf_ref[:, 0:hist, :] = jnp.zeros((3, hist, width), F32)
        s_ref[...] = jnp.zeros_like(s_ref)

    ba = ba_ref[...]
    beta = jax.nn.sigmoid(ba[:, 0:nh])
    g = -jnp.exp(alog_ref[...]) * jax.nn.softplus(ba[:, nh:2 * nh] + dtb_ref[...])
    r_i = lax.broadcasted_iota(I32, (c, c), 0)
    c_i = lax.broadcasted_iota(I32, (c, c), 1)
    causal = r_i >= c_i
    strict = r_i > c_i
    gam = jnp.dot(causal.astype(F32), g, precision=lax.Precision.HIGHEST, preferred_element_type=F32)
    gam_t = lax.dot_general(g, (r_i <= c_i).astype(F32), TN_DIMS, precision=lax.Precision.HIGHEST,
                            preferred_element_type=F32)
    g_last = gam[c - 1:c, :]
    egam = jnp.exp(gam)
    bexp = beta * egam
    kscale = jnp.exp(g_last - gam)
    eg_last = jnp.exp(g_last)
    nw = nw_ref[...]

    hsl = [slice(h * d, (h + 1) * d) for h in range(nh)]
    hcol = [slice(h, h + 1) for h in range(nh)]
    dot16 = lambda a, b: jnp.dot(a.astype(BF16), b.astype(BF16), preferred_element_type=F32)

    def dot32(a, b):
        ah = a.astype(BF16).astype(F32)
        bh = b.astype(BF16).astype(F32)
        lhs = jnp.concatenate([ah, ah, a - ah], axis=1).astype(BF16)
        rhs = jnp.concatenate([bh, b - bh, bh], axis=0).astype(BF16)
        return jnp.dot(lhs, rhs, preferred_element_type=F32)

    base = 8
    same = lambda s: (r_i // s) == (c_i // s)
    eye = (r_i == c_i).astype(F32)

    for g0 in range(0, nh, GDN_HEAD_GROUP):
        heads = range(g0, g0 + GDN_HEAD_GROUP)
        cols = slice(g0 * d, (g0 + GDN_HEAD_GROUP) * d)
        conv = []
        for i, src in enumerate((q_ref, k_ref, v_ref)):
            wcols = slice(i * width + g0 * d, i * width + (g0 + GDN_HEAD_GROUP) * d)
            xbuf_ref[i, hist:hist + c, cols] = src[:, cols]
            acc = cw_ref[GDN_CONV - 1:GDN_CONV, wcols] * src[:, cols]
            for j in range(GDN_CONV - 1):
                shift = GDN_CONV - 1 - j
                acc += cw_ref[j:j + 1, wcols] * xbuf_ref[i, pl.ds(hist - shift, c), cols]
            conv.append(acc * jax.nn.sigmoid(acc))
            xbuf_ref[i, 0:hist, cols] = xbuf_ref[i, c:c + hist, cols]
        cq, ck, cv = conv
        gsl = {h: slice((h - g0) * d, (h - g0 + 1) * d) for h in heads}
        qn, kn, qk, lower = {}, {}, {}, {}
        for h in heads:
            qh = cq[:, gsl[h]]
            kh = ck[:, gsl[h]]
            qh = qh * lax.rsqrt(jnp.sum(qh * qh, axis=-1, keepdims=True) + EPS) * (d ** -0.5)
            kh = kh * lax.rsqrt(jnp.sum(kh * kh, axis=-1, keepdims=True) + EPS)
            kb = kh.astype(BF16)
            both = lax.dot_general(jnp.concatenate([qh.astype(BF16), kb], axis=0), kb, NT_DIMS,
                                   preferred_element_type=F32)
            diff = gam[:, hcol[h]] - gam_t[h:h + 1, :]
            dec = jnp.where(causal, jnp.exp(jnp.where(causal, diff, 0.0)), 0.0)
            qn[h] = qh
            kn[h] = kh
            qk[h] = both[:c] * dec
            lower[h] = jnp.where(strict, beta[:, hcol[h]] * both[c:] * dec, 0.0)

        diag = {h: jnp.where(same(base), lower[h], 0.0) for h in heads}
        p2 = {h: dot32(diag[h], diag[h]) for h in heads}
        qinv = {h: p2[h] - diag[h] - dot32(diag[h], p2[h]) for h in heads}
        p4 = {h: dot32(p2[h], p2[h]) for h in heads}
        inv = {h: eye + (qinv[h] + p4[h] + dot32(qinv[h], p4[h])) for h in heads}
        size = base
        while size < c:
            off_mask = same(2 * size) & jnp.logical_not(same(size))
            t1 = {h: dot32(jnp.where(off_mask, lower[h], 0.0), inv[h]) for h in heads}
            inv = {h: inv[h] - dot32(inv[h], t1[h]) for h in heads}
            size *= 2

        sol = {}
        for h in heads:
            rhs = jnp.concatenate([cv[:, gsl[h]] * beta[:, hcol[h]], kn[h] * bexp[:, hcol[h]]], axis=1)
            sol[h] = rhs + dot16(inv[h] - eye, rhs)
        wq = {h: dot16(jnp.concatenate([sol[h][:, d:], qn[h] * egam[:, hcol[h]]], axis=0), s_ref[h])
              for h in heads}
        vnew = {h: (sol[h][:, :d] - wq[h][:c]).astype(BF16) for h in heads}
        for h in heads:
            k_dec = kn[h] * kscale[:, hcol[h]]
            s_ref[h] = s_ref[h] * eg_last[:, hcol[h]] + lax.dot_general(
                k_dec.astype(BF16), vnew[h], TN_DIMS, preferred_element_type=F32)
        for h in heads:
            o = wq[h][c:] + jnp.dot(qk[h].astype(BF16), vnew[h], preferred_element_type=F32)
            y = o * lax.rsqrt(jnp.mean(o * o, axis=-1, keepdims=True) + EPS) * nw
            zh = z_ref[:, hsl[h]]
            o_ref[:, hsl[h]] = (y * (zh * jax.nn.sigmoid(zh))).astype(o_ref.dtype)


def _gdn(pa, pb, ba_blk, conv_w, a_log, dt_bias, norm_w, seq):
    bsz = pa.shape[0]
    c, d, nh = GDN_CHUNK, HEAD_DIM, GDN_HEADS
    width = nh * d
    col_spec = lambda j: pl.BlockSpec((None, c, width), lambda b, n: (b, n, j))
    full = lambda shape: pl.BlockSpec(shape, lambda b, n: (0,) * len(shape))
    return pl.pallas_call(
        _gdn_kernel,
        out_shape=jax.ShapeDtypeStruct((bsz, seq, width), BF16),
        grid=(bsz, seq // c),
        in_specs=[col_spec(0), col_spec(1), col_spec(2), col_spec(3),
                  pl.BlockSpec((None, c, LANES), lambda b, n: (b, n, ba_blk)),
                  full((GDN_CONV, 3 * width)), full((1, nh)), full((1, nh)), full((1, d))],
        out_specs=pl.BlockSpec((None, c, width), lambda b, n: (b, n, 0)),
        scratch_shapes=[pltpu.VMEM((3, 8 + c, width), F32), pltpu.VMEM((nh, d, d), F32)],
        compiler_params=_params(("parallel", "arbitrary")),
        name="gdn_chunked",
    )(pa, pa, pa, pa, pb, conv_w, a_log.reshape(1, nh), dt_bias.reshape(1, nh), norm_w.reshape(1, d))


def _compress_kernel(x_ref, pos_ref, w1_ref, w2_ref, o_ref):
    n = o_ref.shape[0]
    d = x_ref.shape[1]
    acc_a = jnp.zeros((n, d), F32)
    acc_b = jnp.zeros((n, d), F32)
    for j in range(CMP_STRIDE):
        xj = x_ref[pl.ds(j, n, stride=CMP_STRIDE), :]
        lo = (xj + pos_ref[j:j + 1, :]).astype(BF16)
        hi = (xj + pos_ref[CMP_STRIDE + j:CMP_STRIDE + j + 1, :]).astype(BF16)
        acc_a += jnp.dot(lo, w1_ref[j * d:(j + 1) * d, :].astype(BF16), preferred_element_type=F32)
        acc_b += jnp.dot(hi, w1_ref[(CMP_STRIDE + j) * d:(CMP_STRIDE + j + 1) * d, :].astype(BF16),
                         preferred_element_type=F32)
    pre = acc_a + pltpu.roll(acc_b, n - 1, 0)
    hid = (pre * jax.nn.sigmoid(pre)).astype(BF16)
    out = jnp.dot(hid, w2_ref[...].astype(BF16), preferred_element_type=F32)
    row = lax.broadcasted_iota(I32, (n, d), 0)
    o_ref[...] = jnp.where(row < n - 1, out, 0.0).astype(o_ref.dtype)


def _compress(kvc, pos, w1, w2, seq):
    bsz = kvc.shape[0]
    g = NSA_KV_HEADS
    n = seq // CMP_STRIDE
    d = HEAD_DIM
    return pl.pallas_call(
        _compress_kernel,
        out_shape=jax.ShapeDtypeStruct((2, bsz, g, n, d), BF16),
        grid=(2, bsz, g),
        in_specs=[pl.BlockSpec((None, seq, d), lambda s, b, h: (b, 0, s * NSA_KV_HEADS + h)),
                  pl.BlockSpec((None, CMP_BLOCK, d), lambda s, b, h: (s, 0, 0)),
                  pl.BlockSpec((None, CMP_BLOCK * d, d), lambda s, b, h: (s, 0, 0)),
                  pl.BlockSpec((None, d, d), lambda s, b, h: (s, 0, 0))],
        out_specs=pl.BlockSpec((None, None, None, n, d), lambda s, b, h: (s, b, h, 0, 0)),
        compiler_params=_params(("parallel", "parallel", "parallel")),
        name="nsa_compress",
    )(kvc, pos, w1, w2)


def _softmax_rows(s, mask):
    s = jnp.where(mask, s, NEG_INF)
    m = jnp.max(s, axis=-1, keepdims=True)
    m = jnp.where(jnp.abs(m) < jnp.inf, m, 0.0)
    e = jnp.exp(s - m)
    d = jnp.sum(e, axis=-1, keepdims=True)
    return e * (1.0 / jnp.where(d > 0, d, 1.0))


def _nsa_key_aux(seq):
    key = jnp.arange(seq, dtype=I32)[:, None]
    lane = jnp.arange(LANES, dtype=I32)[None, :]
    onehot = (lane == key // SEL_BLOCK) & (lane < AUX_SEL_LANES)
    hi = (lane >= AUX_SEL_LANES) & (lane < AUX_SEL_LANES + 3)
    lo = (lane >= AUX_SEL_LANES + 3) & (lane < AUX_SEL_LANES + 6)
    aux = jnp.where(onehot, 1, 0) + jnp.where(hi, (key // 64) * 64, 0) + jnp.where(lo, key % 64, 0)
    return aux.astype(BF16)


def _nsa_query_aux(slopes):
    s1 = slopes.astype(BF16).astype(F32)
    s2 = (slopes - s1).astype(BF16).astype(F32)
    s3 = (slopes - s1 - s2).astype(BF16).astype(F32)
    pieces = jnp.stack([s1, s2, s3, s1, s2, s3], axis=-1)
    aux = jnp.zeros((slopes.shape[0], LANES), F32).at[:, AUX_SEL_LANES:AUX_SEL_LANES + 6].set(pieces)
    return aux.reshape(NSA_KV_HEADS, NSA_REP, LANES)


def _nsa_kernel(slopes_ref, qaux_ref, kaux_ref, q_ref, kc_ref, vc_ref, ksel_ref, vs_ref, kwin_ref, vw_ref,
                gate_ref, o_ref, ks_ref, kw_ref):
    seq = ks_ref.shape[0]
    tq, tk, rep, d = NSA_TQ, NSA_TK, NSA_REP, HEAD_DIM
    rows = rep * tq
    g = pl.program_id(1)
    t0 = pl.program_id(2) * tq

    @pl.when(pl.program_id(2) == 0)
    def _():
        ks_ref[:, 0:d] = ksel_ref[...]
        ks_ref[:, d:2 * d] = kaux_ref[...]
        kw_ref[:, 0:d] = kwin_ref[...]
        kw_ref[:, d:2 * d] = kaux_ref[...]

    q = q_ref[...] * (d ** -0.5)
    qs = jnp.concatenate([q[:, r * d:(r + 1) * d] for r in range(rep)], axis=0).astype(BF16)
    slope = jnp.concatenate([jnp.full((tq, 1), slopes_ref[g * rep + r], F32) for r in range(rep)], axis=0)
    row = lax.broadcasted_iota(I32, (rows, 1), 0)
    t_col = t0 + (row & (tq - 1))

    ncp = kc_ref.shape[0]
    n_idx = lax.broadcasted_iota(I32, (rows, ncp), 1)
    dist_c = t_col - (n_idx * CMP_STRIDE + (CMP_BLOCK - 1))
    s_c = lax.dot_general(qs, kc_ref[...], NT_DIMS, preferred_element_type=F32)
    s_c = s_c - slope * dist_c.astype(F32)
    p_c = _softmax_rows(s_c, dist_c >= 0).astype(BF16)
    o_c = jnp.dot(p_c, vc_ref[...], preferred_element_type=F32)

    n_sel = seq // SEL_BLOCK
    ov_s = lax.broadcasted_iota(I32, (n_sel, ncp), 0) * SEL_BLOCK
    ov_n = lax.broadcasted_iota(I32, (n_sel, ncp), 1) * CMP_STRIDE
    ov_t = ((ov_n <= ov_s + (SEL_BLOCK - 1)) & (ov_n + (CMP_BLOCK - 1) >= ov_s)).astype(BF16)
    imp = jnp.zeros((n_sel, tq), F32)
    for r in range(rep):
        imp += lax.dot_general(ov_t, p_c[r * tq:(r + 1) * tq, :], NT_DIMS, preferred_element_type=F32)
    blk = lax.broadcasted_iota(I32, (n_sel, tq), 0)
    cur = (t0 + lax.broadcasted_iota(I32, (n_sel, tq), 1)) // SEL_BLOCK
    forced = (blk == 0) | ((cur - blk) < SEL_LOCAL)
    imp = jnp.where(forced, jnp.inf, imp)
    imp = jnp.where(blk <= cur, imp, NEG_INF)
    rank = jnp.zeros((n_sel, tq), I32)
    for i in range(n_sel):
        ri = imp[i:i + 1, :]
        rank += ((ri > imp) | ((ri == imp) & (blk > i))).astype(I32)
    sel_t = (rank < min(SEL_TOPN, n_sel)).astype(F32)

    to_lanes = (lax.broadcasted_iota(I32, (n_sel, LANES), 0)
                == lax.broadcasted_iota(I32, (n_sel, LANES), 1)).astype(BF16)
    picked = lax.dot_general(sel_t.astype(BF16), to_lanes, TN_DIMS, preferred_element_type=F32)
    lane = lax.broadcasted_iota(I32, (tq, LANES), 1)
    sel_bias = jnp.where(lane < AUX_SEL_LANES, (picked - 1.0) * SEL_BIAS, 0.0)
    qaux = qaux_ref[...]
    qx_sel = jnp.concatenate(
        [qs, jnp.concatenate([sel_bias + qaux[r:r + 1, :] for r in range(rep)], axis=0).astype(BF16)], axis=1)
    qx_win = jnp.concatenate(
        [qs, jnp.concatenate([jnp.broadcast_to(qaux[r:r + 1, :], (tq, LANES)) for r in range(rep)],
                             axis=0).astype(BF16)], axis=1)

    def sel_tile(j, carry, diagonal):
        m, l, acc = carry
        k0 = pl.multiple_of(j * tk, tk)
        s = lax.dot_general(qx_sel, ks_ref[pl.ds(k0, tk), :], NT_DIMS, preferred_element_type=F32)
        if diagonal:
            ahead = (lax.broadcasted_iota(I32, (rows, tk), 1)
                     - (lax.broadcasted_iota(I32, (rows, tk), 0) & (tq - 1)))
            s = jnp.where(ahead <= t0 - k0, s, -SEL_BIAS)
        m_new = jnp.maximum(m, jnp.max(s, axis=-1, keepdims=True))
        alpha = jnp.exp(m - m_new)
        p = jnp.exp(s - m_new)
        l = alpha * l + jnp.sum(p, axis=-1, keepdims=True)
        acc = alpha * acc + jnp.dot(p.astype(BF16), vs_ref[pl.ds(k0, tk), :], preferred_element_type=F32)
        return m_new, l, acc

    n_tiles = (t0 + tq + tk - 1) // tk
    init = (jnp.full((rows, 1), NEG_INF, F32), jnp.zeros((rows, 1), F32), jnp.zeros((rows, d), F32))
    carry = lax.fori_loop(0, n_tiles - 1, lambda j, c: sel_tile(j, c, False), init)
    _, l_s, acc_s = sel_tile(n_tiles - 1, carry, True)
    o_s = acc_s * (1.0 / l_s)

    wk = tq + WINDOW
    start = pl.multiple_of(jnp.maximum(t0 - WINDOW, 0), tq)
    s_w = lax.dot_general(qx_win, kw_ref[pl.ds(start, wk), :], NT_DIMS, preferred_element_type=F32)
    dist_w = t_col - (start + lax.broadcasted_iota(I32, (rows, wk), 1))
    p_w = _softmax_rows(s_w, (dist_w >= 0) & (dist_w < WINDOW)).astype(BF16)
    o_w = jnp.dot(p_w, vw_ref[pl.ds(start, wk), :], preferred_element_type=F32)

    gt = jax.nn.sigmoid(gate_ref[...])
    for r in range(rep):
        sl = slice(r * tq, (r + 1) * tq)
        o = (gt[:, 3 * r:3 * r + 1] * o_c[sl] + gt[:, 3 * r + 1:3 * r + 2] * o_s[sl]
             + gt[:, 3 * r + 2:3 * r + 3] * o_w[sl])
        o_ref[:, r * d:(r + 1) * d] = o.astype(o_ref.dtype)


def _nsa_attention(qa, q_col0, cmp, kvb, gates, slopes, seq):
    bsz = qa.shape[0]
    g, rep, d, tq = NSA_KV_HEADS, NSA_REP, HEAD_DIM, NSA_TQ
    ncp = seq // CMP_STRIDE
    n_sel = seq // SEL_BLOCK
    assert n_sel <= AUX_SEL_LANES and seq % NSA_TK == 0 and seq >= tq + WINDOW
    qblk0 = q_col0 // (rep * d)
    kv_spec = lambda s: pl.BlockSpec((None, seq, d), lambda b, h, i: (b, 0, s * NSA_KV_HEADS + h))
    cmp_spec = lambda s: pl.BlockSpec((None, None, None, ncp, d), lambda b, h, i: (s, b, h, 0, 0))
    return pl.pallas_call(
        _nsa_kernel,
        out_shape=jax.ShapeDtypeStruct((bsz, seq, NSA_DQ), BF16),
        grid=(bsz, g, seq // tq),
        in_specs=[pl.BlockSpec(memory_space=pltpu.SMEM),
                  pl.BlockSpec((None, rep, LANES), lambda b, h, i: (h, 0, 0)),
                  pl.BlockSpec((seq, LANES), lambda b, h, i: (0, 0)),
                  pl.BlockSpec((None, tq, rep * d), lambda b, h, i: (b, i, qblk0 + h)),
                  cmp_spec(0), cmp_spec(1), kv_spec(0), kv_spec(1), kv_spec(2), kv_spec(3),
                  pl.BlockSpec((None, None, tq, 3 * rep), lambda b, h, i: (b, h, i, 0))],
        out_specs=pl.BlockSpec((None, tq, rep * d), lambda b, h, i: (b, i, h)),
        scratch_shapes=[pltpu.VMEM((seq, 2 * d), BF16), pltpu.VMEM((seq, 2 * d), BF16)],
        compiler_params=_params(("parallel", "parallel", "arbitrary")),
        name="nsa_attention",
    )(slopes, _nsa_query_aux(slopes), _nsa_key_aux(seq), qa, cmp, cmp, kvb, kvb, kvb, kvb, gates)


def _split_columns(proj):
    out, start = [], 0
    for size in IN_SIZES:
        out.append(proj[..., start:start + size])
        start += size
    return out


def kernel(x, c, ada_w, ada_b, norm1_w, w_in, gdn_conv_w, gdn_a_log, gdn_dt_bias, gdn_norm_w,
           cmp_pos_k, cmp_w1_k, cmp_w2_k, cmp_pos_v, cmp_w1_v, cmp_w2_v, w_out, norm2_w,
           w_up, w_down, final_norm_w):
    bsz, seq, d = x.shape
    depth = ada_w.shape[0]
    m = bsz * seq
    x2 = x.reshape(m, d)
    slopes = 2.0 ** (-8.0 * jnp.arange(1, NSA_HEADS + 1, dtype=F32) / NSA_HEADS)
    for i in range(depth):
        mod = _modulation(c, ada_w[i], ada_b[i])
        sh1, sc1, g1, sh2, sc2, g2 = jnp.split(mod, 6, axis=-1)
        h = _norm_mod(x2, norm1_w[i], sc1, sh1, seq, BF16)
        cols = _split_columns(w_in[i])
        small = jnp.concatenate([cols[4], cols[5], cols[13]], axis=-1)
        small = jnp.pad(small, ((0, 0), (0, LANES - small.shape[1])))
        w_a = jnp.concatenate(cols[0:4] + [cols[6]], axis=-1).astype(BF16)
        w_b = jnp.concatenate(cols[7:9] + [small], axis=-1).astype(BF16)
        w_c = jnp.concatenate(cols[9:13], axis=-1).astype(BF16)
        pa = _matmul(h, w_a, tm=1024, tn=1024, out_dtype=F32, name="in_proj_a").reshape(bsz, seq, -1)
        pb = _matmul(h, w_b, tm=1024, tn=w_b.shape[1], out_dtype=F32, name="in_proj_b").reshape(bsz, seq, -1)
        pc = _matmul(h, w_c, tm=1024, tn=1024, out_dtype=BF16, name="in_proj_c").reshape(bsz, seq, -1)

        o_a = _gdn(pa, pb, 2 * NSA_DKV // LANES, gdn_conv_w[i], gdn_a_log[i], gdn_dt_bias[i], gdn_norm_w[i], seq)
        ngate = pb[..., 2 * NSA_DKV + 2 * GDN_HEADS:2 * NSA_DKV + 2 * GDN_HEADS + 3 * NSA_HEADS]

        cmp = _compress(pb, jnp.stack([cmp_pos_k[i], cmp_pos_v[i]]), jnp.stack([cmp_w1_k[i], cmp_w1_v[i]]),
                        jnp.stack([cmp_w2_k[i], cmp_w2_v[i]]), seq)
        gates = ngate.reshape(bsz, seq, NSA_KV_HEADS, 3 * NSA_REP).transpose(0, 2, 1, 3)
        o_b = _nsa_attention(pa, 4 * GDN_DK, cmp, pc, gates, slopes, seq)

        x2 = _matmul_parts_residual([o_a.reshape(m, -1), o_b.reshape(m, -1)], w_out[i].astype(BF16), x2, g1, seq,
                                    tm=1024, tn=1024, name="out_proj")
        h2 = _norm_mod(x2, norm2_w[i], sc2, sh2, seq, BF16)
        hid = _matmul(h2, w_up[i].astype(BF16), tm=1024, tn=1024, out_dtype=BF16, relu2=True, name="mlp_up")
        x2 = _matmul_residual(hid, w_down[i].astype(BF16), x2, g2, seq,
                              tm=1024, tn=1024, tk=4096, name="mlp_down")
    return _final_norm(x2, final_norm_w).reshape(bsz, seq, d)
```

```python
import functools

import jax
import jax.numpy as jnp
from jax import lax
from jax.experimental import pallas as pl
from jax.experimental.pallas import tpu as pltpu

F32 = jnp.float32
BF16 = jnp.bfloat16
I32 = jnp.int32

HEAD_DIM = 128
GDN_HEADS = 16
NSA_HEADS = 16
NSA_KV_HEADS = 4
NSA_REP = NSA_HEADS // NSA_KV_HEADS
GDN_DK = GDN_HEADS * HEAD_DIM
NSA_DQ = NSA_HEADS * HEAD_DIM
NSA_DKV = NSA_KV_HEADS * HEAD_DIM
GDN_CONV = 4
GDN_CHUNK = 64
GDN_HEAD_GROUP = 16
CMP_BLOCK = 32
CMP_STRIDE = 16
SEL_BLOCK = 64
SEL_TOPN = 16
SEL_LOCAL = 2
WINDOW = 512
EPS = 1e-6
IN_SIZES = (GDN_DK, GDN_DK, GDN_DK, GDN_DK, GDN_HEADS, GDN_HEADS,
            NSA_DQ, NSA_DKV, NSA_DKV, NSA_DKV, NSA_DKV, NSA_DKV, NSA_DKV, 3 * NSA_HEADS)

VMEM_LIMIT_V7X = 56 * 1024 * 1024
LANES = 128
NSA_TQ = 256
NSA_TK = 1024
NEG_INF = float("-inf")
AUX_SEL_LANES = 64
SEL_BIAS = 2.0 ** 100
LOG2E = 1.4426950408889634
NT_DIMS = (((1,), (1,)), ((), ()))
TN_DIMS = (((0,), (0,)), ((), ()))


def _params(sem):
    return pltpu.CompilerParams(dimension_semantics=sem, vmem_limit_bytes=VMEM_LIMIT_V7X)


def _mod_kernel(c_ref, w_ref, b_ref, o_ref):
    c = c_ref[...]
    s = (c * jax.nn.sigmoid(c)).astype(BF16)
    o_ref[...] = jnp.dot(s, w_ref[...].astype(BF16), preferred_element_type=F32) + b_ref[...]


def _modulation(c, w, b):
    bsz, d = c.shape
    n = w.shape[1]
    rows = 8
    cp = jnp.zeros((rows, d), F32).at[:bsz].set(c)
    tn = 512
    out = pl.pallas_call(
        _mod_kernel,
        out_shape=jax.ShapeDtypeStruct((rows, n), F32),
        grid=(n // tn,),
        in_specs=[pl.BlockSpec((rows, d), lambda j: (0, 0)),
                  pl.BlockSpec((d, tn), lambda j: (0, j)),
                  pl.BlockSpec((1, tn), lambda j: (0, j))],
        out_specs=pl.BlockSpec((rows, tn), lambda j: (0, j)),
        compiler_params=_params(("parallel",)),
        name="adaln_mod",
    )(cp, w, b.reshape(1, n))
    return out[:bsz]


def _norm_kernel(x_ref, g_ref, sc_ref, sh_ref, o_ref):
    x = x_ref[...]
    y = x * lax.rsqrt(jnp.mean(x * x, axis=-1, keepdims=True) + EPS) * g_ref[...]
    o_ref[...] = (y * (1.0 + sc_ref[0]) + sh_ref[0]).astype(o_ref.dtype)


def _final_norm_kernel(x_ref, g_ref, o_ref):
    x = x_ref[...]
    o_ref[...] = x * lax.rsqrt(jnp.mean(x * x, axis=-1, keepdims=True) + EPS) * g_ref[...]


def _norm_mod(x2, gain, scale, shift, seq, out_dtype):
    m, d = x2.shape
    tm = 256
    per_b = seq // tm
    bsz = m // seq
    return pl.pallas_call(
        _norm_kernel,
        out_shape=jax.ShapeDtypeStruct((m, d), out_dtype),
        grid=(m // tm,),
        in_specs=[pl.BlockSpec((tm, d), lambda i: (i, 0)),
                  pl.BlockSpec((1, d), lambda i: (0, 0)),
                  pl.BlockSpec((1, 1, d), lambda i: (i // per_b, 0, 0)),
                  pl.BlockSpec((1, 1, d), lambda i: (i // per_b, 0, 0))],
        out_specs=pl.BlockSpec((tm, d), lambda i: (i, 0)),
        compiler_params=_params(("parallel",)),
        name="rmsnorm_mod",
    )(x2, gain.reshape(1, d), scale.reshape(bsz, 1, d), shift.reshape(bsz, 1, d))


def _final_norm(x2, gain):
    m, d = x2.shape
    tm = 256
    return pl.pallas_call(
        _final_norm_kernel,
        out_shape=jax.ShapeDtypeStruct((m, d), F32),
        grid=(m // tm,),
        in_specs=[pl.BlockSpec((tm, d), lambda i: (i, 0)),
                  pl.BlockSpec((1, d), lambda i: (0, 0))],
        out_specs=pl.BlockSpec((tm, d), lambda i: (i, 0)),
        compiler_params=_params(("parallel",)),
        name="final_rmsnorm",
    )(x2, gain.reshape(1, d))


def _mm_kernel(a_ref, w_ref, o_ref, *, relu2):
    acc = jnp.dot(a_ref[...], w_ref[...], preferred_element_type=F32)
    if relu2:
        acc = jnp.square(jnp.maximum(acc, 0.0))
    o_ref[...] = acc.astype(o_ref.dtype)


def _matmul(a, w, *, tm, tn, out_dtype, relu2=False, name):
    m, k = a.shape
    n = w.shape[1]
    return pl.pallas_call(
        functools.partial(_mm_kernel, relu2=relu2),
        out_shape=jax.ShapeDtypeStruct((m, n), out_dtype),
        grid=(m // tm, n // tn),
        in_specs=[pl.BlockSpec((tm, k), lambda i, j: (i, 0)),
                  pl.BlockSpec((k, tn), lambda i, j: (0, j))],
        out_specs=pl.BlockSpec((tm, tn), lambda i, j: (i, j)),
        compiler_params=_params(("parallel", "parallel")),
        name=name,
    )(a, w)


def _mm_res_kernel(a_ref, w_ref, x_ref, g_ref, o_ref):
    k = pl.program_id(2)

    @pl.when(k == 0)
    def _():
        o_ref[...] = jnp.zeros_like(o_ref)

    o_ref[...] += jnp.dot(a_ref[...], w_ref[...], preferred_element_type=F32)

    @pl.when(k == pl.num_programs(2) - 1)
    def _():
        o_ref[...] = x_ref[...] + g_ref[0] * o_ref[...]


def _matmul_residual(a, w, x2, gate, seq, *, tm, tn, tk, name):
    m, kdim = a.shape
    n = w.shape[1]
    bsz = m // seq
    per_b = seq // tm
    return pl.pallas_call(
        _mm_res_kernel,
        out_shape=jax.ShapeDtypeStruct((m, n), F32),
        grid=(m // tm, n // tn, kdim // tk),
        in_specs=[pl.BlockSpec((tm, tk), lambda i, j, k: (i, k)),
                  pl.BlockSpec((tk, tn), lambda i, j, k: (k, j)),
                  pl.BlockSpec((tm, tn), lambda i, j, k: (i, j)),
                  pl.BlockSpec((1, 1, tn), lambda i, j, k: (i // per_b, 0, j))],
        out_specs=pl.BlockSpec((tm, tn), lambda i, j, k: (i, j)),
        compiler_params=_params(("parallel", "parallel", "arbitrary")),
        name=name,
    )(a, w, x2, gate.reshape(bsz, 1, n))


def _mm_parts_res_kernel(*refs, nparts):
    a_refs = refs[:nparts]
    w_ref, x_ref, g_ref, o_ref = refs[nparts:]
    width = a_refs[0].shape[1]
    acc = jnp.dot(a_refs[0][...], w_ref[0:width, :], preferred_element_type=F32)
    for idx in range(1, nparts):
        acc += jnp.dot(a_refs[idx][...], w_ref[idx * width:(idx + 1) * width, :], preferred_element_type=F32)
    o_ref[...] = x_ref[...] + g_ref[0] * acc


def _matmul_parts_residual(a_parts, w, x2, gate, seq, *, tm, tn, name):
    nparts = len(a_parts)
    m, width = a_parts[0].shape
    assert all(a.shape == (m, width) for a in a_parts) and w.shape[0] == nparts * width
    n = w.shape[1]
    bsz = m // seq
    per_b = seq // tm
    return pl.pallas_call(
        functools.partial(_mm_parts_res_kernel, nparts=nparts),
        out_shape=jax.ShapeDtypeStruct((m, n), F32),
        grid=(m // tm, n // tn),
        in_specs=[pl.BlockSpec((tm, width), lambda i, j: (i, 0)) for _ in a_parts]
        + [pl.BlockSpec((nparts * width, tn), lambda i, j: (0, j)),
           pl.BlockSpec((tm, tn), lambda i, j: (i, j)),
           pl.BlockSpec((1, 1, tn), lambda i, j: (i // per_b, 0, j))],
        out_specs=pl.BlockSpec((tm, tn), lambda i, j: (i, j)),
        compiler_params=_params(("parallel", "parallel")),
        name=name,
    )(*a_parts, w, x2, gate.reshape(bsz, 1, n))


def _gdn_kernel(q_ref, k_ref, v_ref, z_ref, ba_ref, cw_ref, alog_ref, dtb_ref, nw_ref, o_ref,
                xbuf_ref, s_ref):
    c, d, nh = GDN_CHUNK, HEAD_DIM, GDN_HEADS
    width = nh * d
    hist = 8

    @pl.when(pl.program_id(1) == 0)
    def _():
        xbuf_ref[:, 0:hist, :] = jnp.zeros((3, hist, width), F32)
        s_ref[...] = jnp.zeros_like(s_ref)

    ba = ba_ref[...]
    beta = jax.nn.sigmoid(ba[:, 0:nh])
    g = -jnp.exp(alog_ref[...]) * jax.nn.softplus(ba[:, nh:2 * nh] + dtb_ref[...])
    r_i = lax.broadcasted_iota(I32, (c, c), 0)
    c_i = lax.broadcasted_iota(I32, (c, c), 1)
    causal = r_i >= c_i
    strict = r_i > c_i
    gam = jnp.dot(causal.astype(F32), g, precision=lax.Precision.HIGHEST, preferred_element_type=F32)
    gam_t = lax.dot_general(g, (r_i <= c_i).astype(F32), TN_DIMS, precision=lax.Precision.HIGHEST,
                            preferred_element_type=F32)
    g_last = gam[c - 1:c, :]
    egam = jnp.exp(gam)
    bexp = beta * egam
    kscale = jnp.exp(g_last - gam)
    eg_last = jnp.exp(g_last)
    nw = nw_ref[...]

    hsl = [slice(h * d, (h + 1) * d) for h in range(nh)]
    hcol = [slice(h, h + 1) for h in range(nh)]
    dot16 = lambda a, b: jnp.dot(a.astype(BF16), b.astype(BF16), preferred_element_type=F32)

    def split(a):
        hi = a.astype(BF16).astype(F32)
        return hi, a - hi

    as_lhs = lambda sp: jnp.concatenate([sp[0], sp[0], sp[1]], axis=1).astype(BF16)
    as_rhs = lambda sp: jnp.concatenate([sp[0], sp[1], sp[0]], axis=0).astype(BF16)
    dot32 = lambda lhs, rhs: jnp.dot(lhs, rhs, preferred_element_type=F32)

    base = 8
    same = lambda s: (r_i // s) == (c_i // s)
    eye = (r_i == c_i).astype(F32)

    for g0 in range(0, nh, GDN_HEAD_GROUP):
        heads = range(g0, g0 + GDN_HEAD_GROUP)
        cols = slice(g0 * d, (g0 + GDN_HEAD_GROUP) * d)
        conv = []
        for i, src in enumerate((q_ref, k_ref, v_ref)):
            wcols = slice(i * width + g0 * d, i * width + (g0 + GDN_HEAD_GROUP) * d)
            xbuf_ref[i, hist:hist + c, cols] = src[:, cols]
            acc = cw_ref[GDN_CONV - 1:GDN_CONV, wcols] * src[:, cols]
            for j in range(GDN_CONV - 1):
                shift = GDN_CONV - 1 - j
                acc += cw_ref[j:j + 1, wcols] * xbuf_ref[i, pl.ds(hist - shift, c), cols]
            conv.append(acc * jax.nn.sigmoid(acc))
            xbuf_ref[i, 0:hist, cols] = xbuf_ref[i, c:c + hist, cols]
        cq, ck, cv = conv
        gsl = {h: slice((h - g0) * d, (h - g0 + 1) * d) for h in heads}
        qn, kn, qk, lower = {}, {}, {}, {}
        for h in heads:
            qh = cq[:, gsl[h]]
            kh = ck[:, gsl[h]]
            qh = qh * lax.rsqrt(jnp.sum(qh * qh, axis=-1, keepdims=True) + EPS) * (d ** -0.5)
            kh = kh * lax.rsqrt(jnp.sum(kh * kh, axis=-1, keepdims=True) + EPS)
            kb = kh.astype(BF16)
            both = lax.dot_general(jnp.concatenate([qh.astype(BF16), kb], axis=0), kb, NT_DIMS,
                                   preferred_element_type=F32)
            diff = gam[:, hcol[h]] - gam_t[h:h + 1, :]
            dec = jnp.where(causal, jnp.exp(jnp.where(causal, diff, 0.0)), 0.0)
            qn[h] = qh
            kn[h] = kh
            qk[h] = both[:c] * dec
            lower[h] = jnp.where(strict, beta[:, hcol[h]] * both[c:] * dec, 0.0)

        diag = {h: jnp.where(same(base), lower[h], 0.0) for h in heads}
        dsp = {h: split(diag[h]) for h in heads}
        dlhs = {h: as_lhs(dsp[h]) for h in heads}
        p2 = {h: dot32(dlhs[h], as_rhs(dsp[h])) for h in heads}
        psp = {h: split(p2[h]) for h in heads}
        prhs = {h: as_rhs(psp[h]) for h in heads}
        qinv = {h: p2[h] - diag[h] - dot32(dlhs[h], prhs[h]) for h in heads}
        p4 = {h: dot32(as_lhs(psp[h]), prhs[h]) for h in heads}
        inv = {h: eye + (qinv[h] + p4[h] + dot32(as_lhs(split(qinv[h])), as_rhs(split(p4[h]))))
               for h in heads}
        size = base
        while size < c:
            off_mask = same(2 * size) & jnp.logical_not(same(size))
            t1 = {h: dot16(jnp.where(off_mask, lower[h], 0.0), inv[h]) for h in heads}
            inv = {h: inv[h] - dot16(inv[h], t1[h]) for h in heads}
            size *= 2

        sol = {}
        for h in heads:
            rhs = jnp.concatenate([cv[:, gsl[h]] * beta[:, hcol[h]], kn[h] * bexp[:, hcol[h]]], axis=1)
            sol[h] = rhs + dot16(inv[h] - eye, rhs)
        wq = {h: dot16(jnp.concatenate([sol[h][:, d:], qn[h] * egam[:, hcol[h]]], axis=0), s_ref[h])
              for h in heads}
        vnew = {h: (sol[h][:, :d] - wq[h][:c]).astype(BF16) for h in heads}
        for h in heads:
            k_dec = kn[h] * kscale[:, hcol[h]]
            s_ref[h] = s_ref[h] * eg_last[:, hcol[h]] + lax.dot_general(
                k_dec.astype(BF16), vnew[h], TN_DIMS, preferred_element_type=F32)
        for h in heads:
            o = wq[h][c:] + jnp.dot(qk[h].astype(BF16), vnew[h], preferred_element_type=F32)
            y = o * lax.rsqrt(jnp.mean(o * o, axis=-1, keepdims=True) + EPS) * nw
            zh = z_ref[:, hsl[h]]
            o_ref[:, hsl[h]] = (y * (zh * jax.nn.sigmoid(zh))).astype(o_ref.dtype)


def _gdn(pa, pb, ba_blk, conv_w, a_log, dt_bias, norm_w, seq):
    bsz = pa.shape[0]
    c, d, nh = GDN_CHUNK, HEAD_DIM, GDN_HEADS
    width = nh * d
    col_spec = lambda j: pl.BlockSpec((None, c, width), lambda b, n: (b, n, j))
    full = lambda shape: pl.BlockSpec(shape, lambda b, n: (0,) * len(shape))
    return pl.pallas_call(
        _gdn_kernel,
        out_shape=jax.ShapeDtypeStruct((bsz, seq, width), BF16),
        grid=(bsz, seq // c),
        in_specs=[col_spec(0), col_spec(1), col_spec(2), col_spec(3),
                  pl.BlockSpec((None, c, LANES), lambda b, n: (b, n, ba_blk)),
                  full((GDN_CONV, 3 * width)), full((1, nh)), full((1, nh)), full((1, d))],
        out_specs=pl.BlockSpec((None, c, width), lambda b, n: (b, n, 0)),
        scratch_shapes=[pltpu.VMEM((3, 8 + c, width), F32), pltpu.VMEM((nh, d, d), F32)],
        compiler_params=_params(("parallel", "arbitrary")),
        name="gdn_chunked",
    )(pa, pa, pa, pa, pb, conv_w, a_log.reshape(1, nh), dt_bias.reshape(1, nh), norm_w.reshape(1, d))


def _compress_kernel(x_ref, pos_ref, w1_ref, w2_ref, o_ref):
    n = o_ref.shape[0]
    d = x_ref.shape[1]
    acc_a = jnp.zeros((n, d), F32)
    acc_b = jnp.zeros((n, d), F32)
    for j in range(CMP_STRIDE):
        xj = x_ref[pl.ds(j, n, stride=CMP_STRIDE), :]
        lo = (xj + pos_ref[j:j + 1, :]).astype(BF16)
        hi = (xj + pos_ref[CMP_STRIDE + j:CMP_STRIDE + j + 1, :]).astype(BF16)
        acc_a += jnp.dot(lo, w1_ref[j * d:(j + 1) * d, :].astype(BF16), preferred_element_type=F32)
        acc_b += jnp.dot(hi, w1_ref[(CMP_STRIDE + j) * d:(CMP_STRIDE + j + 1) * d, :].astype(BF16),
                         preferred_element_type=F32)
    pre = acc_a + pltpu.roll(acc_b, n - 1, 0)
    hid = (pre * jax.nn.sigmoid(pre)).astype(BF16)
    out = jnp.dot(hid, w2_ref[...].astype(BF16), preferred_element_type=F32)
    row = lax.broadcasted_iota(I32, (n, d), 0)
    o_ref[...] = jnp.where(row < n - 1, out, 0.0).astype(o_ref.dtype)


def _compress(kvc, pos, w1, w2, seq):
    bsz = kvc.shape[0]
    g = NSA_KV_HEADS
    n = seq // CMP_STRIDE
    d = HEAD_DIM
    return pl.pallas_call(
        _compress_kernel,
        out_shape=jax.ShapeDtypeStruct((2, bsz, g, n, d), BF16),
        grid=(2, bsz, g),
        in_specs=[pl.BlockSpec((None, seq, d), lambda s, b, h: (b, 0, s * NSA_KV_HEADS + h)),
                  pl.BlockSpec((None, CMP_BLOCK, d), lambda s, b, h: (s, 0, 0)),
                  pl.BlockSpec((None, CMP_BLOCK * d, d), lambda s, b, h: (s, 0, 0)),
                  pl.BlockSpec((None, d, d), lambda s, b, h: (s, 0, 0))],
        out_specs=pl.BlockSpec((None, None, None, n, d), lambda s, b, h: (s, b, h, 0, 0)),
        compiler_params=_params(("parallel", "parallel", "parallel")),
        name="nsa_compress",
    )(kvc, pos, w1, w2)


def _softmax_rows(s, mask):
    s = jnp.where(mask, s, NEG_INF)
    m = jnp.max(s, axis=-1, keepdims=True)
    m = jnp.where(jnp.abs(m) < jnp.inf, m, 0.0)
    e = jnp.exp(s - m)
    d = jnp.sum(e, axis=-1, keepdims=True)
    return e * (1.0 / jnp.where(d > 0, d, 1.0))


def _nsa_key_aux(seq):
    key = jnp.arange(seq, dtype=I32)[:, None]
    lane = jnp.arange(LANES, dtype=I32)[None, :]
    onehot = (lane == key // SEL_BLOCK) & (lane < AUX_SEL_LANES)
    hi = (lane >= AUX_SEL_LANES) & (lane < AUX_SEL_LANES + 3)
    lo = (lane >= AUX_SEL_LANES + 3) & (lane < AUX_SEL_LANES + 6)
    aux = jnp.where(onehot, 1, 0) + jnp.where(hi, (key // 64) * 64, 0) + jnp.where(lo, key % 64, 0)
    return aux.astype(BF16)


def _nsa_query_aux(slopes):
    s1 = slopes.astype(BF16).astype(F32)
    s2 = (slopes - s1).astype(BF16).astype(F32)
    s3 = (slopes - s1 - s2).astype(BF16).astype(F32)
    pieces = jnp.stack([s1, s2, s3, s1, s2, s3], axis=-1)
    aux = jnp.zeros((slopes.shape[0], LANES), F32).at[:, AUX_SEL_LANES:AUX_SEL_LANES + 6].set(pieces)
    return aux.reshape(NSA_KV_HEADS, NSA_REP, LANES)


def _nsa_overlap(n_sel, n_cmp):
    lo = jnp.arange(n_sel, dtype=I32)[:, None] * SEL_BLOCK
    start = jnp.arange(n_cmp, dtype=I32)[None, :] * CMP_STRIDE
    return ((start <= lo + (SEL_BLOCK - 1)) & (start + (CMP_BLOCK - 1) >= lo)).astype(BF16)


def _nsa_kernel(slopes_ref, qaux_ref, kaux_ref, ov_ref, q_ref, kc_ref, vc_ref, ksel_ref, vs_ref, kwin_ref,
                vw_ref, gate_ref, o_ref, ks_ref, kw_ref):
    seq = ks_ref.shape[0]
    tq, tk, rep, d = NSA_TQ, NSA_TK, NSA_REP, HEAD_DIM
    rows = rep * tq
    g = pl.program_id(1)
    t0 = pl.program_id(2) * tq

    @pl.when(pl.program_id(2) == 0)
    def _():
        ks_ref[:, 0:d] = ksel_ref[...]
        ks_ref[:, d:2 * d] = kaux_ref[...]
        kw_ref[:, 0:d] = kwin_ref[...]
        kw_ref[:, d:2 * d] = kaux_ref[...]

    q = q_ref[...] * (d ** -0.5)
    q = jnp.concatenate([q[:, r * d:(r + 1) * d] for r in range(rep)], axis=0)
    qs = q.astype(BF16)
    qs2 = (q * LOG2E).astype(BF16)
    slope = jnp.concatenate([jnp.full((tq, 1), slopes_ref[g * rep + r], F32) for r in range(rep)], axis=0)
    row = lax.broadcasted_iota(I32, (rows, 1), 0)
    t_col = t0 + (row & (tq - 1))

    ncp = kc_ref.shape[0]
    n_idx = lax.broadcasted_iota(I32, (rows, ncp), 1)
    dist_c = t_col - (n_idx * CMP_STRIDE + (CMP_BLOCK - 1))
    s_c = lax.dot_general(qs, kc_ref[...], NT_DIMS, preferred_element_type=F32)
    s_c = s_c - slope * dist_c.astype(F32)
    p_c = _softmax_rows(s_c, dist_c >= 0).astype(BF16)
    o_c = jnp.dot(p_c, vc_ref[...], preferred_element_type=F32)

    n_sel = seq // SEL_BLOCK
    ov_t = ov_ref[...]
    imp = jnp.zeros((n_sel, tq), F32)
    for r in range(rep):
        imp += lax.dot_general(ov_t, p_c[r * tq:(r + 1) * tq, :], NT_DIMS, preferred_element_type=F32)
    blk = lax.broadcasted_iota(I32, (n_sel, tq), 0)
    cur = lax.shift_right_logical(t0 + lax.broadcasted_iota(I32, (n_sel, tq), 1), SEL_BLOCK.bit_length() - 1)
    forced = (blk == 0) | ((cur - blk) < SEL_LOCAL)
    imp = jnp.where(forced, jnp.inf, imp)
    imp = jnp.where(blk <= cur, imp, NEG_INF)
    rank = jnp.zeros((n_sel, tq), I32)
    for i in range(n_sel):
        ri = imp[i:i + 1, :]
        rank += ((ri > imp) | ((ri == imp) & (blk > i))).astype(I32)
    sel_t = (rank < min(SEL_TOPN, n_sel)).astype(F32)

    to_lanes = (lax.broadcasted_iota(I32, (n_sel, LANES), 0)
                == lax.broadcasted_iota(I32, (n_sel, LANES), 1)).astype(BF16)
    picked = lax.dot_general(sel_t.astype(BF16), to_lanes, TN_DIMS, preferred_element_type=F32)
    lane = lax.broadcasted_iota(I32, (tq, LANES), 1)
    sel_bias = jnp.where(lane < AUX_SEL_LANES, (picked - 1.0) * SEL_BIAS, 0.0)
    qaux = qaux_ref[...]
    qx_sel = jnp.concatenate(
        [qs2, jnp.concatenate([sel_bias + qaux[r:r + 1, :] for r in range(rep)], axis=0).astype(BF16)], axis=1)
    qx_win = jnp.concatenate(
        [qs2, jnp.concatenate([jnp.broadcast_to(qaux[r:r + 1, :], (tq, LANES)) for r in range(rep)],
                              axis=0).astype(BF16)], axis=1)

    def sel_tile(j, carry, diagonal):
        m, l, acc = carry
        k0 = pl.multiple_of(j * tk, tk)
        s = lax.dot_general(qx_sel, ks_ref[pl.ds(k0, tk), :], NT_DIMS, preferred_element_type=F32)
        if diagonal:
            ahead = (lax.broadcasted_iota(I32, (rows, tk), 1)
                     - (lax.broadcasted_iota(I32, (rows, tk), 0) & (tq - 1)))
            s = jnp.where(ahead <= t0 - k0, s, -SEL_BIAS)
        m_new = jnp.maximum(m, jnp.max(s, axis=-1, keepdims=True))
        alpha = jnp.exp2(m - m_new)
        p = jnp.exp2(s - m_new)
        l = alpha * l + jnp.sum(p, axis=-1, keepdims=True)
        acc = alpha * acc + jnp.dot(p.astype(BF16), vs_ref[pl.ds(k0, tk), :], preferred_element_type=F32)
        return m_new, l, acc

    n_tiles = (t0 + tq + tk - 1) // tk
    init = (jnp.full((rows, 1), NEG_INF, F32), jnp.zeros((rows, 1), F32), jnp.zeros((rows, d), F32))
    carry = lax.fori_loop(0, n_tiles - 1, lambda j, c: sel_tile(j, c, False), init)
    _, l_s, acc_s = sel_tile(n_tiles - 1, carry, True)
    o_s = acc_s * (1.0 / l_s)

    wk = tq + WINDOW
    start = pl.multiple_of(jnp.maximum(t0 - WINDOW, 0), tq)
    s_w = lax.dot_general(qx_win, kw_ref[pl.ds(start, wk), :], NT_DIMS, preferred_element_type=F32)
    dist_w = t_col - (start + lax.broadcasted_iota(I32, (rows, wk), 1))
    s_w = jnp.where((dist_w >= 0) & (dist_w < WINDOW), s_w, NEG_INF)
    e_w = jnp.exp2(s_w - jnp.max(s_w, axis=-1, keepdims=True))
    d_w = jnp.sum(e_w, axis=-1, keepdims=True)
    o_w = jnp.dot(e_w.astype(BF16), vw_ref[pl.ds(start, wk), :], preferred_element_type=F32) * (1.0 / d_w)

    gt = jax.nn.sigmoid(gate_ref[...])
    for r in range(rep):
        sl = slice(r * tq, (r + 1) * tq)
        o = (gt[:, 3 * r:3 * r + 1] * o_c[sl] + gt[:, 3 * r + 1:3 * r + 2] * o_s[sl]
             + gt[:, 3 * r + 2:3 * r + 3] * o_w[sl])
        o_ref[:, r * d:(r + 1) * d] = o.astype(o_ref.dtype)


def _nsa_attention(qa, q_col0, cmp, kvb, gates, slopes, seq):
    bsz = qa.shape[0]
    g, rep, d, tq = NSA_KV_HEADS, NSA_REP, HEAD_DIM, NSA_TQ
    ncp = seq // CMP_STRIDE
    n_sel = seq // SEL_BLOCK
    assert n_sel <= AUX_SEL_LANES and seq % NSA_TK == 0 and seq >= tq + WINDOW
    qblk0 = q_col0 // (rep * d)
    kv_spec = lambda s: pl.BlockSpec((None, seq, d), lambda b, h, i: (b, 0, s * NSA_KV_HEADS + h))
    cmp_spec = lambda s: pl.BlockSpec((None, None, None, ncp, d), lambda b, h, i: (s, b, h, 0, 0))
    return pl.pallas_call(
        _nsa_kernel,
        out_shape=jax.ShapeDtypeStruct((bsz, seq, NSA_DQ), BF16),
        grid=(bsz, g, seq // tq),
        in_specs=[pl.BlockSpec(memory_space=pltpu.SMEM),
                  pl.BlockSpec((None, rep, LANES), lambda b, h, i: (h, 0, 0)),
                  pl.BlockSpec((seq, LANES), lambda b, h, i: (0, 0)),
                  pl.BlockSpec((n_sel, ncp), lambda b, h, i: (0, 0)),
                  pl.BlockSpec((None, tq, rep * d), lambda b, h, i: (b, i, qblk0 + h)),
                  cmp_spec(0), cmp_spec(1), kv_spec(0), kv_spec(1), kv_spec(2), kv_spec(3),
                  pl.BlockSpec((None, None, tq, 3 * rep), lambda b, h, i: (b, h, i, 0))],
        out_specs=pl.BlockSpec((None, tq, rep * d), lambda b, h, i: (b, i, h)),
        scratch_shapes=[pltpu.VMEM((seq, 2 * d), BF16), pltpu.VMEM((seq, 2 * d), BF16)],
        compiler_params=_params(("parallel", "parallel", "arbitrary")),
        name="nsa_attention",
    )(slopes, _nsa_query_aux(slopes * LOG2E), _nsa_key_aux(seq), _nsa_overlap(n_sel, ncp), qa, cmp, cmp, kvb, kvb, kvb, kvb, gates)


def _split_columns(proj):
    out, start = [], 0
    for size in IN_SIZES:
        out.append(proj[..., start:start + size])
        start += size
    return out


def kernel(x, c, ada_w, ada_b, norm1_w, w_in, gdn_conv_w, gdn_a_log, gdn_dt_bias, gdn_norm_w,
           cmp_pos_k, cmp_w1_k, cmp_w2_k, cmp_pos_v, cmp_w1_v, cmp_w2_v, w_out, norm2_w,
           w_up, w_down, final_norm_w):
    bsz, seq, d = x.shape
    depth = ada_w.shape[0]
    m = bsz * seq
    x2 = x.reshape(m, d)
    slopes = 2.0 ** (-8.0 * jnp.arange(1, NSA_HEADS + 1, dtype=F32) / NSA_HEADS)
    for i in range(depth):
        mod = _modulation(c, ada_w[i], ada_b[i])
        sh1, sc1, g1, sh2, sc2, g2 = jnp.split(mod, 6, axis=-1)
        h = _norm_mod(x2, norm1_w[i], sc1, sh1, seq, BF16)
        cols = _split_columns(w_in[i])
        small = jnp.concatenate([cols[4], cols[5], cols[13]], axis=-1)
        small = jnp.pad(small, ((0, 0), (0, LANES - small.shape[1])))
        w_a = jnp.concatenate(cols[0:4] + [cols[6]], axis=-1).astype(BF16)
        w_b = jnp.concatenate(cols[7:9] + [small], axis=-1).astype(BF16)
        w_c = jnp.concatenate(cols[9:13], axis=-1).astype(BF16)
        pa = _matmul(h, w_a, tm=1024, tn=1024, out_dtype=F32, name="in_proj_a").reshape(bsz, seq, -1)
        pb = _matmul(h, w_b, tm=1024, tn=w_b.shape[1], out_dtype=F32, name="in_proj_b").reshape(bsz, seq, -1)
        pc = _matmul(h, w_c, tm=1024, tn=1024, out_dtype=BF16, name="in_proj_c").reshape(bsz, seq, -1)

        o_a = _gdn(pa, pb, 2 * NSA_DKV // LANES, gdn_conv_w[i], gdn_a_log[i], gdn_dt_bias[i], gdn_norm_w[i], seq)
        ngate = pb[..., 2 * NSA_DKV + 2 * GDN_HEADS:2 * NSA_DKV + 2 * GDN_HEADS + 3 * NSA_HEADS]

        cmp = _compress(pb, jnp.stack([cmp_pos_k[i], cmp_pos_v[i]]), jnp.stack([cmp_w1_k[i], cmp_w1_v[i]]),
                        jnp.stack([cmp_w2_k[i], cmp_w2_v[i]]), seq)
        gates = ngate.reshape(bsz, seq, NSA_KV_HEADS, 3 * NSA_REP).transpose(0, 2, 1, 3)
        o_b = _nsa_attention(pa, 4 * GDN_DK, cmp, pc, gates, slopes, seq)

        x2 = _matmul_parts_residual([o_a.reshape(m, -1), o_b.reshape(m, -1)], w_out[i].astype(BF16), x2, g1, seq,
                                    tm=1024, tn=1024, name="out_proj")
        h2 = _norm_mod(x2, norm2_w[i], sc2, sh2, seq, BF16)
        hid = _matmul(h2, w_up[i].astype(BF16), tm=1024, tn=1024, out_dtype=BF16, relu2=True, name="mlp_up")
        x2 = _matmul_residual(hid, w_down[i].astype(BF16), x2, g2, seq,
                              tm=1024, tn=1024, tk=4096, name="mlp_down")
    return _final_norm(x2, final_norm_w).reshape(bsz, seq, d)
```

```python
import functools

import jax
import jax.numpy as jnp
from jax import lax
from jax.experimental import pallas as pl
from jax.experimental.pallas import tpu as pltpu

F32 = jnp.float32
BF16 = jnp.bfloat16
I32 = jnp.int32

HEAD_DIM = 128
GDN_HEADS = 16
NSA_HEADS = 16
NSA_KV_HEADS = 4
NSA_REP = NSA_HEADS // NSA_KV_HEADS
GDN_DK = GDN_HEADS * HEAD_DIM
NSA_DQ = NSA_HEADS * HEAD_DIM
NSA_DKV = NSA_KV_HEADS * HEAD_DIM
GDN_CONV = 4
GDN_CHUNK = 64
GDN_HEAD_GROUP = 16
CMP_BLOCK = 32
CMP_STRIDE = 16
SEL_BLOCK = 64
SEL_TOPN = 16
SEL_LOCAL = 2
WINDOW = 512
EPS = 1e-6
IN_SIZES = (GDN_DK, GDN_DK, GDN_DK, GDN_DK, GDN_HEADS, GDN_HEADS,
            NSA_DQ, NSA_DKV, NSA_DKV, NSA_DKV, NSA_DKV, NSA_DKV, NSA_DKV, 3 * NSA_HEADS)

VMEM_LIMIT_V7X = 56 * 1024 * 1024
LANES = 128
NSA_TQ = 256
NSA_TK = 1024
NEG_INF = float("-inf")
AUX_SEL_LANES = 64
SEL_BIAS = 2.0 ** 100
LOG2E = 1.4426950408889634
NT_DIMS = (((1,), (1,)), ((), ()))
TN_DIMS = (((0,), (0,)), ((), ()))


def _params(sem):
    return pltpu.CompilerParams(dimension_semantics=sem, vmem_limit_bytes=VMEM_LIMIT_V7X)


def _mod_kernel(c_ref, w_ref, b_ref, o_ref):
    c = c_ref[...]
    s = (c * jax.nn.sigmoid(c)).astype(BF16)
    o_ref[...] = jnp.dot(s, w_ref[...].astype(BF16), preferred_element_type=F32) + b_ref[...]


def _modulation(c, w, b):
    bsz, d = c.shape
    n = w.shape[1]
    rows = 8
    cp = jnp.zeros((rows, d), F32).at[:bsz].set(c)
    tn = 512
    out = pl.pallas_call(
        _mod_kernel,
        out_shape=jax.ShapeDtypeStruct((rows, n), F32),
        grid=(n // tn,),
        in_specs=[pl.BlockSpec((rows, d), lambda j: (0, 0)),
                  pl.BlockSpec((d, tn), lambda j: (0, j)),
                  pl.BlockSpec((1, tn), lambda j: (0, j))],
        out_specs=pl.BlockSpec((rows, tn), lambda j: (0, j)),
        compiler_params=_params(("parallel",)),
        name="adaln_mod",
    )(cp, w, b.reshape(1, n))
    return out[:bsz]


def _norm_kernel(x_ref, g_ref, sc_ref, sh_ref, o_ref):
    x = x_ref[...]
    y = x * lax.rsqrt(jnp.mean(x * x, axis=-1, keepdims=True) + EPS) * g_ref[...]
    o_ref[...] = (y * (1.0 + sc_ref[0]) + sh_ref[0]).astype(o_ref.dtype)


def _final_norm_kernel(x_ref, g_ref, o_ref):
    x = x_ref[...]
    o_ref[...] = x * lax.rsqrt(jnp.mean(x * x, axis=-1, keepdims=True) + EPS) * g_ref[...]


def _norm_mod(x2, gain, scale, shift, seq, out_dtype):
    m, d = x2.shape
    tm = 256
    per_b = seq // tm
    bsz = m // seq
    return pl.pallas_call(
        _norm_kernel,
        out_shape=jax.ShapeDtypeStruct((m, d), out_dtype),
        grid=(m // tm,),
        in_specs=[pl.BlockSpec((tm, d), lambda i: (i, 0)),
                  pl.BlockSpec((1, d), lambda i: (0, 0)),
                  pl.BlockSpec((1, 1, d), lambda i: (i // per_b, 0, 0)),
                  pl.BlockSpec((1, 1, d), lambda i: (i // per_b, 0, 0))],
        out_specs=pl.BlockSpec((tm, d), lambda i: (i, 0)),
        compiler_params=_params(("parallel",)),
        name="rmsnorm_mod",
    )(x2, gain.reshape(1, d), scale.reshape(bsz, 1, d), shift.reshape(bsz, 1, d))


def _final_norm(x2, gain):
    m, d = x2.shape
    tm = 256
    return pl.pallas_call(
        _final_norm_kernel,
        out_shape=jax.ShapeDtypeStruct((m, d), F32),
        grid=(m // tm,),
        in_specs=[pl.BlockSpec((tm, d), lambda i: (i, 0)),
                  pl.BlockSpec((1, d), lambda i: (0, 0))],
        out_specs=pl.BlockSpec((tm, d), lambda i: (i, 0)),
        compiler_params=_params(("parallel",)),
        name="final_rmsnorm",
    )(x2, gain.reshape(1, d))


def _mm_kernel(a_ref, w_ref, *rest, relu2, side_cast):
    if side_cast:
        side_ref, o_ref, side_o_ref = rest
        side_o_ref[...] = side_ref[...].astype(side_o_ref.dtype)
    else:
        (o_ref,) = rest
    acc = jnp.dot(a_ref[...], w_ref[...], preferred_element_type=F32)
    if relu2:
        acc = jnp.square(jnp.maximum(acc, 0.0))
    o_ref[...] = acc.astype(o_ref.dtype)


def _matmul(a, w, *, tm, tn, out_dtype, relu2=False, name, cast_to_bf16=None):
    m, k = a.shape
    n = w.shape[1]
    grid = (m // tm, n // tn)
    in_specs = [pl.BlockSpec((tm, k), lambda i, j: (i, 0)),
                pl.BlockSpec((k, tn), lambda i, j: (0, j))]
    out_shape = jax.ShapeDtypeStruct((m, n), out_dtype)
    out_specs = pl.BlockSpec((tm, tn), lambda i, j: (i, j))
    operands = (a, w)
    if cast_to_bf16 is not None:
        rows, cols = cast_to_bf16.shape
        steps = grid[0] * grid[1]
        blocks = min(steps, rows // 16)
        while rows % blocks or (rows // blocks) % 16:
            blocks -= 1
        side_map = lambda i, j: (jnp.minimum(i * grid[1] + j, blocks - 1), 0)
        in_specs.append(pl.BlockSpec((rows // blocks, cols), side_map))
        out_shape = (out_shape, jax.ShapeDtypeStruct((rows, cols), BF16))
        out_specs = (out_specs, pl.BlockSpec((rows // blocks, cols), side_map))
        operands += (cast_to_bf16,)
    return pl.pallas_call(
        functools.partial(_mm_kernel, relu2=relu2, side_cast=cast_to_bf16 is not None),
        out_shape=out_shape,
        grid=grid,
        in_specs=in_specs,
        out_specs=out_specs,
        compiler_params=_params(("arbitrary", "arbitrary") if cast_to_bf16 is not None
                                else ("parallel", "parallel")),
        name=name,
    )(*operands)


def _mm_res_kernel(a_ref, w_ref, x_ref, g_ref, o_ref):
    k = pl.program_id(2)

    @pl.when(k == 0)
    def _():
        o_ref[...] = jnp.zeros_like(o_ref)

    o_ref[...] += jnp.dot(a_ref[...], w_ref[...], preferred_element_type=F32)

    @pl.when(k == pl.num_programs(2) - 1)
    def _():
        o_ref[...] = x_ref[...] + g_ref[0] * o_ref[...]


def _matmul_residual(a, w, x2, gate, seq, *, tm, tn, tk, name):
    m, kdim = a.shape
    n = w.shape[1]
    bsz = m // seq
    per_b = seq // tm
    return pl.pallas_call(
        _mm_res_kernel,
        out_shape=jax.ShapeDtypeStruct((m, n), F32),
        grid=(m // tm, n // tn, kdim // tk),
        in_specs=[pl.BlockSpec((tm, tk), lambda i, j, k: (i, k)),
                  pl.BlockSpec((tk, tn), lambda i, j, k: (k, j)),
                  pl.BlockSpec((tm, tn), lambda i, j, k: (i, j)),
                  pl.BlockSpec((1, 1, tn), lambda i, j, k: (i // per_b, 0, j))],
        out_specs=pl.BlockSpec((tm, tn), lambda i, j, k: (i, j)),
        compiler_params=_params(("parallel", "parallel", "arbitrary")),
        name=name,
    )(a, w, x2, gate.reshape(bsz, 1, n))


def _mm_parts_res_kernel(*refs, nparts):
    a_refs = refs[:nparts]
    w_ref, x_ref, g_ref, o_ref = refs[nparts:]
    width = a_refs[0].shape[1]
    acc = jnp.dot(a_refs[0][...], w_ref[0:width, :], preferred_element_type=F32)
    for idx in range(1, nparts):
        acc += jnp.dot(a_refs[idx][...], w_ref[idx * width:(idx + 1) * width, :], preferred_element_type=F32)
    o_ref[...] = x_ref[...] + g_ref[0] * acc


def _matmul_parts_residual(a_parts, w, x2, gate, seq, *, tm, tn, name):
    nparts = len(a_parts)
    m, width = a_parts[0].shape
    assert all(a.shape == (m, width) for a in a_parts) and w.shape[0] == nparts * width
    n = w.shape[1]
    bsz = m // seq
    per_b = seq // tm
    return pl.pallas_call(
        functools.partial(_mm_parts_res_kernel, nparts=nparts),
        out_shape=jax.ShapeDtypeStruct((m, n), F32),
        grid=(m // tm, n // tn),
        in_specs=[pl.BlockSpec((tm, width), lambda i, j: (i, 0)) for _ in a_parts]
        + [pl.BlockSpec((nparts * width, tn), lambda i, j: (0, j)),
           pl.BlockSpec((tm, tn), lambda i, j: (i, j)),
           pl.BlockSpec((1, 1, tn), lambda i, j: (i // per_b, 0, j))],
        out_specs=pl.BlockSpec((tm, tn), lambda i, j: (i, j)),
        compiler_params=_params(("parallel", "parallel")),
        name=name,
    )(*a_parts, w, x2, gate.reshape(bsz, 1, n))


def _gdn_kernel(q_ref, k_ref, v_ref, z_ref, ba_ref, cw_ref, alog_ref, dtb_ref, nw_ref, o_ref,
                xbuf_ref, s_ref):
    c, d, nh = GDN_CHUNK, HEAD_DIM, GDN_HEADS
    width = nh * d
    hist = 8

    @pl.when(pl.program_id(1) == 0)
    def _():
        xbuf_ref[:, 0:hist, :] = jnp.zeros((3, hist, width), F32)
        s_ref[...] = jnp.zeros_like(s_ref)

    ba = ba_ref[...]
    beta = jax.nn.sigmoid(ba[:, 0:nh])
    g = -jnp.exp(alog_ref[...]) * jax.nn.softplus(ba[:, nh:2 * nh] + dtb_ref[...])
    r_i = lax.broadcasted_iota(I32, (c, c), 0)
    c_i = lax.broadcasted_iota(I32, (c, c), 1)
    causal = r_i >= c_i
    strict = r_i > c_i
    gam = jnp.dot(causal.astype(F32), g, precision=lax.Precision.HIGHEST, preferred_element_type=F32)
    gam_t = lax.dot_general(g, (r_i <= c_i).astype(F32), TN_DIMS, precision=lax.Precision.HIGHEST,
                            preferred_element_type=F32)
    g_last = gam[c - 1:c, :]
    egam = jnp.exp(gam)
    bexp = beta * egam
    kscale = jnp.exp(g_last - gam)
    eg_last = jnp.exp(g_last)
    nw = nw_ref[...]

    hsl = [slice(h * d, (h + 1) * d) for h in range(nh)]
    hcol = [slice(h, h + 1) for h in range(nh)]
    dot16 = lambda a, b: jnp.dot(a.astype(BF16), b.astype(BF16), preferred_element_type=F32)

    def split(a):
        hi = a.astype(BF16).astype(F32)
        return hi, a - hi

    as_lhs = lambda sp: jnp.concatenate([sp[0], sp[0], sp[1]], axis=1).astype(BF16)
    as_rhs = lambda sp: jnp.concatenate([sp[0], sp[1], sp[0]], axis=0).astype(BF16)
    dot32 = lambda lhs, rhs: jnp.dot(lhs, rhs, preferred_element_type=F32)

    base = 8
    same = lambda s: (r_i // s) == (c_i // s)
    eye = (r_i == c_i).astype(F32)

    for g0 in range(0, nh, GDN_HEAD_GROUP):
        heads = range(g0, g0 + GDN_HEAD_GROUP)
        cols = slice(g0 * d, (g0 + GDN_HEAD_GROUP) * d)
        conv = []
        for i, src in enumerate((q_ref, k_ref, v_ref)):
            wcols = slice(i * width + g0 * d, i * width + (g0 + GDN_HEAD_GROUP) * d)
            xbuf_ref[i, hist:hist + c, cols] = src[:, cols]
            acc = cw_ref[GDN_CONV - 1:GDN_CONV, wcols] * src[:, cols]
            for j in range(GDN_CONV - 1):
                shift = GDN_CONV - 1 - j
                acc += cw_ref[j:j + 1, wcols] * xbuf_ref[i, pl.ds(hist - shift, c), cols]
            conv.append(acc * jax.nn.sigmoid(acc))
            xbuf_ref[i, 0:hist, cols] = xbuf_ref[i, c:c + hist, cols]
        cq, ck, cv = conv
        gsl = {h: slice((h - g0) * d, (h - g0 + 1) * d) for h in heads}
        qn, kn, qk, lower = {}, {}, {}, {}
        for h in heads:
            qh = cq[:, gsl[h]]
            kh = ck[:, gsl[h]]
            qh = qh * lax.rsqrt(jnp.sum(qh * qh, axis=-1, keepdims=True) + EPS) * (d ** -0.5)
            kh = kh * lax.rsqrt(jnp.sum(kh * kh, axis=-1, keepdims=True) + EPS)
            kb = kh.astype(BF16)
            both = lax.dot_general(jnp.concatenate([qh.astype(BF16), kb], axis=0), kb, NT_DIMS,
                                   preferred_element_type=F32)
            diff = gam[:, hcol[h]] - gam_t[h:h + 1, :]
            dec = jnp.where(causal, jnp.exp(jnp.where(causal, diff, 0.0)), 0.0)
            qn[h] = qh
            kn[h] = kh
            qk[h] = both[:c] * dec
            lower[h] = jnp.where(strict, beta[:, hcol[h]] * both[c:] * dec, 0.0)

        diag = {h: jnp.where(same(base), lower[h], 0.0) for h in heads}
        dsp = {h: split(diag[h]) for h in heads}
        dlhs = {h: as_lhs(dsp[h]) for h in heads}
        p2 = {h: dot32(dlhs[h], as_rhs(dsp[h])) for h in heads}
        psp = {h: split(p2[h]) for h in heads}
        prhs = {h: as_rhs(psp[h]) for h in heads}
        qinv = {h: p2[h] - diag[h] - dot32(dlhs[h], prhs[h]) for h in heads}
        p4 = {h: dot32(as_lhs(psp[h]), prhs[h]) for h in heads}
        inv = {h: eye + (qinv[h] + p4[h] + dot32(as_lhs(split(qinv[h])), as_rhs(split(p4[h]))))
               for h in heads}
        size = base
        while size < c:
            off_mask = same(2 * size) & jnp.logical_not(same(size))
            t1 = {h: dot16(jnp.where(off_mask, lower[h], 0.0), inv[h]) for h in heads}
            inv = {h: inv[h] - dot16(inv[h], t1[h]) for h in heads}
            size *= 2

        sol = {}
        for h in heads:
            rhs = jnp.concatenate([cv[:, gsl[h]] * beta[:, hcol[h]], kn[h] * bexp[:, hcol[h]]], axis=1)
            sol[h] = rhs + dot16(inv[h] - eye, rhs)
        wq = {h: dot16(jnp.concatenate([sol[h][:, d:], qn[h] * egam[:, hcol[h]]], axis=0), s_ref[h])
              for h in heads}
        vnew = {h: (sol[h][:, :d] - wq[h][:c]).astype(BF16) for h in heads}
        for h in heads:
            k_dec = kn[h] * kscale[:, hcol[h]]
            s_ref[h] = s_ref[h] * eg_last[:, hcol[h]] + lax.dot_general(
                k_dec.astype(BF16), vnew[h], TN_DIMS, preferred_element_type=F32)
        for h in heads:
            o = wq[h][c:] + jnp.dot(qk[h].astype(BF16), vnew[h], preferred_element_type=F32)
            y = o * lax.rsqrt(jnp.mean(o * o, axis=-1, keepdims=True) + EPS) * nw
            zh = z_ref[:, hsl[h]]
            o_ref[:, hsl[h]] = (y * (zh * jax.nn.sigmoid(zh))).astype(o_ref.dtype)


def _gdn(pa, pb, ba_blk, conv_w, a_log, dt_bias, norm_w, seq):
    bsz = pa.shape[0]
    c, d, nh = GDN_CHUNK, HEAD_DIM, GDN_HEADS
    width = nh * d
    col_spec = lambda j: pl.BlockSpec((None, c, width), lambda b, n: (b, n, j))
    full = lambda shape: pl.BlockSpec(shape, lambda b, n: (0,) * len(shape))
    return pl.pallas_call(
        _gdn_kernel,
        out_shape=jax.ShapeDtypeStruct((bsz, seq, width), BF16),
        grid=(bsz, seq // c),
        in_specs=[col_spec(0), col_spec(1), col_spec(2), col_spec(3),
                  pl.BlockSpec((None, c, LANES), lambda b, n: (b, n, ba_blk)),
                  full((GDN_CONV, 3 * width)), full((1, nh)), full((1, nh)), full((1, d))],
        out_specs=pl.BlockSpec((None, c, width), lambda b, n: (b, n, 0)),
        scratch_shapes=[pltpu.VMEM((3, 8 + c, width), F32), pltpu.VMEM((nh, d, d), F32)],
        compiler_params=_params(("parallel", "arbitrary")),
        name="gdn_chunked",
    )(pa, pa, pa, pa, pb, conv_w, a_log.reshape(1, nh), dt_bias.reshape(1, nh), norm_w.reshape(1, d))


def _compress_kernel(x_ref, pos_ref, w1_ref, w2_ref, o_ref):
    n = o_ref.shape[0]
    d = x_ref.shape[1]
    acc_a = jnp.zeros((n, d), F32)
    acc_b = jnp.zeros((n, d), F32)
    for j in range(CMP_STRIDE):
        xj = x_ref[pl.ds(j, n, stride=CMP_STRIDE), :]
        lo = (xj + pos_ref[j:j + 1, :]).astype(BF16)
        hi = (xj + pos_ref[CMP_STRIDE + j:CMP_STRIDE + j + 1, :]).astype(BF16)
        acc_a += jnp.dot(lo, w1_ref[j * d:(j + 1) * d, :].astype(BF16), preferred_element_type=F32)
        acc_b += jnp.dot(hi, w1_ref[(CMP_STRIDE + j) * d:(CMP_STRIDE + j + 1) * d, :].astype(BF16),
                         preferred_element_type=F32)
    pre = acc_a + pltpu.roll(acc_b, n - 1, 0)
    hid = (pre * jax.nn.sigmoid(pre)).astype(BF16)
    out = jnp.dot(hid, w2_ref[...].astype(BF16), preferred_element_type=F32)
    row = lax.broadcasted_iota(I32, (n, d), 0)
    o_ref[...] = jnp.where(row < n - 1, out, 0.0).astype(o_ref.dtype)


def _compress(kvc, pos, w1, w2, seq):
    bsz = kvc.shape[0]
    g = NSA_KV_HEADS
    n = seq // CMP_STRIDE
    d = HEAD_DIM
    return pl.pallas_call(
        _compress_kernel,
        out_shape=jax.ShapeDtypeStruct((2, bsz, g, n, d), BF16),
        grid=(2, bsz, g),
        in_specs=[pl.BlockSpec((None, seq, d), lambda s, b, h: (b, 0, s * NSA_KV_HEADS + h)),
                  pl.BlockSpec((None, CMP_BLOCK, d), lambda s, b, h: (s, 0, 0)),
                  pl.BlockSpec((None, CMP_BLOCK * d, d), lambda s, b, h: (s, 0, 0)),
                  pl.BlockSpec((None, d, d), lambda s, b, h: (s, 0, 0))],
        out_specs=pl.BlockSpec((None, None, None, n, d), lambda s, b, h: (s, b, h, 0, 0)),
        compiler_params=_params(("parallel", "parallel", "parallel")),
        name="nsa_compress",
    )(kvc, pos, w1, w2)


def _softmax_rows(s, mask):
    s = jnp.where(mask, s, NEG_INF)
    m = jnp.max(s, axis=-1, keepdims=True)
    m = jnp.where(jnp.abs(m) < jnp.inf, m, 0.0)
    e = jnp.exp(s - m)
    d = jnp.sum(e, axis=-1, keepdims=True)
    return e * (1.0 / jnp.where(d > 0, d, 1.0))


def _nsa_key_aux(seq):
    key = jnp.arange(seq, dtype=I32)[:, None]
    lane = jnp.arange(LANES, dtype=I32)[None, :]
    onehot = (lane == key // SEL_BLOCK) & (lane < AUX_SEL_LANES)
    hi = (lane >= AUX_SEL_LANES) & (lane < AUX_SEL_LANES + 3)
    lo = (lane >= AUX_SEL_LANES + 3) & (lane < AUX_SEL_LANES + 6)
    aux = jnp.where(onehot, 1, 0) + jnp.where(hi, (key // 64) * 64, 0) + jnp.where(lo, key % 64, 0)
    return aux.astype(BF16)


def _nsa_query_aux(slopes):
    s1 = slopes.astype(BF16).astype(F32)
    s2 = (slopes - s1).astype(BF16).astype(F32)
    s3 = (slopes - s1 - s2).astype(BF16).astype(F32)
    pieces = jnp.stack([s1, s2, s3, s1, s2, s3], axis=-1)
    aux = jnp.zeros((slopes.shape[0], LANES), F32).at[:, AUX_SEL_LANES:AUX_SEL_LANES + 6].set(pieces)
    return aux.reshape(NSA_KV_HEADS, NSA_REP, LANES)


def _nsa_overlap(n_sel, n_cmp):
    lo = jnp.arange(n_sel, dtype=I32)[:, None] * SEL_BLOCK
    start = jnp.arange(n_cmp, dtype=I32)[None, :] * CMP_STRIDE
    return ((start <= lo + (SEL_BLOCK - 1)) & (start + (CMP_BLOCK - 1) >= lo)).astype(BF16)


def _nsa_kernel(slopes_ref, qaux_ref, kaux_ref, ov_ref, q_ref, kc_ref, vc_ref, ksel_ref, vs_ref, kwin_ref,
                vw_ref, gate_ref, o_ref, ks_ref, kw_ref):
    seq = ks_ref.shape[0]
    tq, tk, rep, d = NSA_TQ, NSA_TK, NSA_REP, HEAD_DIM
    rows = rep * tq
    g = pl.program_id(1)
    t0 = pl.program_id(2) * tq

    @pl.when(pl.program_id(2) == 0)
    def _():
        ks_ref[:, 0:d] = ksel_ref[...]
        ks_ref[:, d:2 * d] = kaux_ref[...]
        kw_ref[:, 0:d] = kwin_ref[...]
        kw_ref[:, d:2 * d] = kaux_ref[...]

    q = q_ref[...] * (d ** -0.5)
    q = jnp.concatenate([q[:, r * d:(r + 1) * d] for r in range(rep)], axis=0)
    qs = q.astype(BF16)
    qs2 = (q * LOG2E).astype(BF16)
    slope = jnp.concatenate([jnp.full((tq, 1), slopes_ref[g * rep + r], F32) for r in range(rep)], axis=0)
    row = lax.broadcasted_iota(I32, (rows, 1), 0)
    t_col = t0 + (row & (tq - 1))

    ncp = kc_ref.shape[0]
    n_idx = lax.broadcasted_iota(I32, (rows, ncp), 1)
    dist_c = t_col - (n_idx * CMP_STRIDE + (CMP_BLOCK - 1))
    s_c = lax.dot_general(qs, kc_ref[...], NT_DIMS, preferred_element_type=F32)
    s_c = s_c - slope * dist_c.astype(F32)
    p_c = _softmax_rows(s_c, dist_c >= 0).astype(BF16)
    o_c = jnp.dot(p_c, vc_ref[...], preferred_element_type=F32)

    n_sel = seq // SEL_BLOCK
    ov_t = ov_ref[...]
    imp = jnp.zeros((n_sel, tq), F32)
    for r in range(rep):
        imp += lax.dot_general(ov_t, p_c[r * tq:(r + 1) * tq, :], NT_DIMS, preferred_element_type=F32)
    blk = lax.broadcasted_iota(I32, (n_sel, tq), 0)
    cur = lax.shift_right_logical(t0 + lax.broadcasted_iota(I32, (n_sel, tq), 1), SEL_BLOCK.bit_length() - 1)
    forced = (blk == 0) | ((cur - blk) < SEL_LOCAL)
    imp = jnp.where(forced, jnp.inf, imp)
    imp = jnp.where(blk <= cur, imp, NEG_INF)
    rank = jnp.zeros((n_sel, tq), I32)
    for i in range(n_sel):
        ri = imp[i:i + 1, :]
        rank += ((ri > imp) | ((ri == imp) & (blk > i))).astype(I32)
    sel_t = (rank < min(SEL_TOPN, n_sel)).astype(F32)

    to_lanes = (lax.broadcasted_iota(I32, (n_sel, LANES), 0)
                == lax.broadcasted_iota(I32, (n_sel, LANES), 1)).astype(BF16)
    picked = lax.dot_general(sel_t.astype(BF16), to_lanes, TN_DIMS, preferred_element_type=F32)
    lane = lax.broadcasted_iota(I32, (tq, LANES), 1)
    sel_bias = jnp.where(lane < AUX_SEL_LANES, (picked - 1.0) * SEL_BIAS, 0.0)
    qaux = qaux_ref[...]
    qx_sel = jnp.concatenate(
        [qs2, jnp.concatenate([sel_bias + qaux[r:r + 1, :] for r in range(rep)], axis=0).astype(BF16)], axis=1)
    qx_win = jnp.concatenate(
        [qs2, jnp.concatenate([jnp.broadcast_to(qaux[r:r + 1, :], (tq, LANES)) for r in range(rep)],
                              axis=0).astype(BF16)], axis=1)

    def sel_tile(j, carry, diagonal):
        m, l, acc = carry
        k0 = pl.multiple_of(j * tk, tk)
        s = lax.dot_general(qx_sel, ks_ref[pl.ds(k0, tk), :], NT_DIMS, preferred_element_type=F32)
        if diagonal:
            ahead = (lax.broadcasted_iota(I32, (rows, tk), 1)
                     - (lax.broadcasted_iota(I32, (rows, tk), 0) & (tq - 1)))
            s = jnp.where(ahead <= t0 - k0, s, -SEL_BIAS)
        m_new = jnp.maximum(m, jnp.max(s, axis=-1, keepdims=True))
        alpha = jnp.exp2(m - m_new)
        p = jnp.exp2(s - m_new)
        l = alpha * l + jnp.sum(p, axis=-1, keepdims=True)
        acc = alpha * acc + jnp.dot(p.astype(BF16), vs_ref[pl.ds(k0, tk), :], preferred_element_type=F32)
        return m_new, l, acc

    n_tiles = (t0 + tq + tk - 1) // tk
    init = (jnp.full((rows, 1), NEG_INF, F32), jnp.zeros((rows, 1), F32), jnp.zeros((rows, d), F32))
    carry = lax.fori_loop(0, n_tiles - 1, lambda j, c: sel_tile(j, c, False), init)
    _, l_s, acc_s = sel_tile(n_tiles - 1, carry, True)
    o_s = acc_s * (1.0 / l_s)

    wk = tq + WINDOW
    start = pl.multiple_of(jnp.maximum(t0 - WINDOW, 0), tq)
    s_w = lax.dot_general(qx_win, kw_ref[pl.ds(start, wk), :], NT_DIMS, preferred_element_type=F32)
    dist_w = t_col - (start + lax.broadcasted_iota(I32, (rows, wk), 1))
    s_w = jnp.where((dist_w >= 0) & (dist_w < WINDOW), s_w, NEG_INF)
    e_w = jnp.exp2(s_w - jnp.max(s_w, axis=-1, keepdims=True))
    d_w = jnp.sum(e_w, axis=-1, keepdims=True)
    o_w = jnp.dot(e_w.astype(BF16), vw_ref[pl.ds(start, wk), :], preferred_element_type=F32) * (1.0 / d_w)

    gt = jax.nn.sigmoid(gate_ref[...])
    for r in range(rep):
        sl = slice(r * tq, (r + 1) * tq)
        o = (gt[:, 3 * r:3 * r + 1] * o_c[sl] + gt[:, 3 * r + 1:3 * r + 2] * o_s[sl]
             + gt[:, 3 * r + 2:3 * r + 3] * o_w[sl])
        o_ref[:, r * d:(r + 1) * d] = o.astype(o_ref.dtype)


def _nsa_attention(qa, q_col0, cmp, kvb, gates, slopes, seq):
    bsz = qa.shape[0]
    g, rep, d, tq = NSA_KV_HEADS, NSA_REP, HEAD_DIM, NSA_TQ
    ncp = seq // CMP_STRIDE
    n_sel = seq // SEL_BLOCK
    assert n_sel <= AUX_SEL_LANES and seq % NSA_TK == 0 and seq >= tq + WINDOW
    qblk0 = q_col0 // (rep * d)
    kv_spec = lambda s: pl.BlockSpec((None, seq, d), lambda b, h, i: (b, 0, s * NSA_KV_HEADS + h))
    cmp_spec = lambda s: pl.BlockSpec((None, None, None, ncp, d), lambda b, h, i: (s, b, h, 0, 0))
    return pl.pallas_call(
        _nsa_kernel,
        out_shape=jax.ShapeDtypeStruct((bsz, seq, NSA_DQ), BF16),
        grid=(bsz, g, seq // tq),
        in_specs=[pl.BlockSpec(memory_space=pltpu.SMEM),
                  pl.BlockSpec((None, rep, LANES), lambda b, h, i: (h, 0, 0)),
                  pl.BlockSpec((seq, LANES), lambda b, h, i: (0, 0)),
                  pl.BlockSpec((n_sel, ncp), lambda b, h, i: (0, 0)),
                  pl.BlockSpec((None, tq, rep * d), lambda b, h, i: (b, i, qblk0 + h)),
                  cmp_spec(0), cmp_spec(1), kv_spec(0), kv_spec(1), kv_spec(2), kv_spec(3),
                  pl.BlockSpec((None, None, tq, 3 * rep), lambda b, h, i: (b, h, i, 0))],
        out_specs=pl.BlockSpec((None, tq, rep * d), lambda b, h, i: (b, i, h)),
        scratch_shapes=[pltpu.VMEM((seq, 2 * d), BF16), pltpu.VMEM((seq, 2 * d), BF16)],
        compiler_params=_params(("parallel", "parallel", "arbitrary")),
        name="nsa_attention",
    )(slopes, _nsa_query_aux(slopes * LOG2E), _nsa_key_aux(seq), _nsa_overlap(n_sel, ncp), qa, cmp, cmp, kvb, kvb, kvb, kvb, gates)


def _split_columns(proj):
    out, start = [], 0
    for size in IN_SIZES:
        out.append(proj[..., start:start + size])
        start += size
    return out


def kernel(x, c, ada_w, ada_b, norm1_w, w_in, gdn_conv_w, gdn_a_log, gdn_dt_bias, gdn_norm_w,
           cmp_pos_k, cmp_w1_k, cmp_w2_k, cmp_pos_v, cmp_w1_v, cmp_w2_v, w_out, norm2_w,
           w_up, w_down, final_norm_w):
    bsz, seq, d = x.shape
    depth = ada_w.shape[0]
    m = bsz * seq
    x2 = x.reshape(m, d)
    slopes = 2.0 ** (-8.0 * jnp.arange(1, NSA_HEADS + 1, dtype=F32) / NSA_HEADS)
    for i in range(depth):
        mod = _modulation(c, ada_w[i], ada_b[i])
        sh1, sc1, g1, sh2, sc2, g2 = jnp.split(mod, 6, axis=-1)
        h = _norm_mod(x2, norm1_w[i], sc1, sh1, seq, BF16)
        cols = _split_columns(w_in[i])
        small = jnp.concatenate([cols[4], cols[5], cols[13]], axis=-1)
        small = jnp.pad(small, ((0, 0), (0, LANES - small.shape[1])))
        w_a = jnp.concatenate(cols[0:4] + [cols[6]], axis=-1).astype(BF16)
        w_b = jnp.concatenate(cols[7:9] + [small], axis=-1).astype(BF16)
        w_c = jnp.concatenate(cols[9:13], axis=-1).astype(BF16)
        pa, w_up_bf16 = _matmul(h, w_a, tm=1024, tn=1024, out_dtype=F32, name="in_proj_a", cast_to_bf16=w_up[i])
        pa = pa.reshape(bsz, seq, -1)
        pb = _matmul(h, w_b, tm=1024, tn=w_b.shape[1], out_dtype=F32, name="in_proj_b").reshape(bsz, seq, -1)
        pc = _matmul(h, w_c, tm=1024, tn=1024, out_dtype=BF16, name="in_proj_c").reshape(bsz, seq, -1)

        o_a = _gdn(pa, pb, 2 * NSA_DKV // LANES, gdn_conv_w[i], gdn_a_log[i], gdn_dt_bias[i], gdn_norm_w[i], seq)
        ngate = pb[..., 2 * NSA_DKV + 2 * GDN_HEADS:2 * NSA_DKV + 2 * GDN_HEADS + 3 * NSA_HEADS]

        cmp = _compress(pb, jnp.stack([cmp_pos_k[i], cmp_pos_v[i]]), jnp.stack([cmp_w1_k[i], cmp_w1_v[i]]),
                        jnp.stack([cmp_w2_k[i], cmp_w2_v[i]]), seq)
        gates = ngate.reshape(bsz, seq, NSA_KV_HEADS, 3 * NSA_REP).transpose(0, 2, 1, 3)
        o_b = _nsa_attention(pa, 4 * GDN_DK, cmp, pc, gates, slopes, seq)

        x2 = _matmul_parts_residual([o_a.reshape(m, -1), o_b.reshape(m, -1)], w_out[i].astype(BF16), x2, g1, seq,
                                    tm=1024, tn=1024, name="out_proj")
        h2 = _norm_mod(x2, norm2_w[i], sc2, sh2, seq, BF16)
        hid, w_down_bf16 = _matmul(h2, w_up_bf16, tm=1024, tn=1024, out_dtype=BF16, relu2=True, name="mlp_up",
                                   cast_to_bf16=w_down[i])
        x2 = _matmul_residual(hid, w_down_bf16, x2, g2, seq,
                              tm=1024, tn=1024, tk=4096, name="mlp_down")
    return _final_norm(x2, final_norm_w).reshape(bsz, seq, d)
```

```python
import functools

import jax
import jax.numpy as jnp
from jax import lax
from jax.experimental import pallas as pl
from jax.experimental.pallas import tpu as pltpu

F32 = jnp.float32
BF16 = jnp.bfloat16
I32 = jnp.int32

HEAD_DIM = 128
GDN_HEADS = 16
NSA_HEADS = 16
NSA_KV_HEADS = 4
NSA_REP = NSA_HEADS // NSA_KV_HEADS
GDN_DK = GDN_HEADS * HEAD_DIM
NSA_DQ = NSA_HEADS * HEAD_DIM
NSA_DKV = NSA_KV_HEADS * HEAD_DIM
GDN_CONV = 4
GDN_CHUNK = 128
GDN_HEAD_GROUP = 16
CMP_BLOCK = 32
CMP_STRIDE = 16
SEL_BLOCK = 64
SEL_TOPN = 16
SEL_LOCAL = 2
WINDOW = 512
EPS = 1e-6
IN_SIZES = (GDN_DK, GDN_DK, GDN_DK, GDN_DK, GDN_HEADS, GDN_HEADS,
            NSA_DQ, NSA_DKV, NSA_DKV, NSA_DKV, NSA_DKV, NSA_DKV, NSA_DKV, 3 * NSA_HEADS)

VMEM_LIMIT_V7X = 56 * 1024 * 1024
LANES = 128
SUBLANES = 8
BF16_SUBLANES = 16
MM_TILE = 1024
MM_TILE_K = 4096
NORM_ROWS = 256
MOD_TILE_N = 512
NSA_TQ = 256
NSA_TK = 1024
NEG_INF = float("-inf")
AUX_SEL_LANES = 64
SEL_BIAS = 2.0 ** 100
LOG2E = 1.4426950408889634
NT_DIMS = (((1,), (1,)), ((), ()))
TN_DIMS = (((0,), (0,)), ((), ()))


def _params(sem):
    return pltpu.CompilerParams(dimension_semantics=sem, vmem_limit_bytes=VMEM_LIMIT_V7X)


def _mod_kernel(c_ref, w_ref, b_ref, o_ref):
    c = c_ref[...]
    s = (c * jax.nn.sigmoid(c)).astype(BF16)
    o_ref[...] = jnp.dot(s, w_ref[...].astype(BF16), preferred_element_type=F32) + b_ref[...]


def _modulation(c, w, b):
    bsz, d = c.shape
    n = w.shape[1]
    rows = SUBLANES
    cp = jnp.zeros((rows, d), F32).at[:bsz].set(c)
    tn = MOD_TILE_N
    out = pl.pallas_call(
        _mod_kernel,
        out_shape=jax.ShapeDtypeStruct((rows, n), F32),
        grid=(n // tn,),
        in_specs=[pl.BlockSpec((rows, d), lambda j: (0, 0)),
                  pl.BlockSpec((d, tn), lambda j: (0, j)),
                  pl.BlockSpec((1, tn), lambda j: (0, j))],
        out_specs=pl.BlockSpec((rows, tn), lambda j: (0, j)),
        compiler_params=_params(("parallel",)),
        name="adaln_mod",
    )(cp, w, b.reshape(1, n))
    return out[:bsz]


def _norm_kernel(x_ref, g_ref, sc_ref, sh_ref, o_ref):
    x = x_ref[...]
    y = x * lax.rsqrt(jnp.mean(x * x, axis=-1, keepdims=True) + EPS) * g_ref[...]
    o_ref[...] = (y * (1.0 + sc_ref[0]) + sh_ref[0]).astype(o_ref.dtype)


def _final_norm_kernel(x_ref, g_ref, o_ref):
    x = x_ref[...]
    o_ref[...] = x * lax.rsqrt(jnp.mean(x * x, axis=-1, keepdims=True) + EPS) * g_ref[...]


def _norm_mod(x2, gain, scale, shift, seq, out_dtype):
    m, d = x2.shape
    tm = NORM_ROWS
    per_b = seq // tm
    bsz = m // seq
    return pl.pallas_call(
        _norm_kernel,
        out_shape=jax.ShapeDtypeStruct((m, d), out_dtype),
        grid=(m // tm,),
        in_specs=[pl.BlockSpec((tm, d), lambda i: (i, 0)),
                  pl.BlockSpec((1, d), lambda i: (0, 0)),
                  pl.BlockSpec((1, 1, d), lambda i: (i // per_b, 0, 0)),
                  pl.BlockSpec((1, 1, d), lambda i: (i // per_b, 0, 0))],
        out_specs=pl.BlockSpec((tm, d), lambda i: (i, 0)),
        compiler_params=_params(("parallel",)),
        name="rmsnorm_mod",
    )(x2, gain.reshape(1, d), scale.reshape(bsz, 1, d), shift.reshape(bsz, 1, d))


def _final_norm(x2, gain):
    m, d = x2.shape
    tm = NORM_ROWS
    return pl.pallas_call(
        _final_norm_kernel,
        out_shape=jax.ShapeDtypeStruct((m, d), F32),
        grid=(m // tm,),
        in_specs=[pl.BlockSpec((tm, d), lambda i: (i, 0)),
                  pl.BlockSpec((1, d), lambda i: (0, 0))],
        out_specs=pl.BlockSpec((tm, d), lambda i: (i, 0)),
        compiler_params=_params(("parallel",)),
        name="final_rmsnorm",
    )(x2, gain.reshape(1, d))


def _mm_kernel(a_ref, w_ref, *rest, relu2, side_cast):
    if side_cast:
        side_ref, o_ref, side_o_ref = rest
        side_o_ref[...] = side_ref[...].astype(side_o_ref.dtype)
    else:
        (o_ref,) = rest
    acc = jnp.dot(a_ref[...], w_ref[...], preferred_element_type=F32)
    if relu2:
        acc = jnp.square(jnp.maximum(acc, 0.0))
    o_ref[...] = acc.astype(o_ref.dtype)


def _matmul(a, w, *, tm, tn, out_dtype, relu2=False, name, cast_to_bf16=None):
    m, k = a.shape
    n = w.shape[1]
    grid = (m // tm, n // tn)
    in_specs = [pl.BlockSpec((tm, k), lambda i, j: (i, 0)),
                pl.BlockSpec((k, tn), lambda i, j: (0, j))]
    out_shape = jax.ShapeDtypeStruct((m, n), out_dtype)
    out_specs = pl.BlockSpec((tm, tn), lambda i, j: (i, j))
    operands = (a, w)
    if cast_to_bf16 is not None:
        rows, cols = cast_to_bf16.shape
        steps = grid[0] * grid[1]
        blocks = min(steps, rows // BF16_SUBLANES)
        while rows % blocks or (rows // blocks) % BF16_SUBLANES:
            blocks -= 1
        side_map = lambda i, j: (jnp.minimum(i * grid[1] + j, blocks - 1), 0)
        in_specs.append(pl.BlockSpec((rows // blocks, cols), side_map))
        out_shape = (out_shape, jax.ShapeDtypeStruct((rows, cols), BF16))
        out_specs = (out_specs, pl.BlockSpec((rows // blocks, cols), side_map))
        operands += (cast_to_bf16,)
    return pl.pallas_call(
        functools.partial(_mm_kernel, relu2=relu2, side_cast=cast_to_bf16 is not None),
        out_shape=out_shape,
        grid=grid,
        in_specs=in_specs,
        out_specs=out_specs,
        compiler_params=_params(("arbitrary", "arbitrary") if cast_to_bf16 is not None
                                else ("parallel", "parallel")),
        name=name,
    )(*operands)


def _mm_res_kernel(a_ref, w_ref, x_ref, g_ref, o_ref):
    k = pl.program_id(2)

    @pl.when(k == 0)
    def _():
        o_ref[...] = jnp.zeros_like(o_ref)

    o_ref[...] += jnp.dot(a_ref[...], w_ref[...], preferred_element_type=F32)

    @pl.when(k == pl.num_programs(2) - 1)
    def _():
        o_ref[...] = x_ref[...] + g_ref[0] * o_ref[...]


def _matmul_residual(a, w, x2, gate, seq, *, tm, tn, tk, name):
    m, kdim = a.shape
    n = w.shape[1]
    bsz = m // seq
    per_b = seq // tm
    return pl.pallas_call(
        _mm_res_kernel,
        out_shape=jax.ShapeDtypeStruct((m, n), F32),
        grid=(m // tm, n // tn, kdim // tk),
        in_specs=[pl.BlockSpec((tm, tk), lambda i, j, k: (i, k)),
                  pl.BlockSpec((tk, tn), lambda i, j, k: (k, j)),
                  pl.BlockSpec((tm, tn), lambda i, j, k: (i, j)),
                  pl.BlockSpec((1, 1, tn), lambda i, j, k: (i // per_b, 0, j))],
        out_specs=pl.BlockSpec((tm, tn), lambda i, j, k: (i, j)),
        compiler_params=_params(("parallel", "parallel", "arbitrary")),
        name=name,
    )(a, w, x2, gate.reshape(bsz, 1, n))


def _mm_parts_res_kernel(*refs, nparts):
    a_refs = refs[:nparts]
    w_ref, x_ref, g_ref, o_ref = refs[nparts:]
    width = a_refs[0].shape[1]
    acc = jnp.dot(a_refs[0][...], w_ref[0:width, :], preferred_element_type=F32)
    for idx in range(1, nparts):
        acc += jnp.dot(a_refs[idx][...], w_ref[idx * width:(idx + 1) * width, :], preferred_element_type=F32)
    o_ref[...] = x_ref[...] + g_ref[0] * acc


def _matmul_parts_residual(a_parts, w, x2, gate, seq, *, tm, tn, name):
    nparts = len(a_parts)
    m, width = a_parts[0].shape
    assert all(a.shape == (m, width) for a in a_parts) and w.shape[0] == nparts * width
    n = w.shape[1]
    bsz = m // seq
    per_b = seq // tm
    return pl.pallas_call(
        functools.partial(_mm_parts_res_kernel, nparts=nparts),
        out_shape=jax.ShapeDtypeStruct((m, n), F32),
        grid=(m // tm, n // tn),
        in_specs=[pl.BlockSpec((tm, width), lambda i, j: (i, 0)) for _ in a_parts]
        + [pl.BlockSpec((nparts * width, tn), lambda i, j: (0, j)),
           pl.BlockSpec((tm, tn), lambda i, j: (i, j)),
           pl.BlockSpec((1, 1, tn), lambda i, j: (i // per_b, 0, j))],
        out_specs=pl.BlockSpec((tm, tn), lambda i, j: (i, j)),
        compiler_params=_params(("parallel", "parallel")),
        name=name,
    )(*a_parts, w, x2, gate.reshape(bsz, 1, n))


def _gdn_kernel(q_ref, k_ref, v_ref, z_ref, ba_ref, cw_ref, alog_ref, dtb_ref, nw_ref, o_ref,
                xbuf_ref, s_ref):
    c, d, nh = GDN_CHUNK, HEAD_DIM, GDN_HEADS
    width = nh * d
    hist = SUBLANES

    @pl.when(pl.program_id(1) == 0)
    def _():
        xbuf_ref[:, 0:hist, :] = jnp.zeros((3, hist, width), F32)
        s_ref[...] = jnp.zeros_like(s_ref)

    ba = ba_ref[...]
    beta = jax.nn.sigmoid(ba[:, 0:nh])
    g = -jnp.exp(alog_ref[...]) * jax.nn.softplus(ba[:, nh:2 * nh] + dtb_ref[...])
    r_i = lax.broadcasted_iota(I32, (c, c), 0)
    c_i = lax.broadcasted_iota(I32, (c, c), 1)
    causal = r_i >= c_i
    strict = r_i > c_i
    gam = jnp.dot(causal.astype(F32), g, precision=lax.Precision.HIGHEST, preferred_element_type=F32)
    gam_t = lax.dot_general(g, (r_i <= c_i).astype(F32), TN_DIMS, precision=lax.Precision.HIGHEST,
                            preferred_element_type=F32)
    g_last = gam[c - 1:c, :]
    egam = jnp.exp(gam)
    bexp = beta * egam
    kscale = jnp.exp(g_last - gam)
    eg_last = jnp.exp(g_last)
    nw = nw_ref[...]

    hsl = [slice(h * d, (h + 1) * d) for h in range(nh)]
    hcol = [slice(h, h + 1) for h in range(nh)]
    dot16 = lambda a, b: jnp.dot(a.astype(BF16), b.astype(BF16), preferred_element_type=F32)

    def split(a):
        hi = a.astype(BF16).astype(F32)
        return hi, a - hi

    as_lhs = lambda sp: jnp.concatenate([sp[0], sp[0], sp[1]], axis=1).astype(BF16)
    as_rhs = lambda sp: jnp.concatenate([sp[0], sp[1], sp[0]], axis=0).astype(BF16)
    dot32 = lambda lhs, rhs: jnp.dot(lhs, rhs, preferred_element_type=F32)

    base = 8
    same = lambda s: (r_i // s) == (c_i // s)
    eye = (r_i == c_i).astype(F32)

    for g0 in range(0, nh, GDN_HEAD_GROUP):
        heads = range(g0, g0 + GDN_HEAD_GROUP)
        cols = slice(g0 * d, (g0 + GDN_HEAD_GROUP) * d)
        conv = []
        for i, src in enumerate((q_ref, k_ref, v_ref)):
            wcols = slice(i * width + g0 * d, i * width + (g0 + GDN_HEAD_GROUP) * d)
            xbuf_ref[i, hist:hist + c, cols] = src[:, cols]
            acc = cw_ref[GDN_CONV - 1:GDN_CONV, wcols] * src[:, cols]
            for j in range(GDN_CONV - 1):
                shift = GDN_CONV - 1 - j
                acc += cw_ref[j:j + 1, wcols] * xbuf_ref[i, pl.ds(hist - shift, c), cols]
            conv.append(acc * jax.nn.sigmoid(acc))
            xbuf_ref[i, 0:hist, cols] = xbuf_ref[i, c:c + hist, cols]
        cq, ck, cv = conv
        gsl = {h: slice((h - g0) * d, (h - g0 + 1) * d) for h in heads}
        qn, kn, qk, lower = {}, {}, {}, {}
        for h in heads:
            qh = cq[:, gsl[h]]
            kh = ck[:, gsl[h]]
            qh = qh * lax.rsqrt(jnp.sum(qh * qh, axis=-1, keepdims=True) + EPS) * (d ** -0.5)
            kh = kh * lax.rsqrt(jnp.sum(kh * kh, axis=-1, keepdims=True) + EPS)
            kb = kh.astype(BF16)
            both = lax.dot_general(jnp.concatenate([qh.astype(BF16), kb], axis=0), kb, NT_DIMS,
                                   preferred_element_type=F32)
            diff = gam[:, hcol[h]] - gam_t[h:h + 1, :]
            dec = jnp.where(causal, jnp.exp(jnp.where(causal, diff, 0.0)), 0.0)
            qn[h] = qh
            kn[h] = kh
            qk[h] = both[:c] * dec
            lower[h] = jnp.where(strict, beta[:, hcol[h]] * both[c:] * dec, 0.0)

        diag = {h: jnp.where(same(base), lower[h], 0.0) for h in heads}
        dsp = {h: split(diag[h]) for h in heads}
        dlhs = {h: as_lhs(dsp[h]) for h in heads}
        p2 = {h: dot32(dlhs[h], as_rhs(dsp[h])) for h in heads}
        psp = {h: split(p2[h]) for h in heads}
        prhs = {h: as_rhs(psp[h]) for h in heads}
        qinv = {h: p2[h] - diag[h] - dot32(dlhs[h], prhs[h]) for h in heads}
        p4 = {h: dot32(as_lhs(psp[h]), prhs[h]) for h in heads}
        inv = {h: eye + (qinv[h] + p4[h] + dot32(as_lhs(split(qinv[h])), as_rhs(split(p4[h]))))
               for h in heads}
        size = base
        while size < c:
            off_mask = same(2 * size) & jnp.logical_not(same(size))
            t1 = {h: dot16(jnp.where(off_mask, lower[h], 0.0), inv[h]) for h in heads}
            inv = {h: inv[h] - dot16(inv[h], t1[h]) for h in heads}
            size *= 2

        sol = {}
        for h in heads:
            rhs = jnp.concatenate([cv[:, gsl[h]] * beta[:, hcol[h]], kn[h] * bexp[:, hcol[h]]], axis=1)
            sol[h] = rhs + dot16(inv[h] - eye, rhs)
        wq = {h: dot16(jnp.concatenate([sol[h][:, d:], qn[h] * egam[:, hcol[h]]], axis=0), s_ref[h])
              for h in heads}
        vnew = {h: (sol[h][:, :d] - wq[h][:c]).astype(BF16) for h in heads}
        for h in heads:
            k_dec = kn[h] * kscale[:, hcol[h]]
            s_ref[h] = s_ref[h] * eg_last[:, hcol[h]] + lax.dot_general(
                k_dec.astype(BF16), vnew[h], TN_DIMS, preferred_element_type=F32)
        for h in heads:
            o = wq[h][c:] + jnp.dot(qk[h].astype(BF16), vnew[h], preferred_element_type=F32)
            y = o * lax.rsqrt(jnp.mean(o * o, axis=-1, keepdims=True) + EPS) * nw
            zh = z_ref[:, hsl[h]]
            o_ref[:, hsl[h]] = (y * (zh * jax.nn.sigmoid(zh))).astype(o_ref.dtype)


def _gdn(pa, pb, ba_blk, conv_w, a_log, dt_bias, norm_w, seq):
    bsz = pa.shape[0]
    c, d, nh = GDN_CHUNK, HEAD_DIM, GDN_HEADS
    width = nh * d
    col_spec = lambda j: pl.BlockSpec((None, c, width), lambda b, n: (b, n, j))
    full = lambda shape: pl.BlockSpec(shape, lambda b, n: (0,) * len(shape))
    return pl.pallas_call(
        _gdn_kernel,
        out_shape=jax.ShapeDtypeStruct((bsz, seq, width), BF16),
        grid=(bsz, seq // c),
        in_specs=[col_spec(0), col_spec(1), col_spec(2), col_spec(3),
                  pl.BlockSpec((None, c, LANES), lambda b, n: (b, n, ba_blk)),
                  full((GDN_CONV, 3 * width)), full((1, nh)), full((1, nh)), full((1, d))],
        out_specs=pl.BlockSpec((None, c, width), lambda b, n: (b, n, 0)),
        scratch_shapes=[pltpu.VMEM((3, 8 + c, width), F32), pltpu.VMEM((nh, d, d), F32)],
        compiler_params=_params(("parallel", "arbitrary")),
        name="gdn_chunked",
    )(pa, pa, pa, pa, pb, conv_w, a_log.reshape(1, nh), dt_bias.reshape(1, nh), norm_w.reshape(1, d))


def _compress_kernel(x_ref, pos_ref, w1_ref, w2_ref, o_ref):
    n = o_ref.shape[0]
    d = x_ref.shape[1]
    acc_a = jnp.zeros((n, d), F32)
    acc_b = jnp.zeros((n, d), F32)
    for j in range(CMP_STRIDE):
        xj = x_ref[pl.ds(j, n, stride=CMP_STRIDE), :]
        lo = (xj + pos_ref[j:j + 1, :]).astype(BF16)
        hi = (xj + pos_ref[CMP_STRIDE + j:CMP_STRIDE + j + 1, :]).astype(BF16)
        acc_a += jnp.dot(lo, w1_ref[j * d:(j + 1) * d, :].astype(BF16), preferred_element_type=F32)
        acc_b += jnp.dot(hi, w1_ref[(CMP_STRIDE + j) * d:(CMP_STRIDE + j + 1) * d, :].astype(BF16),
                         preferred_element_type=F32)
    pre = acc_a + pltpu.roll(acc_b, n - 1, 0)
    hid = (pre * jax.nn.sigmoid(pre)).astype(BF16)
    out = jnp.dot(hid, w2_ref[...].astype(BF16), preferred_element_type=F32)
    row = lax.broadcasted_iota(I32, (n, d), 0)
    o_ref[...] = jnp.where(row < n - 1, out, 0.0).astype(o_ref.dtype)


def _compress(kvc, pos, w1, w2, seq):
    bsz = kvc.shape[0]
    g = NSA_KV_HEADS
    n = seq // CMP_STRIDE
    d = HEAD_DIM
    return pl.pallas_call(
        _compress_kernel,
        out_shape=jax.ShapeDtypeStruct((2, bsz, g, n, d), BF16),
        grid=(2, bsz, g),
        in_specs=[pl.BlockSpec((None, seq, d), lambda s, b, h: (b, 0, s * NSA_KV_HEADS + h)),
                  pl.BlockSpec((None, CMP_BLOCK, d), lambda s, b, h: (s, 0, 0)),
                  pl.BlockSpec((None, CMP_BLOCK * d, d), lambda s, b, h: (s, 0, 0)),
                  pl.BlockSpec((None, d, d), lambda s, b, h: (s, 0, 0))],
        out_specs=pl.BlockSpec((None, None, None, n, d), lambda s, b, h: (s, b, h, 0, 0)),
        compiler_params=_params(("parallel", "parallel", "parallel")),
        name="nsa_compress",
    )(kvc, pos, w1, w2)


def _softmax_rows(s, mask):
    s = jnp.where(mask, s, NEG_INF)
    m = jnp.max(s, axis=-1, keepdims=True)
    m = jnp.where(jnp.abs(m) < jnp.inf, m, 0.0)
    e = jnp.exp(s - m)
    d = jnp.sum(e, axis=-1, keepdims=True)
    return e * (1.0 / jnp.where(d > 0, d, 1.0))


def _nsa_key_aux(seq):
    key = jnp.arange(seq, dtype=I32)[:, None]
    lane = jnp.arange(LANES, dtype=I32)[None, :]
    onehot = (lane == key // SEL_BLOCK) & (lane < AUX_SEL_LANES)
    hi = (lane >= AUX_SEL_LANES) & (lane < AUX_SEL_LANES + 3)
    lo = (lane >= AUX_SEL_LANES + 3) & (lane < AUX_SEL_LANES + 6)
    aux = jnp.where(onehot, 1, 0) + jnp.where(hi, (key // 64) * 64, 0) + jnp.where(lo, key % 64, 0)
    return aux.astype(BF16)


def _nsa_query_aux(slopes):
    s1 = slopes.astype(BF16).astype(F32)
    s2 = (slopes - s1).astype(BF16).astype(F32)
    s3 = (slopes - s1 - s2).astype(BF16).astype(F32)
    pieces = jnp.stack([s1, s2, s3, s1, s2, s3], axis=-1)
    aux = jnp.zeros((slopes.shape[0], LANES), F32).at[:, AUX_SEL_LANES:AUX_SEL_LANES + 6].set(pieces)
    return aux.reshape(NSA_KV_HEADS, NSA_REP, LANES)


def _nsa_overlap(n_sel, n_cmp):
    lo = jnp.arange(n_sel, dtype=I32)[:, None] * SEL_BLOCK
    start = jnp.arange(n_cmp, dtype=I32)[None, :] * CMP_STRIDE
    return ((start <= lo + (SEL_BLOCK - 1)) & (start + (CMP_BLOCK - 1) >= lo)).astype(BF16)


def _nsa_kernel(slopes_ref, qaux_ref, kaux_ref, ov_ref, q_ref, kc_ref, vc_ref, ksel_ref, vs_ref, kwin_ref,
                vw_ref, gate_ref, o_ref, ks_ref, kw_ref):
    seq = ks_ref.shape[0]
    tq, tk, rep, d = NSA_TQ, NSA_TK, NSA_REP, HEAD_DIM
    rows = rep * tq
    g = pl.program_id(1)
    t0 = pl.program_id(2) * tq

    @pl.when(pl.program_id(2) == 0)
    def _():
        ks_ref[:, 0:d] = ksel_ref[...]
        ks_ref[:, d:2 * d] = kaux_ref[...]
        kw_ref[:, 0:d] = kwin_ref[...]
        kw_ref[:, d:2 * d] = kaux_ref[...]

    q = q_ref[...] * (d ** -0.5)
    q = jnp.concatenate([q[:, r * d:(r + 1) * d] for r in range(rep)], axis=0)
    qs = q.astype(BF16)
    qs2 = (q * LOG2E).astype(BF16)
    slope = jnp.concatenate([jnp.full((tq, 1), slopes_ref[g * rep + r], F32) for r in range(rep)], axis=0)
    row = lax.broadcasted_iota(I32, (rows, 1), 0)
    t_col = t0 + (row & (tq - 1))

    ncp = kc_ref.shape[0]
    n_idx = lax.broadcasted_iota(I32, (rows, ncp), 1)
    dist_c = t_col - (n_idx * CMP_STRIDE + (CMP_BLOCK - 1))
    s_c = lax.dot_general(qs, kc_ref[...], NT_DIMS, preferred_element_type=F32)
    s_c = s_c - slope * dist_c.astype(F32)
    p_c = _softmax_rows(s_c, dist_c >= 0).astype(BF16)
    o_c = jnp.dot(p_c, vc_ref[...], preferred_element_type=F32)

    n_sel = seq // SEL_BLOCK
    ov_t = ov_ref[...]
    imp = jnp.zeros((n_sel, tq), F32)
    for r in range(rep):
        imp += lax.dot_general(ov_t, p_c[r * tq:(r + 1) * tq, :], NT_DIMS, preferred_element_type=F32)
    blk = lax.broadcasted_iota(I32, (n_sel, tq), 0)
    cur = lax.shift_right_logical(t0 + lax.broadcasted_iota(I32, (n_sel, tq), 1), SEL_BLOCK.bit_length() - 1)
    forced = (blk == 0) | ((cur - blk) < SEL_LOCAL)
    imp = jnp.where(forced, jnp.inf, imp)
    imp = jnp.where(blk <= cur, imp, NEG_INF)
    rank = jnp.zeros((n_sel, tq), I32)
    for i in range(n_sel):
        ri = imp[i:i + 1, :]
        rank += ((ri > imp) | ((ri == imp) & (blk > i))).astype(I32)
    sel_t = (rank < min(SEL_TOPN, n_sel)).astype(F32)

    to_lanes = (lax.broadcasted_iota(I32, (n_sel, LANES), 0)
                == lax.broadcasted_iota(I32, (n_sel, LANES), 1)).astype(BF16)
    picked = lax.dot_general(sel_t.astype(BF16), to_lanes, TN_DIMS, preferred_element_type=F32)
    lane = lax.broadcasted_iota(I32, (tq, LANES), 1)
    sel_bias = jnp.where(lane < AUX_SEL_LANES, (picked - 1.0) * SEL_BIAS, 0.0)
    qaux = qaux_ref[...]
    qx_sel = jnp.concatenate(
        [qs2, jnp.concatenate([sel_bias + qaux[r:r + 1, :] for r in range(rep)], axis=0).astype(BF16)], axis=1)
    qx_win = jnp.concatenate(
        [qs2, jnp.concatenate([jnp.broadcast_to(qaux[r:r + 1, :], (tq, LANES)) for r in range(rep)],
                              axis=0).astype(BF16)], axis=1)

    def sel_tile(j, carry, diagonal):
        m, l, acc = carry
        k0 = pl.multiple_of(j * tk, tk)
        s = lax.dot_general(qx_sel, ks_ref[pl.ds(k0, tk), :], NT_DIMS, preferred_element_type=F32)
        if diagonal:
            ahead = (lax.broadcasted_iota(I32, (rows, tk), 1)
                     - (lax.broadcasted_iota(I32, (rows, tk), 0) & (tq - 1)))
            s = jnp.where(ahead <= t0 - k0, s, -SEL_BIAS)
        m_new = jnp.maximum(m, jnp.max(s, axis=-1, keepdims=True))
        alpha = jnp.exp2(m - m_new)
        p = jnp.exp2(s - m_new)
        l = alpha * l + jnp.sum(p, axis=-1, keepdims=True)
        acc = alpha * acc + jnp.dot(p.astype(BF16), vs_ref[pl.ds(k0, tk), :], preferred_element_type=F32)
        return m_new, l, acc

    n_tiles = (t0 + tq + tk - 1) // tk
    init = (jnp.full((rows, 1), NEG_INF, F32), jnp.zeros((rows, 1), F32), jnp.zeros((rows, d), F32))
    carry = lax.fori_loop(0, n_tiles - 1, lambda j, c: sel_tile(j, c, False), init)
    _, l_s, acc_s = sel_tile(n_tiles - 1, carry, True)
    o_s = acc_s * (1.0 / l_s)

    wk = tq + WINDOW
    start = pl.multiple_of(jnp.maximum(t0 - WINDOW, 0), tq)
    s_w = lax.dot_general(qx_win, kw_ref[pl.ds(start, wk), :], NT_DIMS, preferred_element_type=F32)
    dist_w = t_col - (start + lax.broadcasted_iota(I32, (rows, wk), 1))
    s_w = jnp.where((dist_w >= 0) & (dist_w < WINDOW), s_w, NEG_INF)
    e_w = jnp.exp2(s_w - jnp.max(s_w, axis=-1, keepdims=True))
    d_w = jnp.sum(e_w, axis=-1, keepdims=True)
    o_w = jnp.dot(e_w.astype(BF16), vw_ref[pl.ds(start, wk), :], preferred_element_type=F32) * (1.0 / d_w)

    gt = jax.nn.sigmoid(gate_ref[...])
    for r in range(rep):
        sl = slice(r * tq, (r + 1) * tq)
        o = (gt[:, 3 * r:3 * r + 1] * o_c[sl] + gt[:, 3 * r + 1:3 * r + 2] * o_s[sl]
             + gt[:, 3 * r + 2:3 * r + 3] * o_w[sl])
        o_ref[:, r * d:(r + 1) * d] = o.astype(o_ref.dtype)


def _nsa_attention(qa, q_col0, cmp, kvb, gates, slopes, seq):
    bsz = qa.shape[0]
    g, rep, d, tq = NSA_KV_HEADS, NSA_REP, HEAD_DIM, NSA_TQ
    ncp = seq // CMP_STRIDE
    n_sel = seq // SEL_BLOCK
    assert n_sel <= AUX_SEL_LANES and seq % NSA_TK == 0 and seq >= tq + WINDOW
    qblk0 = q_col0 // (rep * d)
    kv_spec = lambda s: pl.BlockSpec((None, seq, d), lambda b, h, i: (b, 0, s * NSA_KV_HEADS + h))
    cmp_spec = lambda s: pl.BlockSpec((None, None, None, ncp, d), lambda b, h, i: (s, b, h, 0, 0))
    return pl.pallas_call(
        _nsa_kernel,
        out_shape=jax.ShapeDtypeStruct((bsz, seq, NSA_DQ), BF16),
        grid=(bsz, g, seq // tq),
        in_specs=[pl.BlockSpec(memory_space=pltpu.SMEM),
                  pl.BlockSpec((None, rep, LANES), lambda b, h, i: (h, 0, 0)),
                  pl.BlockSpec((seq, LANES), lambda b, h, i: (0, 0)),
                  pl.BlockSpec((n_sel, ncp), lambda b, h, i: (0, 0)),
                  pl.BlockSpec((None, tq, rep * d), lambda b, h, i: (b, i, qblk0 + h)),
                  cmp_spec(0), cmp_spec(1), kv_spec(0), kv_spec(1), kv_spec(2), kv_spec(3),
                  pl.BlockSpec((None, None, tq, 3 * rep), lambda b, h, i: (b, h, i, 0))],
        out_specs=pl.BlockSpec((None, tq, rep * d), lambda b, h, i: (b, i, h)),
        scratch_shapes=[pltpu.VMEM((seq, 2 * d), BF16), pltpu.VMEM((seq, 2 * d), BF16)],
        compiler_params=_params(("parallel", "parallel", "arbitrary")),
        name="nsa_attention",
    )(slopes, _nsa_query_aux(slopes * LOG2E), _nsa_key_aux(seq), _nsa_overlap(n_sel, ncp), qa, cmp, cmp, kvb, kvb, kvb, kvb, gates)


def _split_columns(proj):
    out, start = [], 0
    for size in IN_SIZES:
        out.append(proj[..., start:start + size])
        start += size
    return out


def kernel(x, c, ada_w, ada_b, norm1_w, w_in, gdn_conv_w, gdn_a_log, gdn_dt_bias, gdn_norm_w,
           cmp_pos_k, cmp_w1_k, cmp_w2_k, cmp_pos_v, cmp_w1_v, cmp_w2_v, w_out, norm2_w,
           w_up, w_down, final_norm_w):
    bsz, seq, d = x.shape
    depth = ada_w.shape[0]
    m = bsz * seq
    x2 = x.reshape(m, d)
    slopes = 2.0 ** (-8.0 * jnp.arange(1, NSA_HEADS + 1, dtype=F32) / NSA_HEADS)
    for i in range(depth):
        mod = _modulation(c, ada_w[i], ada_b[i])
        sh1, sc1, g1, sh2, sc2, g2 = jnp.split(mod, 6, axis=-1)
        h = _norm_mod(x2, norm1_w[i], sc1, sh1, seq, BF16)
        cols = _split_columns(w_in[i])
        small = jnp.concatenate([cols[4], cols[5], cols[13]], axis=-1)
        small = jnp.pad(small, ((0, 0), (0, LANES - small.shape[1])))
        w_a = jnp.concatenate(cols[0:4] + [cols[6]], axis=-1).astype(BF16)
        w_b = jnp.concatenate(cols[7:9] + [small], axis=-1).astype(BF16)
        w_c = jnp.concatenate(cols[9:13], axis=-1).astype(BF16)
        pa, w_up_bf16 = _matmul(h, w_a, tm=MM_TILE, tn=MM_TILE, out_dtype=F32, name="in_proj_a", cast_to_bf16=w_up[i])
        pa = pa.reshape(bsz, seq, -1)
        pb = _matmul(h, w_b, tm=MM_TILE, tn=w_b.shape[1], out_dtype=F32, name="in_proj_b").reshape(bsz, seq, -1)
        pc = _matmul(h, w_c, tm=MM_TILE, tn=MM_TILE, out_dtype=BF16, name="in_proj_c").reshape(bsz, seq, -1)

        o_a = _gdn(pa, pb, 2 * NSA_DKV // LANES, gdn_conv_w[i], gdn_a_log[i], gdn_dt_bias[i], gdn_norm_w[i], seq)
        ngate = pb[..., 2 * NSA_DKV + 2 * GDN_HEADS:2 * NSA_DKV + 2 * GDN_HEADS + 3 * NSA_HEADS]

        cmp = _compress(pb, jnp.stack([cmp_pos_k[i], cmp_pos_v[i]]), jnp.stack([cmp_w1_k[i], cmp_w1_v[i]]),
                        jnp.stack([cmp_w2_k[i], cmp_w2_v[i]]), seq)
        gates = ngate.reshape(bsz, seq, NSA_KV_HEADS, 3 * NSA_REP).transpose(0, 2, 1, 3)
        o_b = _nsa_attention(pa, 4 * GDN_DK, cmp, pc, gates, slopes, seq)

        x2 = _matmul_parts_residual([o_a.reshape(m, -1), o_b.reshape(m, -1)], w_out[i].astype(BF16), x2, g1, seq,
                                    tm=MM_TILE, tn=MM_TILE, name="out_proj")
        h2 = _norm_mod(x2, norm2_w[i], sc2, sh2, seq, BF16)
        hid, w_down_bf16 = _matmul(h2, w_up_bf16, tm=MM_TILE, tn=MM_TILE, out_dtype=BF16, relu2=True, name="mlp_up",
                                   cast_to_bf16=w_down[i])
        x2 = _matmul_residual(hid, w_down_bf16, x2, g2, seq,
                              tm=MM_TILE, tn=MM_TILE, tk=MM_TILE_K, name="mlp_down")
    return _final_norm(x2, final_norm_w).reshape(bsz, seq, d)
```

```python
import functools

import jax
import jax.numpy as jnp
from jax import lax
from jax.experimental import pallas as pl
from jax.experimental.pallas import tpu as pltpu

F32 = jnp.float32
BF16 = jnp.bfloat16
I32 = jnp.int32

HEAD_DIM = 128
GDN_HEADS = 16
NSA_HEADS = 16
NSA_KV_HEADS = 4
NSA_REP = NSA_HEADS // NSA_KV_HEADS
GDN_DK = GDN_HEADS * HEAD_DIM
NSA_DQ = NSA_HEADS * HEAD_DIM
NSA_DKV = NSA_KV_HEADS * HEAD_DIM
GDN_CONV = 4
GDN_CHUNK = 128
GDN_HEAD_GROUP = 16
CMP_BLOCK = 32
CMP_STRIDE = 16
SEL_BLOCK = 64
SEL_TOPN = 16
SEL_LOCAL = 2
WINDOW = 512
EPS = 1e-6
IN_SIZES = (GDN_DK, GDN_DK, GDN_DK, GDN_DK, GDN_HEADS, GDN_HEADS,
            NSA_DQ, NSA_DKV, NSA_DKV, NSA_DKV, NSA_DKV, NSA_DKV, NSA_DKV, 3 * NSA_HEADS)

VMEM_LIMIT_V7X = 56 * 1024 * 1024
LANES = 128
SUBLANES = 8
BF16_SUBLANES = 16
MM_TILE = 1024
MM_TILE_K = 4096
NORM_ROWS = 256
MOD_TILE_N = 512
NSA_TQ = 512
NSA_TK = 1024
NEG_INF = float("-inf")
AUX_SEL_LANES = 64
SEL_BIAS = 2.0 ** 100
LOG2E = 1.4426950408889634
NT_DIMS = (((1,), (1,)), ((), ()))
TN_DIMS = (((0,), (0,)), ((), ()))


def _params(sem):
    return pltpu.CompilerParams(dimension_semantics=sem, vmem_limit_bytes=VMEM_LIMIT_V7X)


def _mod_kernel(c_ref, w_ref, b_ref, o_ref):
    c = c_ref[...]
    s = (c * jax.nn.sigmoid(c)).astype(BF16)
    o_ref[...] = jnp.dot(s, w_ref[...].astype(BF16), preferred_element_type=F32) + b_ref[...]


def _modulation(c, w, b):
    bsz, d = c.shape
    n = w.shape[1]
    rows = SUBLANES
    cp = jnp.zeros((rows, d), F32).at[:bsz].set(c)
    tn = MOD_TILE_N
    out = pl.pallas_call(
        _mod_kernel,
        out_shape=jax.ShapeDtypeStruct((rows, n), F32),
        grid=(n // tn,),
        in_specs=[pl.BlockSpec((rows, d), lambda j: (0, 0)),
                  pl.BlockSpec((d, tn), lambda j: (0, j)),
                  pl.BlockSpec((1, tn), lambda j: (0, j))],
        out_specs=pl.BlockSpec((rows, tn), lambda j: (0, j)),
        compiler_params=_params(("parallel",)),
        name="adaln_mod",
    )(cp, w, b.reshape(1, n))
    return out[:bsz]


def _norm_kernel(x_ref, g_ref, sc_ref, sh_ref, o_ref):
    x = x_ref[...]
    y = x * lax.rsqrt(jnp.mean(x * x, axis=-1, keepdims=True) + EPS) * g_ref[...]
    o_ref[...] = (y * (1.0 + sc_ref[0]) + sh_ref[0]).astype(o_ref.dtype)


def _final_norm_kernel(x_ref, g_ref, o_ref):
    x = x_ref[...]
    o_ref[...] = x * lax.rsqrt(jnp.mean(x * x, axis=-1, keepdims=True) + EPS) * g_ref[...]


def _norm_mod(x2, gain, scale, shift, seq, out_dtype):
    m, d = x2.shape
    tm = NORM_ROWS
    per_b = seq // tm
    bsz = m // seq
    return pl.pallas_call(
        _norm_kernel,
        out_shape=jax.ShapeDtypeStruct((m, d), out_dtype),
        grid=(m // tm,),
        in_specs=[pl.BlockSpec((tm, d), lambda i: (i, 0)),
                  pl.BlockSpec((1, d), lambda i: (0, 0)),
                  pl.BlockSpec((1, 1, d), lambda i: (i // per_b, 0, 0)),
                  pl.BlockSpec((1, 1, d), lambda i: (i // per_b, 0, 0))],
        out_specs=pl.BlockSpec((tm, d), lambda i: (i, 0)),
        compiler_params=_params(("parallel",)),
        name="rmsnorm_mod",
    )(x2, gain.reshape(1, d), scale.reshape(bsz, 1, d), shift.reshape(bsz, 1, d))


def _final_norm(x2, gain):
    m, d = x2.shape
    tm = NORM_ROWS
    return pl.pallas_call(
        _final_norm_kernel,
        out_shape=jax.ShapeDtypeStruct((m, d), F32),
        grid=(m // tm,),
        in_specs=[pl.BlockSpec((tm, d), lambda i: (i, 0)),
                  pl.BlockSpec((1, d), lambda i: (0, 0))],
        out_specs=pl.BlockSpec((tm, d), lambda i: (i, 0)),
        compiler_params=_params(("parallel",)),
        name="final_rmsnorm",
    )(x2, gain.reshape(1, d))


def _mm_kernel(a_ref, w_ref, *rest, relu2, side_cast):
    if side_cast:
        side_ref, o_ref, side_o_ref = rest
        side_o_ref[...] = side_ref[...].astype(side_o_ref.dtype)
    else:
        (o_ref,) = rest
    acc = jnp.dot(a_ref[...], w_ref[...], preferred_element_type=F32)
    if relu2:
        acc = jnp.square(jnp.maximum(acc, 0.0))
    o_ref[...] = acc.astype(o_ref.dtype)


def _matmul(a, w, *, tm, tn, out_dtype, relu2=False, name, cast_to_bf16=None):
    m, k = a.shape
    n = w.shape[1]
    grid = (m // tm, n // tn)
    in_specs = [pl.BlockSpec((tm, k), lambda i, j: (i, 0)),
                pl.BlockSpec((k, tn), lambda i, j: (0, j))]
    out_shape = jax.ShapeDtypeStruct((m, n), out_dtype)
    out_specs = pl.BlockSpec((tm, tn), lambda i, j: (i, j))
    operands = (a, w)
    if cast_to_bf16 is not None:
        rows, cols = cast_to_bf16.shape
        steps = grid[0] * grid[1]
        blocks = min(steps, rows // BF16_SUBLANES)
        while rows % blocks or (rows // blocks) % BF16_SUBLANES:
            blocks -= 1
        side_map = lambda i, j: (jnp.minimum(i * grid[1] + j, blocks - 1), 0)
        in_specs.append(pl.BlockSpec((rows // blocks, cols), side_map))
        out_shape = (out_shape, jax.ShapeDtypeStruct((rows, cols), BF16))
        out_specs = (out_specs, pl.BlockSpec((rows // blocks, cols), side_map))
        operands += (cast_to_bf16,)
    return pl.pallas_call(
        functools.partial(_mm_kernel, relu2=relu2, side_cast=cast_to_bf16 is not None),
        out_shape=out_shape,
        grid=grid,
        in_specs=in_specs,
        out_specs=out_specs,
        compiler_params=_params(("arbitrary", "arbitrary") if cast_to_bf16 is not None
                                else ("parallel", "parallel")),
        name=name,
    )(*operands)


def _mm_res_kernel(a_ref, w_ref, x_ref, g_ref, o_ref):
    k = pl.program_id(2)

    @pl.when(k == 0)
    def _():
        o_ref[...] = jnp.zeros_like(o_ref)

    o_ref[...] += jnp.dot(a_ref[...], w_ref[...], preferred_element_type=F32)

    @pl.when(k == pl.num_programs(2) - 1)
    def _():
        o_ref[...] = x_ref[...] + g_ref[0] * o_ref[...]


def _matmul_residual(a, w, x2, gate, seq, *, tm, tn, tk, name):
    m, kdim = a.shape
    n = w.shape[1]
    bsz = m // seq
    per_b = seq // tm
    return pl.pallas_call(
        _mm_res_kernel,
        out_shape=jax.ShapeDtypeStruct((m, n), F32),
        grid=(m // tm, n // tn, kdim // tk),
        in_specs=[pl.BlockSpec((tm, tk), lambda i, j, k: (i, k)),
                  pl.BlockSpec((tk, tn), lambda i, j, k: (k, j)),
                  pl.BlockSpec((tm, tn), lambda i, j, k: (i, j)),
                  pl.BlockSpec((1, 1, tn), lambda i, j, k: (i // per_b, 0, j))],
        out_specs=pl.BlockSpec((tm, tn), lambda i, j, k: (i, j)),
        compiler_params=_params(("parallel", "parallel", "arbitrary")),
        name=name,
    )(a, w, x2, gate.reshape(bsz, 1, n))


def _mm_parts_res_kernel(*refs, nparts):
    a_refs = refs[:nparts]
    w_ref, x_ref, g_ref, o_ref = refs[nparts:]
    width = a_refs[0].shape[1]
    acc = jnp.dot(a_refs[0][...], w_ref[0:width, :], preferred_element_type=F32)
    for idx in range(1, nparts):
        acc += jnp.dot(a_refs[idx][...], w_ref[idx * width:(idx + 1) * width, :], preferred_element_type=F32)
    o_ref[...] = x_ref[...] + g_ref[0] * acc


def _matmul_parts_residual(a_parts, w, x2, gate, seq, *, tm, tn, name):
    nparts = len(a_parts)
    m, width = a_parts[0].shape
    assert all(a.shape == (m, width) for a in a_parts) and w.shape[0] == nparts * width
    n = w.shape[1]
    bsz = m // seq
    per_b = seq // tm
    return pl.pallas_call(
        functools.partial(_mm_parts_res_kernel, nparts=nparts),
        out_shape=jax.ShapeDtypeStruct((m, n), F32),
        grid=(m // tm, n // tn),
        in_specs=[pl.BlockSpec((tm, width), lambda i, j: (i, 0)) for _ in a_parts]
        + [pl.BlockSpec((nparts * width, tn), lambda i, j: (0, j)),
           pl.BlockSpec((tm, tn), lambda i, j: (i, j)),
           pl.BlockSpec((1, 1, tn), lambda i, j: (i // per_b, 0, j))],
        out_specs=pl.BlockSpec((tm, tn), lambda i, j: (i, j)),
        compiler_params=_params(("parallel", "parallel")),
        name=name,
    )(*a_parts, w, x2, gate.reshape(bsz, 1, n))


def _gdn_kernel(q_ref, k_ref, v_ref, z_ref, ba_ref, cw_ref, alog_ref, dtb_ref, nw_ref, o_ref,
                xbuf_ref, s_ref):
    c, d, nh = GDN_CHUNK, HEAD_DIM, GDN_HEADS
    width = nh * d
    hist = SUBLANES

    @pl.when(pl.program_id(1) == 0)
    def _():
        xbuf_ref[:, 0:hist, :] = jnp.zeros((3, hist, width), F32)
        s_ref[...] = jnp.zeros_like(s_ref)

    ba = ba_ref[...]
    beta = jax.nn.sigmoid(ba[:, 0:nh])
    g = -jnp.exp(alog_ref[...]) * jax.nn.softplus(ba[:, nh:2 * nh] + dtb_ref[...])
    r_i = lax.broadcasted_iota(I32, (c, c), 0)
    c_i = lax.broadcasted_iota(I32, (c, c), 1)
    causal = r_i >= c_i
    strict = r_i > c_i
    gam = jnp.dot(causal.astype(F32), g, precision=lax.Precision.HIGHEST, preferred_element_type=F32)
    gam_t = lax.dot_general(g, (r_i <= c_i).astype(F32), TN_DIMS, precision=lax.Precision.HIGHEST,
                            preferred_element_type=F32)
    g_last = gam[c - 1:c, :]
    egam = jnp.exp(gam)
    bexp = beta * egam
    kscale = jnp.exp(g_last - gam)
    eg_last = jnp.exp(g_last)
    nw = nw_ref[...]

    hsl = [slice(h * d, (h + 1) * d) for h in range(nh)]
    hcol = [slice(h, h + 1) for h in range(nh)]
    dot16 = lambda a, b: jnp.dot(a.astype(BF16), b.astype(BF16), preferred_element_type=F32)

    def split(a):
        hi = a.astype(BF16).astype(F32)
        return hi, a - hi

    as_lhs = lambda sp: jnp.concatenate([sp[0], sp[0], sp[1]], axis=1).astype(BF16)
    as_rhs = lambda sp: jnp.concatenate([sp[0], sp[1], sp[0]], axis=0).astype(BF16)
    dot32 = lambda lhs, rhs: jnp.dot(lhs, rhs, preferred_element_type=F32)

    base = 8
    same = lambda s: (r_i // s) == (c_i // s)
    eye = (r_i == c_i).astype(F32)

    for g0 in range(0, nh, GDN_HEAD_GROUP):
        heads = range(g0, g0 + GDN_HEAD_GROUP)
        cols = slice(g0 * d, (g0 + GDN_HEAD_GROUP) * d)
        conv = []
        for i, src in enumerate((q_ref, k_ref, v_ref)):
            wcols = slice(i * width + g0 * d, i * width + (g0 + GDN_HEAD_GROUP) * d)
            xbuf_ref[i, hist:hist + c, cols] = src[:, cols]
            acc = cw_ref[GDN_CONV - 1:GDN_CONV, wcols] * src[:, cols]
            for j in range(GDN_CONV - 1):
                shift = GDN_CONV - 1 - j
                acc += cw_ref[j:j + 1, wcols] * xbuf_ref[i, pl.ds(hist - shift, c), cols]
            conv.append(acc * jax.nn.sigmoid(acc))
            xbuf_ref[i, 0:hist, cols] = xbuf_ref[i, c:c + hist, cols]
        cq, ck, cv = conv
        gsl = {h: slice((h - g0) * d, (h - g0 + 1) * d) for h in heads}
        qn, kn, qk, lower = {}, {}, {}, {}
        for h in heads:
            qh = cq[:, gsl[h]]
            kh = ck[:, gsl[h]]
            qh = qh * lax.rsqrt(jnp.sum(qh * qh, axis=-1, keepdims=True) + EPS) * (d ** -0.5)
            kh = kh * lax.rsqrt(jnp.sum(kh * kh, axis=-1, keepdims=True) + EPS)
            kb = kh.astype(BF16)
            both = lax.dot_general(jnp.concatenate([qh.astype(BF16), kb], axis=0), kb, NT_DIMS,
                                   preferred_element_type=F32)
            diff = gam[:, hcol[h]] - gam_t[h:h + 1, :]
            dec = jnp.where(causal, jnp.exp(jnp.where(causal, diff, 0.0)), 0.0)
            qn[h] = qh
            kn[h] = kh
            qk[h] = both[:c] * dec
            lower[h] = jnp.where(strict, beta[:, hcol[h]] * both[c:] * dec, 0.0)

        diag = {h: jnp.where(same(base), lower[h], 0.0) for h in heads}
        dsp = {h: split(diag[h]) for h in heads}
        dlhs = {h: as_lhs(dsp[h]) for h in heads}
        p2 = {h: dot32(dlhs[h], as_rhs(dsp[h])) for h in heads}
        psp = {h: split(p2[h]) for h in heads}
        prhs = {h: as_rhs(psp[h]) for h in heads}
        qinv = {h: p2[h] - diag[h] - dot32(dlhs[h], prhs[h]) for h in heads}
        p4 = {h: dot32(as_lhs(psp[h]), prhs[h]) for h in heads}
        inv = {h: eye + (qinv[h] + p4[h] + dot32(as_lhs(split(qinv[h])), as_rhs(split(p4[h]))))
               for h in heads}
        size = base
        while size < c:
            off_mask = same(2 * size) & jnp.logical_not(same(size))
            t1 = {h: dot16(jnp.where(off_mask, lower[h], 0.0), inv[h]) for h in heads}
            inv = {h: inv[h] - dot16(inv[h], t1[h]) for h in heads}
            size *= 2

        sol = {}
        for h in heads:
            rhs = jnp.concatenate([cv[:, gsl[h]] * beta[:, hcol[h]], kn[h] * bexp[:, hcol[h]]], axis=1)
            sol[h] = rhs + dot16(inv[h] - eye, rhs)
        wq = {h: dot16(jnp.concatenate([sol[h][:, d:], qn[h] * egam[:, hcol[h]]], axis=0), s_ref[h])
              for h in heads}
        vnew = {h: (sol[h][:, :d] - wq[h][:c]).astype(BF16) for h in heads}
        for h in heads:
            k_dec = kn[h] * kscale[:, hcol[h]]
            s_ref[h] = s_ref[h] * eg_last[:, hcol[h]] + lax.dot_general(
                k_dec.astype(BF16), vnew[h], TN_DIMS, preferred_element_type=F32)
        for h in heads:
            o = wq[h][c:] + jnp.dot(qk[h].astype(BF16), vnew[h], preferred_element_type=F32)
            y = o * lax.rsqrt(jnp.mean(o * o, axis=-1, keepdims=True) + EPS) * nw
            zh = z_ref[:, hsl[h]]
            o_ref[:, hsl[h]] = (y * (zh * jax.nn.sigmoid(zh))).astype(o_ref.dtype)


def _gdn(pa, pb, ba_blk, conv_w, a_log, dt_bias, norm_w, seq):
    bsz = pa.shape[0]
    c, d, nh = GDN_CHUNK, HEAD_DIM, GDN_HEADS
    width = nh * d
    col_spec = lambda j: pl.BlockSpec((None, c, width), lambda b, n: (b, n, j))
    full = lambda shape: pl.BlockSpec(shape, lambda b, n: (0,) * len(shape))
    return pl.pallas_call(
        _gdn_kernel,
        out_shape=jax.ShapeDtypeStruct((bsz, seq, width), BF16),
        grid=(bsz, seq // c),
        in_specs=[col_spec(0), col_spec(1), col_spec(2), col_spec(3),
                  pl.BlockSpec((None, c, LANES), lambda b, n: (b, n, ba_blk)),
                  full((GDN_CONV, 3 * width)), full((1, nh)), full((1, nh)), full((1, d))],
        out_specs=pl.BlockSpec((None, c, width), lambda b, n: (b, n, 0)),
        scratch_shapes=[pltpu.VMEM((3, 8 + c, width), F32), pltpu.VMEM((nh, d, d), F32)],
        compiler_params=_params(("parallel", "arbitrary")),
        name="gdn_chunked",
    )(pa, pa, pa, pa, pb, conv_w, a_log.reshape(1, nh), dt_bias.reshape(1, nh), norm_w.reshape(1, d))


def _compress_kernel(x_ref, pos_ref, w1_ref, w2_ref, o_ref):
    n = o_ref.shape[0]
    d = x_ref.shape[1]
    acc_a = jnp.zeros((n, d), F32)
    acc_b = jnp.zeros((n, d), F32)
    for j in range(CMP_STRIDE):
        xj = x_ref[pl.ds(j, n, stride=CMP_STRIDE), :]
        lo = (xj + pos_ref[j:j + 1, :]).astype(BF16)
        hi = (xj + pos_ref[CMP_STRIDE + j:CMP_STRIDE + j + 1, :]).astype(BF16)
        acc_a += jnp.dot(lo, w1_ref[j * d:(j + 1) * d, :].astype(BF16), preferred_element_type=F32)
        acc_b += jnp.dot(hi, w1_ref[(CMP_STRIDE + j) * d:(CMP_STRIDE + j + 1) * d, :].astype(BF16),
                         preferred_element_type=F32)
    pre = acc_a + pltpu.roll(acc_b, n - 1, 0)
    hid = (pre * jax.nn.sigmoid(pre)).astype(BF16)
    out = jnp.dot(hid, w2_ref[...].astype(BF16), preferred_element_type=F32)
    row = lax.broadcasted_iota(I32, (n, d), 0)
    o_ref[...] = jnp.where(row < n - 1, out, 0.0).astype(o_ref.dtype)


def _compress(kvc, pos, w1, w2, seq):
    bsz = kvc.shape[0]
    g = NSA_KV_HEADS
    n = seq // CMP_STRIDE
    d = HEAD_DIM
    return pl.pallas_call(
        _compress_kernel,
        out_shape=jax.ShapeDtypeStruct((2, bsz, g, n, d), BF16),
        grid=(2, bsz, g),
        in_specs=[pl.BlockSpec((None, seq, d), lambda s, b, h: (b, 0, s * NSA_KV_HEADS + h)),
                  pl.BlockSpec((None, CMP_BLOCK, d), lambda s, b, h: (s, 0, 0)),
                  pl.BlockSpec((None, CMP_BLOCK * d, d), lambda s, b, h: (s, 0, 0)),
                  pl.BlockSpec((None, d, d), lambda s, b, h: (s, 0, 0))],
        out_specs=pl.BlockSpec((None, None, None, n, d), lambda s, b, h: (s, b, h, 0, 0)),
        compiler_params=_params(("parallel", "parallel", "parallel")),
        name="nsa_compress",
    )(kvc, pos, w1, w2)


def _softmax_rows(s, mask):
    s = jnp.where(mask, s, NEG_INF)
    m = jnp.max(s, axis=-1, keepdims=True)
    m = jnp.where(jnp.abs(m) < jnp.inf, m, 0.0)
    e = jnp.exp(s - m)
    d = jnp.sum(e, axis=-1, keepdims=True)
    return e * (1.0 / jnp.where(d > 0, d, 1.0))


def _nsa_key_aux(seq):
    key = jnp.arange(seq, dtype=I32)[:, None]
    lane = jnp.arange(LANES, dtype=I32)[None, :]
    onehot = (lane == key // SEL_BLOCK) & (lane < AUX_SEL_LANES)
    hi = (lane >= AUX_SEL_LANES) & (lane < AUX_SEL_LANES + 3)
    lo = (lane >= AUX_SEL_LANES + 3) & (lane < AUX_SEL_LANES + 6)
    aux = jnp.where(onehot, 1, 0) + jnp.where(hi, (key // 64) * 64, 0) + jnp.where(lo, key % 64, 0)
    return aux.astype(BF16)


def _nsa_query_aux(slopes):
    s1 = slopes.astype(BF16).astype(F32)
    s2 = (slopes - s1).astype(BF16).astype(F32)
    s3 = (slopes - s1 - s2).astype(BF16).astype(F32)
    pieces = jnp.stack([s1, s2, s3, s1, s2, s3], axis=-1)
    aux = jnp.zeros((slopes.shape[0], LANES), F32).at[:, AUX_SEL_LANES:AUX_SEL_LANES + 6].set(pieces)
    return aux.reshape(NSA_KV_HEADS, NSA_REP, LANES)


def _nsa_overlap(n_sel, n_cmp):
    lo = jnp.arange(n_sel, dtype=I32)[:, None] * SEL_BLOCK
    start = jnp.arange(n_cmp, dtype=I32)[None, :] * CMP_STRIDE
    return ((start <= lo + (SEL_BLOCK - 1)) & (start + (CMP_BLOCK - 1) >= lo)).astype(BF16)


def _nsa_kernel(slopes_ref, qaux_ref, kaux_ref, ov_ref, q_ref, kc_ref, vc_ref, ksel_ref, vs_ref, kwin_ref,
                vw_ref, gate_ref, o_ref, ks_ref, kw_ref):
    seq = ks_ref.shape[0]
    tq, tk, rep, d = NSA_TQ, NSA_TK, NSA_REP, HEAD_DIM
    rows = rep * tq
    g = pl.program_id(1)
    t0 = pl.program_id(2) * tq

    @pl.when(pl.program_id(2) == 0)
    def _():
        ks_ref[:, 0:d] = ksel_ref[...]
        ks_ref[:, d:2 * d] = kaux_ref[...]
        kw_ref[:, 0:d] = kwin_ref[...]
        kw_ref[:, d:2 * d] = kaux_ref[...]

    q = q_ref[...] * (d ** -0.5)
    q = jnp.concatenate([q[:, r * d:(r + 1) * d] for r in range(rep)], axis=0)
    qs = q.astype(BF16)
    qs2 = (q * LOG2E).astype(BF16)
    slope = jnp.concatenate([jnp.full((tq, 1), slopes_ref[g * rep + r], F32) for r in range(rep)], axis=0)
    row = lax.broadcasted_iota(I32, (rows, 1), 0)
    t_col = t0 + (row & (tq - 1))

    ncp = kc_ref.shape[0]
    n_idx = lax.broadcasted_iota(I32, (rows, ncp), 1)
    dist_c = t_col - (n_idx * CMP_STRIDE + (CMP_BLOCK - 1))
    s_c = lax.dot_general(qs, kc_ref[...], NT_DIMS, preferred_element_type=F32)
    s_c = s_c - slope * dist_c.astype(F32)
    p_c = _softmax_rows(s_c, dist_c >= 0).astype(BF16)
    o_c = jnp.dot(p_c, vc_ref[...], preferred_element_type=F32)

    n_sel = seq // SEL_BLOCK
    ov_t = ov_ref[...]
    imp = jnp.zeros((n_sel, tq), F32)
    for r in range(rep):
        imp += lax.dot_general(ov_t, p_c[r * tq:(r + 1) * tq, :], NT_DIMS, preferred_element_type=F32)
    blk = lax.broadcasted_iota(I32, (n_sel, tq), 0)
    cur = lax.shift_right_logical(t0 + lax.broadcasted_iota(I32, (n_sel, tq), 1), SEL_BLOCK.bit_length() - 1)
    forced = (blk == 0) | ((cur - blk) < SEL_LOCAL)
    imp = jnp.where(forced, jnp.inf, imp)
    imp = jnp.where(blk <= cur, imp, NEG_INF)
    rank = jnp.zeros((n_sel, tq), I32)
    for i in range(n_sel):
        ri = imp[i:i + 1, :]
        rank += ((ri > imp) | ((ri == imp) & (blk > i))).astype(I32)
    sel_t = (rank < min(SEL_TOPN, n_sel)).astype(F32)

    to_lanes = (lax.broadcasted_iota(I32, (n_sel, LANES), 0)
                == lax.broadcasted_iota(I32, (n_sel, LANES), 1)).astype(BF16)
    picked = lax.dot_general(sel_t.astype(BF16), to_lanes, TN_DIMS, preferred_element_type=F32)
    lane = lax.broadcasted_iota(I32, (tq, LANES), 1)
    sel_bias = jnp.where(lane < AUX_SEL_LANES, (picked - 1.0) * SEL_BIAS, 0.0)
    qaux = qaux_ref[...]
    qx_sel = jnp.concatenate(
        [qs2, jnp.concatenate([sel_bias + qaux[r:r + 1, :] for r in range(rep)], axis=0).astype(BF16)], axis=1)
    qx_win = jnp.concatenate(
        [qs2, jnp.concatenate([jnp.broadcast_to(qaux[r:r + 1, :], (tq, LANES)) for r in range(rep)],
                              axis=0).astype(BF16)], axis=1)

    def sel_tile(j, carry, diagonal):
        m, l, acc = carry
        k0 = pl.multiple_of(j * tk, tk)
        s = lax.dot_general(qx_sel, ks_ref[pl.ds(k0, tk), :], NT_DIMS, preferred_element_type=F32)
        if diagonal:
            ahead = (lax.broadcasted_iota(I32, (rows, tk), 1)
                     - (lax.broadcasted_iota(I32, (rows, tk), 0) & (tq - 1)))
            s = jnp.where(ahead <= t0 - k0, s, -SEL_BIAS)
        m_new = jnp.maximum(m, jnp.max(s, axis=-1, keepdims=True))
        alpha = jnp.exp2(m - m_new)
        p = jnp.exp2(s - m_new)
        l = alpha * l + jnp.sum(p, axis=-1, keepdims=True)
        acc = alpha * acc + jnp.dot(p.astype(BF16), vs_ref[pl.ds(k0, tk), :], preferred_element_type=F32)
        return m_new, l, acc

    n_tiles = (t0 + tq + tk - 1) // tk
    init = (jnp.full((rows, 1), NEG_INF, F32), jnp.zeros((rows, 1), F32), jnp.zeros((rows, d), F32))
    carry = lax.fori_loop(0, n_tiles - 1, lambda j, c: sel_tile(j, c, False), init)
    _, l_s, acc_s = sel_tile(n_tiles - 1, carry, True)
    o_s = acc_s * (1.0 / l_s)

    wk = tq + WINDOW
    start = pl.multiple_of(jnp.maximum(t0 - WINDOW, 0), tq)
    s_w = lax.dot_general(qx_win, kw_ref[pl.ds(start, wk), :], NT_DIMS, preferred_element_type=F32)
    dist_w = t_col - (start + lax.broadcasted_iota(I32, (rows, wk), 1))
    s_w = jnp.where((dist_w >= 0) & (dist_w < WINDOW), s_w, NEG_INF)
    e_w = jnp.exp2(s_w - jnp.max(s_w, axis=-1, keepdims=True))
    d_w = jnp.sum(e_w, axis=-1, keepdims=True)
    o_w = jnp.dot(e_w.astype(BF16), vw_ref[pl.ds(start, wk), :], preferred_element_type=F32) * (1.0 / d_w)

    gt = jax.nn.sigmoid(gate_ref[...])
    for r in range(rep):
        sl = slice(r * tq, (r + 1) * tq)
        o = (gt[:, 3 * r:3 * r + 1] * o_c[sl] + gt[:, 3 * r + 1:3 * r + 2] * o_s[sl]
             + gt[:, 3 * r + 2:3 * r + 3] * o_w[sl])
        o_ref[:, r * d:(r + 1) * d] = o.astype(o_ref.dtype)


def _nsa_attention(qa, q_col0, cmp, kvb, gates, slopes, seq):
    bsz = qa.shape[0]
    g, rep, d, tq = NSA_KV_HEADS, NSA_REP, HEAD_DIM, NSA_TQ
    ncp = seq // CMP_STRIDE
    n_sel = seq // SEL_BLOCK
    assert n_sel <= AUX_SEL_LANES and seq % NSA_TK == 0 and seq >= tq + WINDOW
    qblk0 = q_col0 // (rep * d)
    kv_spec = lambda s: pl.BlockSpec((None, seq, d), lambda b, h, i: (b, 0, s * NSA_KV_HEADS + h))
    cmp_spec = lambda s: pl.BlockSpec((None, None, None, ncp, d), lambda b, h, i: (s, b, h, 0, 0))
    return pl.pallas_call(
        _nsa_kernel,
        out_shape=jax.ShapeDtypeStruct((bsz, seq, NSA_DQ), BF16),
        grid=(bsz, g, seq // tq),
        in_specs=[pl.BlockSpec(memory_space=pltpu.SMEM),
                  pl.BlockSpec((None, rep, LANES), lambda b, h, i: (h, 0, 0)),
                  pl.BlockSpec((seq, LANES), lambda b, h, i: (0, 0)),
                  pl.BlockSpec((n_sel, ncp), lambda b, h, i: (0, 0)),
                  pl.BlockSpec((None, tq, rep * d), lambda b, h, i: (b, i, qblk0 + h)),
                  cmp_spec(0), cmp_spec(1), kv_spec(0), kv_spec(1), kv_spec(2), kv_spec(3),
                  pl.BlockSpec((None, None, tq, 3 * rep), lambda b, h, i: (b, h, i, 0))],
        out_specs=pl.BlockSpec((None, tq, rep * d), lambda b, h, i: (b, i, h)),
        scratch_shapes=[pltpu.VMEM((seq, 2 * d), BF16), pltpu.VMEM((seq, 2 * d), BF16)],
        compiler_params=_params(("parallel", "parallel", "arbitrary")),
        name="nsa_attention",
    )(slopes, _nsa_query_aux(slopes * LOG2E), _nsa_key_aux(seq), _nsa_overlap(n_sel, ncp), qa, cmp, cmp, kvb, kvb, kvb, kvb, gates)


def _split_columns(proj):
    out, start = [], 0
    for size in IN_SIZES:
        out.append(proj[..., start:start + size])
        start += size
    return out


def kernel(x, c, ada_w, ada_b, norm1_w, w_in, gdn_conv_w, gdn_a_log, gdn_dt_bias, gdn_norm_w,
           cmp_pos_k, cmp_w1_k, cmp_w2_k, cmp_pos_v, cmp_w1_v, cmp_w2_v, w_out, norm2_w,
           w_up, w_down, final_norm_w):
    bsz, seq, d = x.shape
    depth = ada_w.shape[0]
    m = bsz * seq
    x2 = x.reshape(m, d)
    slopes = 2.0 ** (-8.0 * jnp.arange(1, NSA_HEADS + 1, dtype=F32) / NSA_HEADS)
    for i in range(depth):
        mod = _modulation(c, ada_w[i], ada_b[i])
        sh1, sc1, g1, sh2, sc2, g2 = jnp.split(mod, 6, axis=-1)
        h = _norm_mod(x2, norm1_w[i], sc1, sh1, seq, BF16)
        cols = _split_columns(w_in[i])
        small = jnp.concatenate([cols[4], cols[5], cols[13]], axis=-1)
        small = jnp.pad(small, ((0, 0), (0, LANES - small.shape[1])))
        w_a = jnp.concatenate(cols[0:4] + [cols[6]], axis=-1).astype(BF16)
        w_b = jnp.concatenate(cols[7:9] + [small], axis=-1).astype(BF16)
        w_c = jnp.concatenate(cols[9:13], axis=-1).astype(BF16)
        pa, w_up_bf16 = _matmul(h, w_a, tm=MM_TILE, tn=MM_TILE, out_dtype=F32, name="in_proj_a", cast_to_bf16=w_up[i])
        pa = pa.reshape(bsz, seq, -1)
        pb = _matmul(h, w_b, tm=MM_TILE, tn=w_b.shape[1], out_dtype=F32, name="in_proj_b").reshape(bsz, seq, -1)
        pc = _matmul(h, w_c, tm=MM_TILE, tn=MM_TILE, out_dtype=BF16, name="in_proj_c").reshape(bsz, seq, -1)

        o_a = _gdn(pa, pb, 2 * NSA_DKV // LANES, gdn_conv_w[i], gdn_a_log[i], gdn_dt_bias[i], gdn_norm_w[i], seq)
        ngate = pb[..., 2 * NSA_DKV + 2 * GDN_HEADS:2 * NSA_DKV + 2 * GDN_HEADS + 3 * NSA_HEADS]

        cmp = _compress(pb, jnp.stack([cmp_pos_k[i], cmp_pos_v[i]]), jnp.stack([cmp_w1_k[i], cmp_w1_v[i]]),
                        jnp.stack([cmp_w2_k[i], cmp_w2_v[i]]), seq)
        gates = ngate.reshape(bsz, seq, NSA_KV_HEADS, 3 * NSA_REP).transpose(0, 2, 1, 3)
        o_b = _nsa_attention(pa, 4 * GDN_DK, cmp, pc, gates, slopes, seq)

        x2 = _matmul_parts_residual([o_a.reshape(m, -1), o_b.reshape(m, -1)], w_out[i].astype(BF16), x2, g1, seq,
                                    tm=MM_TILE, tn=MM_TILE, name="out_proj")
        h2 = _norm_mod(x2, norm2_w[i], sc2, sh2, seq, BF16)
        hid, w_down_bf16 = _matmul(h2, w_up_bf16, tm=MM_TILE, tn=MM_TILE, out_dtype=BF16, relu2=True, name="mlp_up",
                                   cast_to_bf16=w_down[i])
        x2 = _matmul_residual(hid, w_down_bf16, x2, g2, seq,
                              tm=MM_TILE, tn=MM_TILE, tk=MM_TILE_K, name="mlp_down")
    return _final_norm(x2, final_norm_w).reshape(bsz, seq, d)
```

```python
import functools

import jax
import jax.numpy as jnp
from jax import lax
from jax.experimental import pallas as pl
from jax.experimental.pallas import tpu as pltpu

F32 = jnp.float32
BF16 = jnp.bfloat16
I32 = jnp.int32

HEAD_DIM = 128
GDN_HEADS = 16
NSA_HEADS = 16
NSA_KV_HEADS = 4
NSA_REP = NSA_HEADS // NSA_KV_HEADS
GDN_DK = GDN_HEADS * HEAD_DIM
NSA_DQ = NSA_HEADS * HEAD_DIM
NSA_DKV = NSA_KV_HEADS * HEAD_DIM
GDN_CONV = 4
GDN_CHUNK = 128
GDN_HEAD_GROUP = 16
CMP_BLOCK = 32
CMP_STRIDE = 16
SEL_BLOCK = 64
SEL_TOPN = 16
SEL_LOCAL = 2
WINDOW = 512
EPS = 1e-6
IN_SIZES = (GDN_DK, GDN_DK, GDN_DK, GDN_DK, GDN_HEADS, GDN_HEADS,
            NSA_DQ, NSA_DKV, NSA_DKV, NSA_DKV, NSA_DKV, NSA_DKV, NSA_DKV, 3 * NSA_HEADS)

VMEM_LIMIT_V7X = 56 * 1024 * 1024
LANES = 128
SUBLANES = 8
BF16_SUBLANES = 16
MM_TILE = 1024
MM_TILE_K = 4096
NORM_ROWS = 512
MOD_TILE_N = 512
NSA_TQ = 512
NSA_TK = 1024
NEG_INF = float("-inf")
AUX_SEL_LANES = 64
SEL_BIAS = 2.0 ** 100
LOG2E = 1.4426950408889634
NT_DIMS = (((1,), (1,)), ((), ()))
TN_DIMS = (((0,), (0,)), ((), ()))


def _params(sem):
    return pltpu.CompilerParams(dimension_semantics=sem, vmem_limit_bytes=VMEM_LIMIT_V7X)


def _mod_kernel(c_ref, w_ref, b_ref, o_ref):
    c = c_ref[...]
    s = (c * jax.nn.sigmoid(c)).astype(BF16)
    o_ref[...] = jnp.dot(s, w_ref[...].astype(BF16), preferred_element_type=F32) + b_ref[...]


def _modulation(c, w, b):
    bsz, d = c.shape
    n = w.shape[1]
    rows = SUBLANES
    cp = jnp.zeros((rows, d), F32).at[:bsz].set(c)
    tn = MOD_TILE_N
    out = pl.pallas_call(
        _mod_kernel,
        out_shape=jax.ShapeDtypeStruct((rows, n), F32),
        grid=(n // tn,),
        in_specs=[pl.BlockSpec((rows, d), lambda j: (0, 0)),
                  pl.BlockSpec((d, tn), lambda j: (0, j)),
                  pl.BlockSpec((1, tn), lambda j: (0, j))],
        out_specs=pl.BlockSpec((rows, tn), lambda j: (0, j)),
        compiler_params=_params(("parallel",)),
        name="adaln_mod",
    )(cp, w, b.reshape(1, n))
    return out[:bsz]


def _norm_kernel(x_ref, g_ref, sc_ref, sh_ref, o_ref):
    x = x_ref[...]
    y = x * lax.rsqrt(jnp.mean(x * x, axis=-1, keepdims=True) + EPS) * g_ref[...]
    o_ref[...] = (y * (1.0 + sc_ref[0]) + sh_ref[0]).astype(o_ref.dtype)


def _final_norm_kernel(x_ref, g_ref, o_ref):
    x = x_ref[...]
    o_ref[...] = x * lax.rsqrt(jnp.mean(x * x, axis=-1, keepdims=True) + EPS) * g_ref[...]


def _norm_mod(x2, gain, scale, shift, seq, out_dtype):
    m, d = x2.shape
    tm = NORM_ROWS
    per_b = seq // tm
    bsz = m // seq
    return pl.pallas_call(
        _norm_kernel,
        out_shape=jax.ShapeDtypeStruct((m, d), out_dtype),
        grid=(m // tm,),
        in_specs=[pl.BlockSpec((tm, d), lambda i: (i, 0)),
                  pl.BlockSpec((1, d), lambda i: (0, 0)),
                  pl.BlockSpec((1, 1, d), lambda i: (i // per_b, 0, 0)),
                  pl.BlockSpec((1, 1, d), lambda i: (i // per_b, 0, 0))],
        out_specs=pl.BlockSpec((tm, d), lambda i: (i, 0)),
        compiler_params=_params(("parallel",)),
        name="rmsnorm_mod",
    )(x2, gain.reshape(1, d), scale.reshape(bsz, 1, d), shift.reshape(bsz, 1, d))


def _final_norm(x2, gain):
    m, d = x2.shape
    tm = NORM_ROWS
    return pl.pallas_call(
        _final_norm_kernel,
        out_shape=jax.ShapeDtypeStruct((m, d), F32),
        grid=(m // tm,),
        in_specs=[pl.BlockSpec((tm, d), lambda i: (i, 0)),
                  pl.BlockSpec((1, d), lambda i: (0, 0))],
        out_specs=pl.BlockSpec((tm, d), lambda i: (i, 0)),
        compiler_params=_params(("parallel",)),
        name="final_rmsnorm",
    )(x2, gain.reshape(1, d))


def _mm_kernel(a_ref, w_ref, *rest, relu2, side_cast):
    if side_cast:
        side_ref, o_ref, side_o_ref = rest
        side_o_ref[...] = side_ref[...].astype(side_o_ref.dtype)
    else:
        (o_ref,) = rest
    acc = jnp.dot(a_ref[...], w_ref[...], preferred_element_type=F32)
    if relu2:
        acc = jnp.square(jnp.maximum(acc, 0.0))
    o_ref[...] = acc.astype(o_ref.dtype)


def _matmul(a, w, *, tm, tn, out_dtype, relu2=False, name, cast_to_bf16=None):
    m, k = a.shape
    n = w.shape[1]
    grid = (m // tm, n // tn)
    in_specs = [pl.BlockSpec((tm, k), lambda i, j: (i, 0)),
                pl.BlockSpec((k, tn), lambda i, j: (0, j))]
    out_shape = jax.ShapeDtypeStruct((m, n), out_dtype)
    out_specs = pl.BlockSpec((tm, tn), lambda i, j: (i, j))
    operands = (a, w)
    if cast_to_bf16 is not None:
        rows, cols = cast_to_bf16.shape
        steps = grid[0] * grid[1]
        blocks = min(steps, rows // BF16_SUBLANES)
        while rows % blocks or (rows // blocks) % BF16_SUBLANES:
            blocks -= 1
        side_map = lambda i, j: (jnp.minimum(i * grid[1] + j, blocks - 1), 0)
        in_specs.append(pl.BlockSpec((rows // blocks, cols), side_map))
        out_shape = (out_shape, jax.ShapeDtypeStruct((rows, cols), BF16))
        out_specs = (out_specs, pl.BlockSpec((rows // blocks, cols), side_map))
        operands += (cast_to_bf16,)
    return pl.pallas_call(
        functools.partial(_mm_kernel, relu2=relu2, side_cast=cast_to_bf16 is not None),
        out_shape=out_shape,
        grid=grid,
        in_specs=in_specs,
        out_specs=out_specs,
        compiler_params=_params(("arbitrary", "arbitrary") if cast_to_bf16 is not None
                                else ("parallel", "parallel")),
        name=name,
    )(*operands)


def _mm_res_kernel(a_ref, w_ref, x_ref, g_ref, o_ref):
    k = pl.program_id(2)

    @pl.when(k == 0)
    def _():
        o_ref[...] = jnp.zeros_like(o_ref)

    o_ref[...] += jnp.dot(a_ref[...], w_ref[...], preferred_element_type=F32)

    @pl.when(k == pl.num_programs(2) - 1)
    def _():
        o_ref[...] = x_ref[...] + g_ref[0] * o_ref[...]


def _matmul_residual(a, w, x2, gate, seq, *, tm, tn, tk, name):
    m, kdim = a.shape
    n = w.shape[1]
    bsz = m // seq
    per_b = seq // tm
    return pl.pallas_call(
        _mm_res_kernel,
        out_shape=jax.ShapeDtypeStruct((m, n), F32),
        grid=(m // tm, n // tn, kdim // tk),
        in_specs=[pl.BlockSpec((tm, tk), lambda i, j, k: (i, k)),
                  pl.BlockSpec((tk, tn), lambda i, j, k: (k, j)),
                  pl.BlockSpec((tm, tn), lambda i, j, k: (i, j)),
                  pl.BlockSpec((1, 1, tn), lambda i, j, k: (i // per_b, 0, j))],
        out_specs=pl.BlockSpec((tm, tn), lambda i, j, k: (i, j)),
        compiler_params=_params(("parallel", "parallel", "arbitrary")),
        name=name,
    )(a, w, x2, gate.reshape(bsz, 1, n))


def _mm_parts_res_kernel(*refs, nparts):
    a_refs = refs[:nparts]
    w_ref, x_ref, g_ref, o_ref = refs[nparts:]
    width = a_refs[0].shape[1]
    acc = jnp.dot(a_refs[0][...], w_ref[0:width, :], preferred_element_type=F32)
    for idx in range(1, nparts):
        acc += jnp.dot(a_refs[idx][...], w_ref[idx * width:(idx + 1) * width, :], preferred_element_type=F32)
    o_ref[...] = x_ref[...] + g_ref[0] * acc


def _matmul_parts_residual(a_parts, w, x2, gate, seq, *, tm, tn, name):
    nparts = len(a_parts)
    m, width = a_parts[0].shape
    assert all(a.shape == (m, width) for a in a_parts) and w.shape[0] == nparts * width
    n = w.shape[1]
    bsz = m // seq
    per_b = seq // tm
    return pl.pallas_call(
        functools.partial(_mm_parts_res_kernel, nparts=nparts),
        out_shape=jax.ShapeDtypeStruct((m, n), F32),
        grid=(m // tm, n // tn),
        in_specs=[pl.BlockSpec((tm, width), lambda i, j: (i, 0)) for _ in a_parts]
        + [pl.BlockSpec((nparts * width, tn), lambda i, j: (0, j)),
           pl.BlockSpec((tm, tn), lambda i, j: (i, j)),
           pl.BlockSpec((1, 1, tn), lambda i, j: (i // per_b, 0, j))],
        out_specs=pl.BlockSpec((tm, tn), lambda i, j: (i, j)),
        compiler_params=_params(("parallel", "parallel")),
        name=name,
    )(*a_parts, w, x2, gate.reshape(bsz, 1, n))


def _gdn_kernel(q_ref, k_ref, v_ref, z_ref, ba_ref, cw_ref, alog_ref, dtb_ref, nw_ref, o_ref,
                xbuf_ref, s_ref):
    c, d, nh = GDN_CHUNK, HEAD_DIM, GDN_HEADS
    width = nh * d
    hist = SUBLANES

    @pl.when(pl.program_id(1) == 0)
    def _():
        xbuf_ref[:, 0:hist, :] = jnp.zeros((3, hist, width), F32)
        s_ref[...] = jnp.zeros_like(s_ref)

    ba = ba_ref[...]
    beta = jax.nn.sigmoid(ba[:, 0:nh])
    g = -jnp.exp(alog_ref[...]) * jax.nn.softplus(ba[:, nh:2 * nh] + dtb_ref[...])
    r_i = lax.broadcasted_iota(I32, (c, c), 0)
    c_i = lax.broadcasted_iota(I32, (c, c), 1)
    causal = r_i >= c_i
    strict = r_i > c_i
    gam = jnp.dot(causal.astype(F32), g, precision=lax.Precision.HIGHEST, preferred_element_type=F32)
    gam_t = lax.dot_general(g, (r_i <= c_i).astype(F32), TN_DIMS, precision=lax.Precision.HIGHEST,
                            preferred_element_type=F32)
    g_last = gam[c - 1:c, :]
    egam = jnp.exp(gam)
    bexp = beta * egam
    kscale = jnp.exp(g_last - gam)
    eg_last = jnp.exp(g_last)
    nw = nw_ref[...]

    hsl = [slice(h * d, (h + 1) * d) for h in range(nh)]
    hcol = [slice(h, h + 1) for h in range(nh)]
    dot16 = lambda a, b: jnp.dot(a.astype(BF16), b.astype(BF16), preferred_element_type=F32)

    def split(a):
        hi = a.astype(BF16).astype(F32)
        return hi, a - hi

    as_lhs = lambda sp: jnp.concatenate([sp[0], sp[0], sp[1]], axis=1).astype(BF16)
    as_rhs = lambda sp: jnp.concatenate([sp[0], sp[1], sp[0]], axis=0).astype(BF16)
    dot32 = lambda lhs, rhs: jnp.dot(lhs, rhs, preferred_element_type=F32)

    base = 8
    same = lambda s: (r_i // s) == (c_i // s)
    eye = (r_i == c_i).astype(F32)

    for g0 in range(0, nh, GDN_HEAD_GROUP):
        heads = range(g0, g0 + GDN_HEAD_GROUP)
        cols = slice(g0 * d, (g0 + GDN_HEAD_GROUP) * d)
        conv = []
        for i, src in enumerate((q_ref, k_ref, v_ref)):
            wcols = slice(i * width + g0 * d, i * width + (g0 + GDN_HEAD_GROUP) * d)
            xbuf_ref[i, hist:hist + c, cols] = src[:, cols]
            acc = cw_ref[GDN_CONV - 1:GDN_CONV, wcols] * src[:, cols]
            for j in range(GDN_CONV - 1):
                shift = GDN_CONV - 1 - j
                acc += cw_ref[j:j + 1, wcols] * xbuf_ref[i, pl.ds(hist - shift, c), cols]
            conv.append(acc * jax.nn.sigmoid(acc))
            xbuf_ref[i, 0:hist, cols] = xbuf_ref[i, c:c + hist, cols]
        cq, ck, cv = conv
        gsl = {h: slice((h - g0) * d, (h - g0 + 1) * d) for h in heads}
        qn, kn, qk, lower = {}, {}, {}, {}
        for h in heads:
            qh = cq[:, gsl[h]]
            kh = ck[:, gsl[h]]
            qh = qh * lax.rsqrt(jnp.sum(qh * qh, axis=-1, keepdims=True) + EPS) * (d ** -0.5)
            kh = kh * lax.rsqrt(jnp.sum(kh * kh, axis=-1, keepdims=True) + EPS)
            kb = kh.astype(BF16)
            both = lax.dot_general(jnp.concatenate([qh.astype(BF16), kb], axis=0), kb, NT_DIMS,
                                   preferred_element_type=F32)
            diff = gam[:, hcol[h]] - gam_t[h:h + 1, :]
            dec = jnp.where(causal, jnp.exp(jnp.where(causal, diff, 0.0)), 0.0)
            qn[h] = qh
            kn[h] = kh
            qk[h] = both[:c] * dec
            lower[h] = jnp.where(strict, beta[:, hcol[h]] * both[c:] * dec, 0.0)

        diag = {h: jnp.where(same(base), lower[h], 0.0) for h in heads}
        dsp = {h: split(diag[h]) for h in heads}
        dlhs = {h: as_lhs(dsp[h]) for h in heads}
        p2 = {h: dot32(dlhs[h], as_rhs(dsp[h])) for h in heads}
        psp = {h: split(p2[h]) for h in heads}
        prhs = {h: as_rhs(psp[h]) for h in heads}
        qinv = {h: p2[h] - diag[h] - dot32(dlhs[h], prhs[h]) for h in heads}
        p4 = {h: dot32(as_lhs(psp[h]), prhs[h]) for h in heads}
        inv = {h: eye + (qinv[h] + p4[h] + dot32(as_lhs(split(qinv[h])), as_rhs(split(p4[h]))))
               for h in heads}
        size = base
        while size < c:
            off_mask = same(2 * size) & jnp.logical_not(same(size))
            t1 = {h: dot16(jnp.where(off_mask, lower[h], 0.0), inv[h]) for h in heads}
            inv = {h: inv[h] - dot16(inv[h], t1[h]) for h in heads}
            size *= 2

        sol = {}
        for h in heads:
            rhs = jnp.concatenate([cv[:, gsl[h]] * beta[:, hcol[h]], kn[h] * bexp[:, hcol[h]]], axis=1)
            sol[h] = rhs + dot16(inv[h] - eye, rhs)
        wq = {h: dot16(jnp.concatenate([sol[h][:, d:], qn[h] * egam[:, hcol[h]]], axis=0), s_ref[h])
              for h in heads}
        vnew = {h: (sol[h][:, :d] - wq[h][:c]).astype(BF16) for h in heads}
        for h in heads:
            k_dec = kn[h] * kscale[:, hcol[h]]
            s_ref[h] = s_ref[h] * eg_last[:, hcol[h]] + lax.dot_general(
                k_dec.astype(BF16), vnew[h], TN_DIMS, preferred_element_type=F32)
        for h in heads:
            o = wq[h][c:] + jnp.dot(qk[h].astype(BF16), vnew[h], preferred_element_type=F32)
            y = o * lax.rsqrt(jnp.mean(o * o, axis=-1, keepdims=True) + EPS) * nw
            zh = z_ref[:, hsl[h]]
            o_ref[:, hsl[h]] = (y * (zh * jax.nn.sigmoid(zh))).astype(o_ref.dtype)


def _gdn(pa, pb, ba_blk, conv_w, a_log, dt_bias, norm_w, seq):
    bsz = pa.shape[0]
    c, d, nh = GDN_CHUNK, HEAD_DIM, GDN_HEADS
    width = nh * d
    col_spec = lambda j: pl.BlockSpec((None, c, width), lambda b, n: (b, n, j))
    full = lambda shape: pl.BlockSpec(shape, lambda b, n: (0,) * len(shape))
    return pl.pallas_call(
        _gdn_kernel,
        out_shape=jax.ShapeDtypeStruct((bsz, seq, width), BF16),
        grid=(bsz, seq // c),
        in_specs=[col_spec(0), col_spec(1), col_spec(2), col_spec(3),
                  pl.BlockSpec((None, c, LANES), lambda b, n: (b, n, ba_blk)),
                  full((GDN_CONV, 3 * width)), full((1, nh)), full((1, nh)), full((1, d))],
        out_specs=pl.BlockSpec((None, c, width), lambda b, n: (b, n, 0)),
        scratch_shapes=[pltpu.VMEM((3, 8 + c, width), F32), pltpu.VMEM((nh, d, d), F32)],
        compiler_params=_params(("parallel", "arbitrary")),
        name="gdn_chunked",
    )(pa, pa, pa, pa, pb, conv_w, a_log.reshape(1, nh), dt_bias.reshape(1, nh), norm_w.reshape(1, d))


def _compress_kernel(x_ref, pos_ref, w1_ref, w2_ref, o_ref):
    n = o_ref.shape[0]
    d = x_ref.shape[1]
    acc_a = jnp.zeros((n, d), F32)
    acc_b = jnp.zeros((n, d), F32)
    for j in range(CMP_STRIDE):
        xj = x_ref[pl.ds(j, n, stride=CMP_STRIDE), :]
        lo = (xj + pos_ref[j:j + 1, :]).astype(BF16)
        hi = (xj + pos_ref[CMP_STRIDE + j:CMP_STRIDE + j + 1, :]).astype(BF16)
        acc_a += jnp.dot(lo, w1_ref[j * d:(j + 1) * d, :].astype(BF16), preferred_element_type=F32)
        acc_b += jnp.dot(hi, w1_ref[(CMP_STRIDE + j) * d:(CMP_STRIDE + j + 1) * d, :].astype(BF16),
                         preferred_element_type=F32)
    pre = acc_a + pltpu.roll(acc_b, n - 1, 0)
    hid = (pre * jax.nn.sigmoid(pre)).astype(BF16)
    out = jnp.dot(hid, w2_ref[...].astype(BF16), preferred_element_type=F32)
    row = lax.broadcasted_iota(I32, (n, d), 0)
    o_ref[...] = jnp.where(row < n - 1, out, 0.0).astype(o_ref.dtype)


def _compress(kvc, pos, w1, w2, seq):
    bsz = kvc.shape[0]
    g = NSA_KV_HEADS
    n = seq // CMP_STRIDE
    d = HEAD_DIM
    return pl.pallas_call(
        _compress_kernel,
        out_shape=jax.ShapeDtypeStruct((2, bsz, g, n, d), BF16),
        grid=(2, bsz, g),
        in_specs=[pl.BlockSpec((None, seq, d), lambda s, b, h: (b, 0, s * NSA_KV_HEADS + h)),
                  pl.BlockSpec((None, CMP_BLOCK, d), lambda s, b, h: (s, 0, 0)),
                  pl.BlockSpec((None, CMP_BLOCK * d, d), lambda s, b, h: (s, 0, 0)),
                  pl.BlockSpec((None, d, d), lambda s, b, h: (s, 0, 0))],
        out_specs=pl.BlockSpec((None, None, None, n, d), lambda s, b, h: (s, b, h, 0, 0)),
        compiler_params=_params(("parallel", "parallel", "parallel")),
        name="nsa_compress",
    )(kvc, pos, w1, w2)


def _softmax_rows(s, mask):
    s = jnp.where(mask, s, NEG_INF)
    m = jnp.max(s, axis=-1, keepdims=True)
    m = jnp.where(jnp.abs(m) < jnp.inf, m, 0.0)
    e = jnp.exp(s - m)
    d = jnp.sum(e, axis=-1, keepdims=True)
    return e * (1.0 / jnp.where(d > 0, d, 1.0))


def _nsa_key_aux(seq):
    key = jnp.arange(seq, dtype=I32)[:, None]
    lane = jnp.arange(LANES, dtype=I32)[None, :]
    onehot = (lane == key // SEL_BLOCK) & (lane < AUX_SEL_LANES)
    hi = (lane >= AUX_SEL_LANES) & (lane < AUX_SEL_LANES + 3)
    lo = (lane >= AUX_SEL_LANES + 3) & (lane < AUX_SEL_LANES + 6)
    aux = jnp.where(onehot, 1, 0) + jnp.where(hi, (key // 64) * 64, 0) + jnp.where(lo, key % 64, 0)
    return aux.astype(BF16)


def _nsa_query_aux(slopes):
    s1 = slopes.astype(BF16).astype(F32)
    s2 = (slopes - s1).astype(BF16).astype(F32)
    s3 = (slopes - s1 - s2).astype(BF16).astype(F32)
    pieces = jnp.stack([s1, s2, s3, s1, s2, s3], axis=-1)
    aux = jnp.zeros((slopes.shape[0], LANES), F32).at[:, AUX_SEL_LANES:AUX_SEL_LANES + 6].set(pieces)
    return aux.reshape(NSA_KV_HEADS, NSA_REP, LANES)


def _nsa_overlap(n_sel, n_cmp):
    lo = jnp.arange(n_sel, dtype=I32)[:, None] * SEL_BLOCK
    start = jnp.arange(n_cmp, dtype=I32)[None, :] * CMP_STRIDE
    return ((start <= lo + (SEL_BLOCK - 1)) & (start + (CMP_BLOCK - 1) >= lo)).astype(BF16)


def _nsa_kernel(slopes_ref, qaux_ref, kaux_ref, ov_ref, q_ref, kc_ref, vc_ref, ksel_ref, vs_ref, kwin_ref,
                vw_ref, gate_ref, o_ref, ks_ref, kw_ref):
    seq = ks_ref.shape[0]
    tq, tk, rep, d = NSA_TQ, NSA_TK, NSA_REP, HEAD_DIM
    rows = rep * tq
    g = pl.program_id(1)
    t0 = pl.program_id(2) * tq

    @pl.when(pl.program_id(2) == 0)
    def _():
        ks_ref[:, 0:d] = ksel_ref[...]
        ks_ref[:, d:2 * d] = kaux_ref[...]
        kw_ref[:, 0:d] = kwin_ref[...]
        kw_ref[:, d:2 * d] = kaux_ref[...]

    q = q_ref[...] * (d ** -0.5)
    q = jnp.concatenate([q[:, r * d:(r + 1) * d] for r in range(rep)], axis=0)
    qs = q.astype(BF16)
    qs2 = (q * LOG2E).astype(BF16)
    slope = jnp.concatenate([jnp.full((tq, 1), slopes_ref[g * rep + r], F32) for r in range(rep)], axis=0)
    row = lax.broadcasted_iota(I32, (rows, 1), 0)
    t_col = t0 + (row & (tq - 1))

    ncp = kc_ref.shape[0]
    n_idx = lax.broadcasted_iota(I32, (rows, ncp), 1)
    dist_c = t_col - (n_idx * CMP_STRIDE + (CMP_BLOCK - 1))
    s_c = lax.dot_general(qs, kc_ref[...], NT_DIMS, preferred_element_type=F32)
    s_c = s_c - slope * dist_c.astype(F32)
    p_c = _softmax_rows(s_c, dist_c >= 0).astype(BF16)
    o_c = jnp.dot(p_c, vc_ref[...], preferred_element_type=F32)

    n_sel = seq // SEL_BLOCK
    ov_t = ov_ref[...]
    imp = jnp.zeros((n_sel, tq), F32)
    for r in range(rep):
        imp += lax.dot_general(ov_t, p_c[r * tq:(r + 1) * tq, :], NT_DIMS, preferred_element_type=F32)
    blk = lax.broadcasted_iota(I32, (n_sel, tq), 0)
    cur = lax.shift_right_logical(t0 + lax.broadcasted_iota(I32, (n_sel, tq), 1), SEL_BLOCK.bit_length() - 1)
    forced = (blk == 0) | ((cur - blk) < SEL_LOCAL)
    imp = jnp.where(forced, jnp.inf, imp)
    imp = jnp.where(blk <= cur, imp, NEG_INF)
    rank = jnp.zeros((n_sel, tq), I32)
    for i in range(n_sel):
        ri = imp[i:i + 1, :]
        rank += ((ri > imp) | ((ri == imp) & (blk > i))).astype(I32)
    sel_t = (rank < min(SEL_TOPN, n_sel)).astype(F32)

    to_lanes = (lax.broadcasted_iota(I32, (n_sel, LANES), 0)
                == lax.broadcasted_iota(I32, (n_sel, LANES), 1)).astype(BF16)
    picked = lax.dot_general(sel_t.astype(BF16), to_lanes, TN_DIMS, preferred_element_type=F32)
    lane = lax.broadcasted_iota(I32, (tq, LANES), 1)
    sel_bias = jnp.where(lane < AUX_SEL_LANES, (picked - 1.0) * SEL_BIAS, 0.0)
    qaux = qaux_ref[...]
    qx_sel = jnp.concatenate(
        [qs2, jnp.concatenate([sel_bias + qaux[r:r + 1, :] for r in range(rep)], axis=0).astype(BF16)], axis=1)
    qx_win = jnp.concatenate(
        [qs2, jnp.concatenate([jnp.broadcast_to(qaux[r:r + 1, :], (tq, LANES)) for r in range(rep)],
                              axis=0).astype(BF16)], axis=1)

    def sel_tile(j, carry, diagonal):
        m, l, acc = carry
        k0 = pl.multiple_of(j * tk, tk)
        s = lax.dot_general(qx_sel, ks_ref[pl.ds(k0, tk), :], NT_DIMS, preferred_element_type=F32)
        if diagonal:
            ahead = (lax.broadcasted_iota(I32, (rows, tk), 1)
                     - (lax.broadcasted_iota(I32, (rows, tk), 0) & (tq - 1)))
            s = jnp.where(ahead <= t0 - k0, s, -SEL_BIAS)
        m_new = jnp.maximum(m, jnp.max(s, axis=-1, keepdims=True))
        alpha = jnp.exp2(m - m_new)
        p = jnp.exp2(s - m_new)
        l = alpha * l + jnp.sum(p, axis=-1, keepdims=True)
        acc = alpha * acc + jnp.dot(p.astype(BF16), vs_ref[pl.ds(k0, tk), :], preferred_element_type=F32)
        return m_new, l, acc

    n_tiles = (t0 + tq + tk - 1) // tk
    init = (jnp.full((rows, 1), NEG_INF, F32), jnp.zeros((rows, 1), F32), jnp.zeros((rows, d), F32))
    carry = lax.fori_loop(0, n_tiles - 1, lambda j, c: sel_tile(j, c, False), init)
    _, l_s, acc_s = sel_tile(n_tiles - 1, carry, True)
    o_s = acc_s * (1.0 / l_s)

    wk = tq + WINDOW
    start = pl.multiple_of(jnp.maximum(t0 - WINDOW, 0), tq)
    s_w = lax.dot_general(qx_win, kw_ref[pl.ds(start, wk), :], NT_DIMS, preferred_element_type=F32)
    dist_w = t_col - (start + lax.broadcasted_iota(I32, (rows, wk), 1))
    s_w = jnp.where((dist_w >= 0) & (dist_w < WINDOW), s_w, NEG_INF)
    e_w = jnp.exp2(s_w - jnp.max(s_w, axis=-1, keepdims=True))
    d_w = jnp.sum(e_w, axis=-1, keepdims=True)
    o_w = jnp.dot(e_w.astype(BF16), vw_ref[pl.ds(start, wk), :], preferred_element_type=F32) * (1.0 / d_w)

    gt = jax.nn.sigmoid(gate_ref[...])
    for r in range(rep):
        sl = slice(r * tq, (r + 1) * tq)
        o = (gt[:, 3 * r:3 * r + 1] * o_c[sl] + gt[:, 3 * r + 1:3 * r + 2] * o_s[sl]
             + gt[:, 3 * r + 2:3 * r + 3] * o_w[sl])
        o_ref[:, r * d:(r + 1) * d] = o.astype(o_ref.dtype)


def _nsa_attention(qa, q_col0, cmp, kvb, gates, slopes, seq):
    bsz = qa.shape[0]
    g, rep, d, tq = NSA_KV_HEADS, NSA_REP, HEAD_DIM, NSA_TQ
    ncp = seq // CMP_STRIDE
    n_sel = seq // SEL_BLOCK
    assert n_sel <= AUX_SEL_LANES and seq % NSA_TK == 0 and seq >= tq + WINDOW
    qblk0 = q_col0 // (rep * d)
    kv_spec = lambda s: pl.BlockSpec((None, seq, d), lambda b, h, i: (b, 0, s * NSA_KV_HEADS + h))
    cmp_spec = lambda s: pl.BlockSpec((None, None, None, ncp, d), lambda b, h, i: (s, b, h, 0, 0))
    return pl.pallas_call(
        _nsa_kernel,
        out_shape=jax.ShapeDtypeStruct((bsz, seq, NSA_DQ), BF16),
        grid=(bsz, g, seq // tq),
        in_specs=[pl.BlockSpec(memory_space=pltpu.SMEM),
                  pl.BlockSpec((None, rep, LANES), lambda b, h, i: (h, 0, 0)),
                  pl.BlockSpec((seq, LANES), lambda b, h, i: (0, 0)),
                  pl.BlockSpec((n_sel, ncp), lambda b, h, i: (0, 0)),
                  pl.BlockSpec((None, tq, rep * d), lambda b, h, i: (b, i, qblk0 + h)),
                  cmp_spec(0), cmp_spec(1), kv_spec(0), kv_spec(1), kv_spec(2), kv_spec(3),
                  pl.BlockSpec((None, None, tq, 3 * rep), lambda b, h, i: (b, h, i, 0))],
        out_specs=pl.BlockSpec((None, tq, rep * d), lambda b, h, i: (b, i, h)),
        scratch_shapes=[pltpu.VMEM((seq, 2 * d), BF16), pltpu.VMEM((seq, 2 * d), BF16)],
        compiler_params=_params(("parallel", "parallel", "arbitrary")),
        name="nsa_attention",
    )(slopes, _nsa_query_aux(slopes * LOG2E), _nsa_key_aux(seq), _nsa_overlap(n_sel, ncp), qa, cmp, cmp, kvb, kvb, kvb, kvb, gates)


def _split_columns(proj):
    out, start = [], 0
    for size in IN_SIZES:
        out.append(proj[..., start:start + size])
        start += size
    return out


def kernel(x, c, ada_w, ada_b, norm1_w, w_in, gdn_conv_w, gdn_a_log, gdn_dt_bias, gdn_norm_w,
           cmp_pos_k, cmp_w1_k, cmp_w2_k, cmp_pos_v, cmp_w1_v, cmp_w2_v, w_out, norm2_w,
           w_up, w_down, final_norm_w):
    bsz, seq, d = x.shape
    depth = ada_w.shape[0]
    m = bsz * seq
    x2 = x.reshape(m, d)
    slopes = 2.0 ** (-8.0 * jnp.arange(1, NSA_HEADS + 1, dtype=F32) / NSA_HEADS)
    for i in range(depth):
        mod = _modulation(c, ada_w[i], ada_b[i])
        sh1, sc1, g1, sh2, sc2, g2 = jnp.split(mod, 6, axis=-1)
        h = _norm_mod(x2, norm1_w[i], sc1, sh1, seq, BF16)
        cols = _split_columns(w_in[i])
        small = jnp.concatenate([cols[4], cols[5], cols[13]], axis=-1)
        small = jnp.pad(small, ((0, 0), (0, LANES - small.shape[1])))
        w_a = jnp.concatenate(cols[0:4] + [cols[6]], axis=-1).astype(BF16)
        w_b = jnp.concatenate(cols[7:9] + [small], axis=-1).astype(BF16)
        w_c = jnp.concatenate(cols[9:13], axis=-1).astype(BF16)
        pa, w_up_bf16 = _matmul(h, w_a, tm=MM_TILE, tn=MM_TILE, out_dtype=F32, name="in_proj_a", cast_to_bf16=w_up[i])
        pa = pa.reshape(bsz, seq, -1)
        pb = _matmul(h, w_b, tm=MM_TILE, tn=w_b.shape[1], out_dtype=F32, name="in_proj_b").reshape(bsz, seq, -1)
        pc = _matmul(h, w_c, tm=MM_TILE, tn=MM_TILE, out_dtype=BF16, name="in_proj_c").reshape(bsz, seq, -1)

        o_a = _gdn(pa, pb, 2 * NSA_DKV // LANES, gdn_conv_w[i], gdn_a_log[i], gdn_dt_bias[i], gdn_norm_w[i], seq)
        ngate = pb[..., 2 * NSA_DKV + 2 * GDN_HEADS:2 * NSA_DKV + 2 * GDN_HEADS + 3 * NSA_HEADS]

        cmp = _compress(pb, jnp.stack([cmp_pos_k[i], cmp_pos_v[i]]), jnp.stack([cmp_w1_k[i], cmp_w1_v[i]]),
                        jnp.stack([cmp_w2_k[i], cmp_w2_v[i]]), seq)
        gates = ngate.reshape(bsz, seq, NSA_KV_HEADS, 3 * NSA_REP).transpose(0, 2, 1, 3)
        o_b = _nsa_attention(pa, 4 * GDN_DK, cmp, pc, gates, slopes, seq)

        x2 = _matmul_parts_residual([o_a.reshape(m, -1), o_b.reshape(m, -1)], w_out[i].astype(BF16), x2, g1, seq,
                                    tm=MM_TILE, tn=MM_TILE, name="out_proj")
        h2 = _norm_mod(x2, norm2_w[i], sc2, sh2, seq, BF16)
        hid, w_down_bf16 = _matmul(h2, w_up_bf16, tm=MM_TILE, tn=MM_TILE, out_dtype=BF16, relu2=True, name="mlp_up",
                                   cast_to_bf16=w_down[i])
        x2 = _matmul_residual(hid, w_down_bf16, x2, g2, seq,
                              tm=MM_TILE, tn=MM_TILE, tk=MM_TILE_K, name="mlp_down")
    return _final_norm(x2, final_norm_w).reshape(bsz, seq, d)
```

```python
import functools

import jax
import jax.numpy as jnp
from jax import lax
from jax.experimental import pallas as pl
from jax.experimental.pallas import tpu as pltpu

F32 = jnp.float32
BF16 = jnp.bfloat16
I32 = jnp.int32

HEAD_DIM = 128
GDN_HEADS = 16
NSA_HEADS = 16
NSA_KV_HEADS = 4
NSA_REP = NSA_HEADS // NSA_KV_HEADS
GDN_DK = GDN_HEADS * HEAD_DIM
NSA_DQ = NSA_HEADS * HEAD_DIM
NSA_DKV = NSA_KV_HEADS * HEAD_DIM
GDN_CONV = 4
GDN_CHUNK = 128
GDN_HEAD_GROUP = 16
CMP_BLOCK = 32
CMP_STRIDE = 16
SEL_BLOCK = 64
SEL_TOPN = 16
SEL_LOCAL = 2
WINDOW = 512
EPS = 1e-6
IN_SIZES = (GDN_DK, GDN_DK, GDN_DK, GDN_DK, GDN_HEADS, GDN_HEADS,
            NSA_DQ, NSA_DKV, NSA_DKV, NSA_DKV, NSA_DKV, NSA_DKV, NSA_DKV, 3 * NSA_HEADS)

VMEM_LIMIT_V7X = 56 * 1024 * 1024
LANES = 128
SUBLANES = 8
BF16_SUBLANES = 16
MM_TILE = 1024
MM_TILE_K = 4096
NORM_ROWS = 512
MOD_TILE_N = 512
NSA_TQ = 512
NSA_TK = 1024
NEG_INF = float("-inf")
AUX_SEL_LANES = 64
SEL_BIAS = 2.0 ** 100
LOG2E = 1.4426950408889634
NT_DIMS = (((1,), (1,)), ((), ()))
TN_DIMS = (((0,), (0,)), ((), ()))


def _params(sem):
    return pltpu.CompilerParams(dimension_semantics=sem, vmem_limit_bytes=VMEM_LIMIT_V7X)


def _mod_kernel(c_ref, w_ref, b_ref, o_ref):
    c = c_ref[...]
    s = (c * jax.nn.sigmoid(c)).astype(BF16)
    o_ref[...] = jnp.dot(s, w_ref[...].astype(BF16), preferred_element_type=F32) + b_ref[...]


def _modulation(c, w, b):
    bsz, d = c.shape
    n = w.shape[1]
    rows = SUBLANES
    cp = jnp.zeros((rows, d), F32).at[:bsz].set(c)
    tn = MOD_TILE_N
    out = pl.pallas_call(
        _mod_kernel,
        out_shape=jax.ShapeDtypeStruct((rows, n), F32),
        grid=(n // tn,),
        in_specs=[pl.BlockSpec((rows, d), lambda j: (0, 0)),
                  pl.BlockSpec((d, tn), lambda j: (0, j)),
                  pl.BlockSpec((1, tn), lambda j: (0, j))],
        out_specs=pl.BlockSpec((rows, tn), lambda j: (0, j)),
        compiler_params=_params(("parallel",)),
        name="adaln_mod",
    )(cp, w, b.reshape(1, n))
    return out[:bsz]


def _norm_kernel(x_ref, g_ref, sc_ref, sh_ref, o_ref):
    x = x_ref[...]
    y = x * lax.rsqrt(jnp.mean(x * x, axis=-1, keepdims=True) + EPS) * g_ref[...]
    o_ref[...] = (y * (1.0 + sc_ref[0]) + sh_ref[0]).astype(o_ref.dtype)


def _final_norm_kernel(x_ref, g_ref, o_ref):
    x = x_ref[...]
    o_ref[...] = x * lax.rsqrt(jnp.mean(x * x, axis=-1, keepdims=True) + EPS) * g_ref[...]


def _norm_mod(x2, gain, scale, shift, seq, out_dtype):
    m, d = x2.shape
    tm = NORM_ROWS
    per_b = seq // tm
    bsz = m // seq
    return pl.pallas_call(
        _norm_kernel,
        out_shape=jax.ShapeDtypeStruct((m, d), out_dtype),
        grid=(m // tm,),
        in_specs=[pl.BlockSpec((tm, d), lambda i: (i, 0)),
                  pl.BlockSpec((1, d), lambda i: (0, 0)),
                  pl.BlockSpec((1, 1, d), lambda i: (i // per_b, 0, 0)),
                  pl.BlockSpec((1, 1, d), lambda i: (i // per_b, 0, 0))],
        out_specs=pl.BlockSpec((tm, d), lambda i: (i, 0)),
        compiler_params=_params(("parallel",)),
        name="rmsnorm_mod",
    )(x2, gain.reshape(1, d), scale.reshape(bsz, 1, d), shift.reshape(bsz, 1, d))


def _final_norm(x2, gain):
    m, d = x2.shape
    tm = NORM_ROWS
    return pl.pallas_call(
        _final_norm_kernel,
        out_shape=jax.ShapeDtypeStruct((m, d), F32),
        grid=(m // tm,),
        in_specs=[pl.BlockSpec((tm, d), lambda i: (i, 0)),
                  pl.BlockSpec((1, d), lambda i: (0, 0))],
        out_specs=pl.BlockSpec((tm, d), lambda i: (i, 0)),
        compiler_params=_params(("parallel",)),
        name="final_rmsnorm",
    )(x2, gain.reshape(1, d))


def _mm_kernel(a_ref, w_ref, *rest, relu2, side_cast):
    if side_cast:
        side_ref, o_ref, side_o_ref = rest
        side_o_ref[...] = side_ref[...].astype(side_o_ref.dtype)
    else:
        (o_ref,) = rest
    acc = jnp.dot(a_ref[...], w_ref[...], preferred_element_type=F32)
    if relu2:
        acc = jnp.square(jnp.maximum(acc, 0.0))
    o_ref[...] = acc.astype(o_ref.dtype)


def _matmul(a, w, *, tm, tn, out_dtype, relu2=False, name, cast_to_bf16=None):
    m, k = a.shape
    n = w.shape[1]
    grid = (m // tm, n // tn)
    in_specs = [pl.BlockSpec((tm, k), lambda i, j: (i, 0)),
                pl.BlockSpec((k, tn), lambda i, j: (0, j))]
    out_shape = jax.ShapeDtypeStruct((m, n), out_dtype)
    out_specs = pl.BlockSpec((tm, tn), lambda i, j: (i, j))
    operands = (a, w)
    if cast_to_bf16 is not None:
        rows, cols = cast_to_bf16.shape
        steps = grid[0] * grid[1]
        blocks = min(steps, rows // BF16_SUBLANES)
        while rows % blocks or (rows // blocks) % BF16_SUBLANES:
            blocks -= 1
        side_map = lambda i, j: (jnp.minimum(i * grid[1] + j, blocks - 1), 0)
        in_specs.append(pl.BlockSpec((rows // blocks, cols), side_map))
        out_shape = (out_shape, jax.ShapeDtypeStruct((rows, cols), BF16))
        out_specs = (out_specs, pl.BlockSpec((rows // blocks, cols), side_map))
        operands += (cast_to_bf16,)
    return pl.pallas_call(
        functools.partial(_mm_kernel, relu2=relu2, side_cast=cast_to_bf16 is not None),
        out_shape=out_shape,
        grid=grid,
        in_specs=in_specs,
        out_specs=out_specs,
        compiler_params=_params(("arbitrary", "arbitrary") if cast_to_bf16 is not None
                                else ("parallel", "parallel")),
        name=name,
    )(*operands)


def _mm_res_kernel(a_ref, w_ref, x_ref, g_ref, o_ref):
    k = pl.program_id(2)

    @pl.when(k == 0)
    def _():
        o_ref[...] = jnp.zeros_like(o_ref)

    o_ref[...] += jnp.dot(a_ref[...], w_ref[...], preferred_element_type=F32)

    @pl.when(k == pl.num_programs(2) - 1)
    def _():
        o_ref[...] = x_ref[...] + g_ref[0] * o_ref[...]


def _matmul_residual(a, w, x2, gate, seq, *, tm, tn, tk, name):
    m, kdim = a.shape
    n = w.shape[1]
    bsz = m // seq
    per_b = seq // tm
    return pl.pallas_call(
        _mm_res_kernel,
        out_shape=jax.ShapeDtypeStruct((m, n), F32),
        grid=(m // tm, n // tn, kdim // tk),
        in_specs=[pl.BlockSpec((tm, tk), lambda i, j, k: (i, k)),
                  pl.BlockSpec((tk, tn), lambda i, j, k: (k, j)),
                  pl.BlockSpec((tm, tn), lambda i, j, k: (i, j)),
                  pl.BlockSpec((1, 1, tn), lambda i, j, k: (i // per_b, 0, j))],
        out_specs=pl.BlockSpec((tm, tn), lambda i, j, k: (i, j)),
        compiler_params=_params(("parallel", "parallel", "arbitrary")),
        name=name,
    )(a, w, x2, gate.reshape(bsz, 1, n))


def _mm_parts_res_kernel(*refs, nparts):
    a_refs = refs[:nparts]
    w_ref, x_ref, g_ref, o_ref = refs[nparts:]
    width = a_refs[0].shape[1]
    acc = jnp.dot(a_refs[0][...], w_ref[0:width, :], preferred_element_type=F32)
    for idx in range(1, nparts):
        acc += jnp.dot(a_refs[idx][...], w_ref[idx * width:(idx + 1) * width, :], preferred_element_type=F32)
    o_ref[...] = x_ref[...] + g_ref[0] * acc


def _matmul_parts_residual(a_parts, w, x2, gate, seq, *, tm, tn, name):
    nparts = len(a_parts)
    m, width = a_parts[0].shape
    assert all(a.shape == (m, width) for a in a_parts) and w.shape[0] == nparts * width
    n = w.shape[1]
    bsz = m // seq
    per_b = seq // tm
    return pl.pallas_call(
        functools.partial(_mm_parts_res_kernel, nparts=nparts),
        out_shape=jax.ShapeDtypeStruct((m, n), F32),
        grid=(m // tm, n // tn),
        in_specs=[pl.BlockSpec((tm, width), lambda i, j: (i, 0)) for _ in a_parts]
        + [pl.BlockSpec((nparts * width, tn), lambda i, j: (0, j)),
           pl.BlockSpec((tm, tn), lambda i, j: (i, j)),
           pl.BlockSpec((1, 1, tn), lambda i, j: (i // per_b, 0, j))],
        out_specs=pl.BlockSpec((tm, tn), lambda i, j: (i, j)),
        compiler_params=_params(("parallel", "parallel")),
        name=name,
    )(*a_parts, w, x2, gate.reshape(bsz, 1, n))


def _gdn_kernel(q_ref, k_ref, v_ref, z_ref, ba_ref, cw_ref, alog_ref, dtb_ref, nw_ref, o_ref,
                xbuf_ref, s_ref):
    c, d, nh = GDN_CHUNK, HEAD_DIM, GDN_HEADS
    width = nh * d
    hist = SUBLANES

    @pl.when(pl.program_id(1) == 0)
    def _():
        xbuf_ref[:, 0:hist, :] = jnp.zeros((3, hist, width), F32)
        s_ref[...] = jnp.zeros_like(s_ref)

    ba = ba_ref[...]
    beta = jax.nn.sigmoid(ba[:, 0:nh])
    g = -jnp.exp(alog_ref[...]) * jax.nn.softplus(ba[:, nh:2 * nh] + dtb_ref[...])
    r_i = lax.broadcasted_iota(I32, (c, c), 0)
    c_i = lax.broadcasted_iota(I32, (c, c), 1)
    causal = r_i >= c_i
    strict = r_i > c_i
    gam = jnp.dot(causal.astype(F32), g, precision=lax.Precision.HIGHEST, preferred_element_type=F32)
    gam_t = lax.dot_general(g, (r_i <= c_i).astype(F32), TN_DIMS, precision=lax.Precision.HIGHEST,
                            preferred_element_type=F32)
    g_last = gam[c - 1:c, :]
    egam = jnp.exp(gam)
    bexp = beta * egam
    kscale = jnp.exp(g_last - gam)
    eg_last = jnp.exp(g_last)
    nw = nw_ref[...]

    hsl = [slice(h * d, (h + 1) * d) for h in range(nh)]
    hcol = [slice(h, h + 1) for h in range(nh)]
    dot16 = lambda a, b: jnp.dot(a.astype(BF16), b.astype(BF16), preferred_element_type=F32)

    def split(a):
        hi = a.astype(BF16).astype(F32)
        return hi, a - hi

    as_lhs = lambda sp: jnp.concatenate([sp[0], sp[0], sp[1]], axis=1).astype(BF16)
    as_rhs = lambda sp: jnp.concatenate([sp[0], sp[1], sp[0]], axis=0).astype(BF16)
    dot32 = lambda lhs, rhs: jnp.dot(lhs, rhs, preferred_element_type=F32)

    base = 8
    same = lambda s: (r_i // s) == (c_i // s)
    eye = (r_i == c_i).astype(F32)

    for g0 in range(0, nh, GDN_HEAD_GROUP):
        heads = range(g0, g0 + GDN_HEAD_GROUP)
        cols = slice(g0 * d, (g0 + GDN_HEAD_GROUP) * d)
        conv = []
        for i, src in enumerate((q_ref, k_ref, v_ref)):
            wcols = slice(i * width + g0 * d, i * width + (g0 + GDN_HEAD_GROUP) * d)
            xbuf_ref[i, hist:hist + c, cols] = src[:, cols]
            acc = cw_ref[GDN_CONV - 1:GDN_CONV, wcols] * src[:, cols]
            for j in range(GDN_CONV - 1):
                shift = GDN_CONV - 1 - j
                acc += cw_ref[j:j + 1, wcols] * xbuf_ref[i, pl.ds(hist - shift, c), cols]
            conv.append(acc * jax.nn.sigmoid(acc))
            xbuf_ref[i, 0:hist, cols] = xbuf_ref[i, c:c + hist, cols]
        cq, ck, cv = conv
        gsl = {h: slice((h - g0) * d, (h - g0 + 1) * d) for h in heads}
        qn, kn, qk, lower = {}, {}, {}, {}
        for h in heads:
            qh = cq[:, gsl[h]]
            kh = ck[:, gsl[h]]
            qh = qh * lax.rsqrt(jnp.sum(qh * qh, axis=-1, keepdims=True) + EPS) * (d ** -0.5)
            kh = kh * lax.rsqrt(jnp.sum(kh * kh, axis=-1, keepdims=True) + EPS)
            kb = kh.astype(BF16)
            both = lax.dot_general(jnp.concatenate([qh.astype(BF16), kb], axis=0), kb, NT_DIMS,
                                   preferred_element_type=F32)
            diff = gam[:, hcol[h]] - gam_t[h:h + 1, :]
            dec = jnp.where(causal, jnp.exp(jnp.where(causal, diff, 0.0)), 0.0)
            qn[h] = qh
            kn[h] = kh
            qk[h] = both[:c] * dec
            lower[h] = jnp.where(strict, beta[:, hcol[h]] * both[c:] * dec, 0.0)

        diag = {h: jnp.where(same(base), lower[h], 0.0) for h in heads}
        dsp = {h: split(diag[h]) for h in heads}
        dlhs = {h: as_lhs(dsp[h]) for h in heads}
        p2 = {h: dot32(dlhs[h], as_rhs(dsp[h])) for h in heads}
        psp = {h: split(p2[h]) for h in heads}
        prhs = {h: as_rhs(psp[h]) for h in heads}
        qinv = {h: p2[h] - diag[h] - dot32(dlhs[h], prhs[h]) for h in heads}
        p4 = {h: dot32(as_lhs(psp[h]), prhs[h]) for h in heads}
        inv = {h: eye + (qinv[h] + p4[h] + dot32(as_lhs(split(qinv[h])), as_rhs(split(p4[h]))))
               for h in heads}
        size = base
        while size < c:
            off_mask = same(2 * size) & jnp.logical_not(same(size))
            t1 = {h: dot16(jnp.where(off_mask, lower[h], 0.0), inv[h]) for h in heads}
            inv = {h: inv[h] - dot16(inv[h], t1[h]) for h in heads}
            size *= 2

        sol = {}
        for h in heads:
            rhs = jnp.concatenate([cv[:, gsl[h]] * beta[:, hcol[h]], kn[h] * bexp[:, hcol[h]]], axis=1)
            sol[h] = rhs + dot16(inv[h] - eye, rhs)
        wq = {h: dot16(jnp.concatenate([sol[h][:, d:], qn[h] * egam[:, hcol[h]]], axis=0), s_ref[h])
              for h in heads}
        vnew = {h: (sol[h][:, :d] - wq[h][:c]).astype(BF16) for h in heads}
        for h in heads:
            k_dec = kn[h] * kscale[:, hcol[h]]
            s_ref[h] = s_ref[h] * eg_last[:, hcol[h]] + lax.dot_general(
                k_dec.astype(BF16), vnew[h], TN_DIMS, preferred_element_type=F32)
        for h in heads:
            o = wq[h][c:] + jnp.dot(qk[h].astype(BF16), vnew[h], preferred_element_type=F32)
            y = o * lax.rsqrt(jnp.mean(o * o, axis=-1, keepdims=True) + EPS) * nw
            zh = z_ref[:, hsl[h]]
            o_ref[:, hsl[h]] = (y * (zh * jax.nn.sigmoid(zh))).astype(o_ref.dtype)


def _gdn(pa, pb, ba_blk, conv_w, a_log, dt_bias, norm_w, seq):
    bsz = pa.shape[0]
    c, d, nh = GDN_CHUNK, HEAD_DIM, GDN_HEADS
    width = nh * d
    col_spec = lambda j: pl.BlockSpec((None, c, width), lambda b, n: (b, n, j))
    full = lambda shape: pl.BlockSpec(shape, lambda b, n: (0,) * len(shape))
    return pl.pallas_call(
        _gdn_kernel,
        out_shape=jax.ShapeDtypeStruct((bsz, seq, width), BF16),
        grid=(bsz, seq // c),
        in_specs=[col_spec(0), col_spec(1), col_spec(2), col_spec(3),
                  pl.BlockSpec((None, c, LANES), lambda b, n: (b, n, ba_blk)),
                  full((GDN_CONV, 3 * width)), full((1, nh)), full((1, nh)), full((1, d))],
        out_specs=pl.BlockSpec((None, c, width), lambda b, n: (b, n, 0)),
        scratch_shapes=[pltpu.VMEM((3, 8 + c, width), F32), pltpu.VMEM((nh, d, d), F32)],
        compiler_params=_params(("parallel", "arbitrary")),
        name="gdn_chunked",
    )(pa, pa, pa, pa, pb, conv_w, a_log.reshape(1, nh), dt_bias.reshape(1, nh), norm_w.reshape(1, d))


def _compress_kernel(x_ref, pos_ref, w1_ref, w2_ref, o_ref):
    n = o_ref.shape[0]
    d = x_ref.shape[1]
    acc_a = jnp.zeros((n, d), F32)
    acc_b = jnp.zeros((n, d), F32)
    for j in range(CMP_STRIDE):
        xj = x_ref[pl.ds(j, n, stride=CMP_STRIDE), :]
        lo = (xj + pos_ref[j:j + 1, :]).astype(BF16)
        hi = (xj + pos_ref[CMP_STRIDE + j:CMP_STRIDE + j + 1, :]).astype(BF16)
        acc_a += jnp.dot(lo, w1_ref[j * d:(j + 1) * d, :].astype(BF16), preferred_element_type=F32)
        acc_b += jnp.dot(hi, w1_ref[(CMP_STRIDE + j) * d:(CMP_STRIDE + j + 1) * d, :].astype(BF16),
                         preferred_element_type=F32)
    pre = acc_a + pltpu.roll(acc_b, n - 1, 0)
    hid = (pre * jax.nn.sigmoid(pre)).astype(BF16)
    out = jnp.dot(hid, w2_ref[...].astype(BF16), preferred_element_type=F32)
    row = lax.broadcasted_iota(I32, (n, d), 0)
    o_ref[...] = jnp.where(row < n - 1, out, 0.0).astype(o_ref.dtype)


def _compress(kvc, pos, w1, w2, seq):
    bsz = kvc.shape[0]
    g = NSA_KV_HEADS
    n = seq // CMP_STRIDE
    d = HEAD_DIM
    return pl.pallas_call(
        _compress_kernel,
        out_shape=jax.ShapeDtypeStruct((2, bsz, g, n, d), BF16),
        grid=(2, bsz, g),
        in_specs=[pl.BlockSpec((None, seq, d), lambda s, b, h: (b, 0, s * NSA_KV_HEADS + h)),
                  pl.BlockSpec((None, CMP_BLOCK, d), lambda s, b, h: (s, 0, 0)),
                  pl.BlockSpec((None, CMP_BLOCK * d, d), lambda s, b, h: (s, 0, 0)),
                  pl.BlockSpec((None, d, d), lambda s, b, h: (s, 0, 0))],
        out_specs=pl.BlockSpec((None, None, None, n, d), lambda s, b, h: (s, b, h, 0, 0)),
        compiler_params=_params(("parallel", "parallel", "parallel")),
        name="nsa_compress",
    )(kvc, pos, w1, w2)


def _softmax_rows(s, mask):
    s = jnp.where(mask, s, NEG_INF)
    m = jnp.max(s, axis=-1, keepdims=True)
    m = jnp.where(jnp.abs(m) < jnp.inf, m, 0.0)
    e = jnp.exp(s - m)
    d = jnp.sum(e, axis=-1, keepdims=True)
    return e * (1.0 / jnp.where(d > 0, d, 1.0))


def _nsa_key_aux(seq):
    key = jnp.arange(seq, dtype=I32)[:, None]
    lane = jnp.arange(LANES, dtype=I32)[None, :]
    onehot = (lane == key // SEL_BLOCK) & (lane < AUX_SEL_LANES)
    hi = (lane >= AUX_SEL_LANES) & (lane < AUX_SEL_LANES + 3)
    lo = (lane >= AUX_SEL_LANES + 3) & (lane < AUX_SEL_LANES + 6)
    aux = jnp.where(onehot, 1, 0) + jnp.where(hi, (key // 64) * 64, 0) + jnp.where(lo, key % 64, 0)
    return aux.astype(BF16)


def _nsa_query_aux(slopes):
    s1 = slopes.astype(BF16).astype(F32)
    s2 = (slopes - s1).astype(BF16).astype(F32)
    s3 = (slopes - s1 - s2).astype(BF16).astype(F32)
    pieces = jnp.stack([s1, s2, s3, s1, s2, s3], axis=-1)
    aux = jnp.zeros((slopes.shape[0], LANES), F32).at[:, AUX_SEL_LANES:AUX_SEL_LANES + 6].set(pieces)
    return aux.reshape(NSA_KV_HEADS, NSA_REP, LANES)


def _nsa_overlap(n_sel, n_cmp):
    lo = jnp.arange(n_sel, dtype=I32)[:, None] * SEL_BLOCK
    start = jnp.arange(n_cmp, dtype=I32)[None, :] * CMP_STRIDE
    return ((start <= lo + (SEL_BLOCK - 1)) & (start + (CMP_BLOCK - 1) >= lo)).astype(BF16)


def _nsa_kernel(slopes_ref, qaux_ref, kaux_ref, ov_ref, q_ref, kc_ref, vc_ref, ksel_ref, vs_ref, kwin_ref,
                vw_ref, gate_ref, o_ref, ks_ref, kw_ref):
    seq = ks_ref.shape[0]
    tq, tk, rep, d = NSA_TQ, NSA_TK, NSA_REP, HEAD_DIM
    rows = rep * tq
    g = pl.program_id(1)
    t0 = pl.program_id(2) * tq

    @pl.when(pl.program_id(2) == 0)
    def _():
        ks_ref[:, 0:d] = ksel_ref[...]
        ks_ref[:, d:2 * d] = kaux_ref[...]
        kw_ref[:, 0:d] = kwin_ref[...]
        kw_ref[:, d:2 * d] = kaux_ref[...]

    q = q_ref[...] * (d ** -0.5)
    q = jnp.concatenate([q[:, r * d:(r + 1) * d] for r in range(rep)], axis=0)
    qs = q.astype(BF16)
    qs2 = (q * LOG2E).astype(BF16)
    slope = jnp.concatenate([jnp.full((tq, 1), slopes_ref[g * rep + r], F32) for r in range(rep)], axis=0)
    row = lax.broadcasted_iota(I32, (rows, 1), 0)
    t_col = t0 + (row & (tq - 1))

    ncp = kc_ref.shape[0]
    n_idx = lax.broadcasted_iota(I32, (rows, ncp), 1)
    dist_c = t_col - (n_idx * CMP_STRIDE + (CMP_BLOCK - 1))
    s_c = lax.dot_general(qs, kc_ref[...], NT_DIMS, preferred_element_type=F32)
    s_c = s_c - slope * dist_c.astype(F32)
    p_c = _softmax_rows(s_c, dist_c >= 0).astype(BF16)
    o_c = jnp.dot(p_c, vc_ref[...], preferred_element_type=F32)

    n_sel = seq // SEL_BLOCK
    ov_t = ov_ref[...]
    imp = jnp.zeros((n_sel, tq), F32)
    for r in range(rep):
        imp += lax.dot_general(ov_t, p_c[r * tq:(r + 1) * tq, :], NT_DIMS, preferred_element_type=F32)
    blk = lax.broadcasted_iota(I32, (n_sel, tq), 0)
    cur = lax.shift_right_logical(t0 + lax.broadcasted_iota(I32, (n_sel, tq), 1), SEL_BLOCK.bit_length() - 1)
    forced = (blk == 0) | ((cur - blk) < SEL_LOCAL)
    imp = jnp.where(forced, jnp.inf, imp)
    imp = jnp.where(blk <= cur, imp, NEG_INF)
    rank = jnp.zeros((n_sel, tq), I32)
    for i in range(n_sel):
        ri = imp[i:i + 1, :]
        rank += ((ri > imp) | ((ri == imp) & (blk > i))).astype(I32)
    sel_t = (rank < min(SEL_TOPN, n_sel)).astype(F32)

    to_lanes = (lax.broadcasted_iota(I32, (n_sel, LANES), 0)
                == lax.broadcasted_iota(I32, (n_sel, LANES), 1)).astype(BF16)
    picked = lax.dot_general(sel_t.astype(BF16), to_lanes, TN_DIMS, preferred_element_type=F32)
    lane = lax.broadcasted_iota(I32, (tq, LANES), 1)
    sel_bias = jnp.where(lane < AUX_SEL_LANES, (picked - 1.0) * SEL_BIAS, 0.0)
    qaux = qaux_ref[...]
    qx_sel = jnp.concatenate(
        [qs2, jnp.concatenate([sel_bias + qaux[r:r + 1, :] for r in range(rep)], axis=0).astype(BF16)], axis=1)
    qx_win = jnp.concatenate(
        [qs2, jnp.concatenate([jnp.broadcast_to(qaux[r:r + 1, :], (tq, LANES)) for r in range(rep)],
                              axis=0).astype(BF16)], axis=1)

    def sel_tile(j, carry, diagonal):
        m, l, acc = carry
        k0 = pl.multiple_of(j * tk, tk)
        s = lax.dot_general(qx_sel, ks_ref[pl.ds(k0, tk), :], NT_DIMS, preferred_element_type=F32)
        if diagonal:
            ahead = (lax.broadcasted_iota(I32, (rows, tk), 1)
                     - (lax.broadcasted_iota(I32, (rows, tk), 0) & (tq - 1)))
            s = jnp.where(ahead <= t0 - k0, s, -SEL_BIAS)
        m_new = jnp.maximum(m, jnp.max(s, axis=-1, keepdims=True))
        alpha = jnp.exp2(m - m_new)
        p = jnp.exp2(s - m_new)
        l = alpha * l + jnp.sum(p, axis=-1, keepdims=True)
        acc = alpha * acc + jnp.dot(p.astype(BF16), vs_ref[pl.ds(k0, tk), :], preferred_element_type=F32)
        return m_new, l, acc

    n_tiles = (t0 + tq + tk - 1) // tk
    init = (jnp.full((rows, 1), NEG_INF, F32), jnp.zeros((rows, 1), F32), jnp.zeros((rows, d), F32))
    carry = lax.fori_loop(0, n_tiles - 1, lambda j, c: sel_tile(j, c, False), init)
    _, l_s, acc_s = sel_tile(n_tiles - 1, carry, True)
    o_s = acc_s * (1.0 / l_s)

    wk = tq + WINDOW
    start = pl.multiple_of(jnp.maximum(t0 - WINDOW, 0), tq)
    s_w = lax.dot_general(qx_win, kw_ref[pl.ds(start, wk), :], NT_DIMS, preferred_element_type=F32)
    dist_w = t_col - (start + lax.broadcasted_iota(I32, (rows, wk), 1))
    s_w = jnp.where((dist_w >= 0) & (dist_w < WINDOW), s_w, NEG_INF)
    e_w = jnp.exp2(s_w - jnp.max(s_w, axis=-1, keepdims=True))
    d_w = jnp.sum(e_w, axis=-1, keepdims=True)
    o_w = jnp.dot(e_w.astype(BF16), vw_ref[pl.ds(start, wk), :], preferred_element_type=F32) * (1.0 / d_w)

    gt = jax.nn.sigmoid(gate_ref[...])
    for r in range(rep):
        sl = slice(r * tq, (r + 1) * tq)
        o = (gt[:, 3 * r:3 * r + 1] * o_c[sl] + gt[:, 3 * r + 1:3 * r + 2] * o_s[sl]
             + gt[:, 3 * r + 2:3 * r + 3] * o_w[sl])
        o_ref[:, r * d:(r + 1) * d] = o.astype(o_ref.dtype)


def _nsa_attention(qa, q_col0, cmp, kvb, gates, slopes, seq):
    bsz = qa.shape[0]
    g, rep, d, tq = NSA_KV_HEADS, NSA_REP, HEAD_DIM, NSA_TQ
    ncp = seq // CMP_STRIDE
    n_sel = seq // SEL_BLOCK
    assert n_sel <= AUX_SEL_LANES and seq % NSA_TK == 0 and seq >= tq + WINDOW
    qblk0 = q_col0 // (rep * d)
    kv_spec = lambda s: pl.BlockSpec((None, seq, d), lambda b, h, i: (b, 0, s * NSA_KV_HEADS + h))
    cmp_spec = lambda s: pl.BlockSpec((None, None, None, ncp, d), lambda b, h, i: (s, b, h, 0, 0))
    return pl.pallas_call(
        _nsa_kernel,
        out_shape=jax.ShapeDtypeStruct((bsz, seq, NSA_DQ), BF16),
        grid=(bsz, g, seq // tq),
        in_specs=[pl.BlockSpec(memory_space=pltpu.SMEM),
                  pl.BlockSpec((None, rep, LANES), lambda b, h, i: (h, 0, 0)),
                  pl.BlockSpec((seq, LANES), lambda b, h, i: (0, 0)),
                  pl.BlockSpec((n_sel, ncp), lambda b, h, i: (0, 0)),
                  pl.BlockSpec((None, tq, rep * d), lambda b, h, i: (b, i, qblk0 + h)),
                  cmp_spec(0), cmp_spec(1), kv_spec(0), kv_spec(1), kv_spec(2), kv_spec(3),
                  pl.BlockSpec((None, None, tq, 3 * rep), lambda b, h, i: (b, h, i, 0))],
        out_specs=pl.BlockSpec((None, tq, rep * d), lambda b, h, i: (b, i, h)),
        scratch_shapes=[pltpu.VMEM((seq, 2 * d), BF16), pltpu.VMEM((seq, 2 * d), BF16)],
        compiler_params=_params(("parallel", "parallel", "arbitrary")),
        name="nsa_attention",
    )(slopes, _nsa_query_aux(slopes * LOG2E), _nsa_key_aux(seq), _nsa_overlap(n_sel, ncp), qa, cmp, cmp, kvb, kvb, kvb, kvb, gates)


def _split_columns(proj):
    out, start = [], 0
    for size in IN_SIZES:
        out.append(proj[..., start:start + size])
        start += size
    return out


def kernel(x, c, ada_w, ada_b, norm1_w, w_in, gdn_conv_w, gdn_a_log, gdn_dt_bias, gdn_norm_w,
           cmp_pos_k, cmp_w1_k, cmp_w2_k, cmp_pos_v, cmp_w1_v, cmp_w2_v, w_out, norm2_w,
           w_up, w_down, final_norm_w):
    bsz, seq, d = x.shape
    depth = ada_w.shape[0]
    m = bsz * seq
    x2 = x.reshape(m, d)
    slopes = 2.0 ** (-8.0 * jnp.arange(1, NSA_HEADS + 1, dtype=F32) / NSA_HEADS)
    for i in range(depth):
        mod = _modulation(c, ada_w[i], ada_b[i])
        sh1, sc1, g1, sh2, sc2, g2 = jnp.split(mod, 6, axis=-1)
        h = _norm_mod(x2, norm1_w[i], sc1, sh1, seq, BF16)
        cols = _split_columns(w_in[i])
        small = jnp.concatenate([cols[4], cols[5], cols[13]], axis=-1)
        small = jnp.pad(small, ((0, 0), (0, LANES - small.shape[1])))
        w_a = jnp.concatenate(cols[0:4] + [cols[6]], axis=-1).astype(BF16)
        w_b = jnp.concatenate(cols[7:9] + [small], axis=-1).astype(BF16)
        w_c = jnp.concatenate(cols[9:13], axis=-1).astype(BF16)
        pa, w_up_bf16 = _matmul(h, w_a, tm=MM_TILE, tn=MM_TILE, out_dtype=F32, name="in_proj_a", cast_to_bf16=w_up[i])
        pa = pa.reshape(bsz, seq, -1)
        pb = _matmul(h, w_b, tm=MM_TILE, tn=w_b.shape[1], out_dtype=F32, name="in_proj_b").reshape(bsz, seq, -1)
        pc, w_out_bf16 = _matmul(h, w_c, tm=MM_TILE, tn=MM_TILE, out_dtype=BF16, name="in_proj_c",
                                 cast_to_bf16=w_out[i])
        pc = pc.reshape(bsz, seq, -1)

        o_a = _gdn(pa, pb, 2 * NSA_DKV // LANES, gdn_conv_w[i], gdn_a_log[i], gdn_dt_bias[i], gdn_norm_w[i], seq)
        ngate = pb[..., 2 * NSA_DKV + 2 * GDN_HEADS:2 * NSA_DKV + 2 * GDN_HEADS + 3 * NSA_HEADS]

        cmp = _compress(pb, jnp.stack([cmp_pos_k[i], cmp_pos_v[i]]), jnp.stack([cmp_w1_k[i], cmp_w1_v[i]]),
                        jnp.stack([cmp_w2_k[i], cmp_w2_v[i]]), seq)
        gates = ngate.reshape(bsz, seq, NSA_KV_HEADS, 3 * NSA_REP).transpose(0, 2, 1, 3)
        o_b = _nsa_attention(pa, 4 * GDN_DK, cmp, pc, gates, slopes, seq)

        x2 = _matmul_parts_residual([o_a.reshape(m, -1), o_b.reshape(m, -1)], w_out_bf16, x2, g1, seq,
                                    tm=MM_TILE, tn=MM_TILE, name="out_proj")
        h2 = _norm_mod(x2, norm2_w[i], sc2, sh2, seq, BF16)
        hid, w_down_bf16 = _matmul(h2, w_up_bf16, tm=MM_TILE, tn=MM_TILE, out_dtype=BF16, relu2=True, name="mlp_up",
                                   cast_to_bf16=w_down[i])
        x2 = _matmul_residual(hid, w_down_bf16, x2, g2, seq,
                              tm=MM_TILE, tn=MM_TILE, tk=MM_TILE_K, name="mlp_down")
    return _final_norm(x2, final_norm_w).reshape(bsz, seq, d)
```
